```python
import math
import jax
import jax.numpy as jnp
from jax import lax
import numpy as np

D_MODEL = 1024
BATCH = 8
SEQ = 2048
DEPTH = 2

N_META = 16
BLOCK_Q = 128
MIX_WIDTH = D_MODEL // 2
V_HEAD_DIM = 64
MLA_HEADS = MIX_WIDTH // V_HEAD_DIM
QK_NOPE_DIM = 64
QK_ROPE_DIM = 32
QK_HEAD_DIM = QK_NOPE_DIM + QK_ROPE_DIM
Q_LORA_RANK = 3 * D_MODEL // 8
KV_LORA_RANK = D_MODEL // 4
ROPE_BASE = 10000.0
CONV_WIDTH = MIX_WIDTH
CONV_K = 3
S5_WIDTH = MIX_WIDTH
S5_GROUP = 16
S5_GROUPS = S5_WIDTH // S5_GROUP
S5_STATE = 64
N_BRANCH = 3
D_FF = 128 * ((8 * D_MODEL // 3 + 127) // 128)
ALPHA = (2.0 * DEPTH) ** 0.25
BETA = (8.0 * DEPTH) ** -0.25
LN_EPS = 1e-5
RMS_EPS = 1e-6
IN_SPLITS = (Q_LORA_RANK, KV_LORA_RANK, QK_ROPE_DIM, CONV_WIDTH, CONV_WIDTH, CONV_WIDTH, S5_WIDTH, N_BRANCH * D_MODEL)
D_IN = sum(IN_SPLITS)

kernel_name = "hybrid_mla_shortconv_s5_deepnorm_macaron"


def layer_norm(x, g, b):
    xf = x.astype(jnp.float32)
    mu = jnp.mean(xf, axis=-1, keepdims=True)
    var = jnp.mean(jnp.square(xf - mu), axis=-1, keepdims=True)
    y = (xf - mu) * lax.rsqrt(var + LN_EPS) * g.astype(jnp.float32) + b.astype(jnp.float32)
    return y.astype(x.dtype)


def rms_norm(x, g):
    xf = x.astype(jnp.float32)
    y = xf * lax.rsqrt(jnp.mean(jnp.square(xf), axis=-1, keepdims=True) + RMS_EPS) * g.astype(jnp.float32)
    return y.astype(x.dtype)


def swiglu(x, w_gate, w_up, w_down):
    return (jax.nn.silu(x @ w_gate) * (x @ w_up)) @ w_down


def rope(x, pos):
    d = x.shape[-1]
    inv_freq = ROPE_BASE ** (-jnp.arange(0, d, 2, dtype=jnp.float32) / d)
    ang = pos.astype(jnp.float32)[:, None] * inv_freq[None, :]
    cos = jnp.cos(ang)[None, :, None, :]
    sin = jnp.sin(ang)[None, :, None, :]
    xf = x.astype(jnp.float32)
    x1, x2 = xf[..., : d // 2], xf[..., d // 2:]
    return jnp.concatenate([x1 * cos - x2 * sin, x2 * cos + x1 * sin], axis=-1).astype(x.dtype)


def mla_branch(c_q_raw, c_kv_raw, k_rope_raw, q_norm_g, w_uq, kv_norm_g, w_ukv, w_out):
    B, L, _ = c_q_raw.shape
    pos = jnp.arange(L)
    q = (rms_norm(c_q_raw, q_norm_g) @ w_uq).reshape(B, L, MLA_HEADS, QK_HEAD_DIM)
    kv = (rms_norm(c_kv_raw, kv_norm_g) @ w_ukv).reshape(B, L, MLA_HEADS, QK_NOPE_DIM + V_HEAD_DIM)
    q = jnp.concatenate([q[..., :QK_NOPE_DIM], rope(q[..., QK_NOPE_DIM:], pos)], axis=-1)
    k_rope = rope(k_rope_raw[:, :, None, :], pos)
    k = jnp.concatenate([kv[..., :QK_NOPE_DIM], jnp.broadcast_to(k_rope, (B, L, MLA_HEADS, QK_ROPE_DIM))], axis=-1)
    v = kv[..., QK_NOPE_DIM:]
    pad = (-L) % BLOCK_Q
    padw = ((0, 0), (pad, 0), (0, 0), (0, 0))
    q, k, v = jnp.pad(q, padw), jnp.pad(k, padw), jnp.pad(v, padw)
    n_blocks = (L + pad) // BLOCK_Q
    scale = QK_HEAD_DIM ** -0.5
    outs = []
    for i in range(n_blocks):
        kend = (i + 1) * BLOCK_Q
        qs = q[:, i * BLOCK_Q: kend]
        s = jnp.einsum('bqhd,bkhd->bhqk', qs, k[:, :kend]).astype(jnp.float32) * scale
        qi = i * BLOCK_Q + jnp.arange(BLOCK_Q)
        ki = jnp.arange(kend)
        mask = (ki[None, :] <= qi[:, None]) & (ki[None, :] >= pad)
        s = jnp.where(mask[None, None], s, -1e30)
        p = jax.nn.softmax(s, axis=-1).astype(v.dtype)
        outs.append(jnp.einsum('bhqk,bkhd->bqhd', p, v[:, :kend]))
    o = jnp.concatenate(outs, axis=1)[:, pad:]
    return o.reshape(B, L, MLA_HEADS * V_HEAD_DIM) @ w_out


def short_conv_branch(xbar, b_gate, c_gate, conv_w, conv_b, w_out):
    L = xbar.shape[1]
    u = c_gate * xbar
    up = jnp.pad(u, ((0, 0), (CONV_K - 1, 0), (0, 0)))
    y = conv_b + sum(conv_w[j] * up[:, j: j + L] for j in range(CONV_K))
    return (b_gate * y) @ w_out


def _ssm_combine(e1, e2):
    a1r, a1i, b1r, b1i = e1
    a2r, a2i, b2r, b2i = e2
    ar = a2r * a1r - a2i * a1i
    ai = a2r * a1i + a2i * a1r
    br = a2r * b1r - a2i * b1i + b2r
    bi = a2r * b1i + a2i * b1r + b2i
    return (ar, ai, br, bi)


def s5_branch(u, a_re, a_im, log_dt, b_re, b_im, c_re, c_im, d, w_glu, b_glu, w_out):
    B, L, _ = u.shape
    f32 = jnp.float32
    uf = u.astype(f32).reshape(B, L, S5_GROUPS, S5_GROUP)
    a_re, a_im = a_re.astype(f32), a_im.astype(f32)
    dt = jnp.exp(log_dt.astype(f32))[:, None]
    mag = jnp.exp(dt * a_re)
    ab_re, ab_im = mag * jnp.cos(dt * a_im), mag * jnp.sin(dt * a_im)
    den = a_re * a_re + a_im * a_im
    nr, ni = ab_re - 1.0, ab_im
    coef_re = (nr * a_re + ni * a_im) / den
    coef_im = (ni * a_re - nr * a_im) / den
    b_re, b_im = b_re.astype(f32), b_im.astype(f32)
    bb_re = coef_re[..., None] * b_re - coef_im[..., None] * b_im
    bb_im = coef_re[..., None] * b_im + coef_im[..., None] * b_re
    bu_re = jnp.einsum('gnh,blgh->blgn', bb_re, uf)
    bu_im = jnp.einsum('gnh,blgh->blgn', bb_im, uf)
    shp = (1, L, S5_GROUPS, S5_STATE)
    elems = (jnp.broadcast_to(ab_re[None, None], shp), jnp.broadcast_to(ab_im[None, None], shp), bu_re, bu_im)
    _, _, xr, xi = lax.associative_scan(_ssm_combine, elems, axis=1)
    y = (jnp.einsum('ghn,blgn->blgh', c_re.astype(f32), xr)
         - jnp.einsum('ghn,blgn->blgh', c_im.astype(f32), xi)
         + d.astype(f32) * uf)
    y = jax.nn.gelu(y.reshape(B, L, S5_WIDTH).astype(u.dtype))
    y = y * jax.nn.sigmoid(y @ w_glu + b_glu)
    return y @ w_out


def hybrid_layer(x, ffn1_w_gate, ffn1_w_up, ffn1_w_down, ln1_g, ln1_b, w_in,
                 mla_q_norm_g, mla_w_uq, mla_kv_norm_g, mla_w_ukv, mla_w_o,
                 conv_w, conv_b, conv_w_out,
                 s5_a_re, s5_a_im, s5_log_dt, s5_b_re, s5_b_im, s5_c_re, s5_c_im, s5_d, s5_w_glu, s5_b_glu, s5_w_out,
                 w_o, ln2_g, ln2_b, ffn2_w_gate, ffn2_w_up, ffn2_w_down, ln3_g, ln3_b):
    B, L, _ = x.shape
    x = layer_norm(ALPHA * x + 0.5 * swiglu(x, ffn1_w_gate, ffn1_w_up, ffn1_w_down), ln1_g, ln1_b)
    proj = x @ w_in
    c_q, c_kv, k_rope, xbar, b_gate, c_gate, u_s5, gates = jnp.split(proj, np.cumsum(IN_SPLITS)[:-1].tolist(), axis=-1)
    y_a = mla_branch(c_q, c_kv, k_rope, mla_q_norm_g, mla_w_uq, mla_kv_norm_g, mla_w_ukv, mla_w_o)
    y_b = short_conv_branch(xbar, b_gate, c_gate, conv_w, conv_b, conv_w_out)
    y_c = s5_branch(u_s5, s5_a_re, s5_a_im, s5_log_dt, s5_b_re, s5_b_im, s5_c_re, s5_c_im, s5_d, s5_w_glu, s5_b_glu, s5_w_out)
    g = jax.nn.sigmoid(gates.reshape(B, L, N_BRANCH, D_MODEL))
    mixed = g[:, :, 0] * y_a + g[:, :, 1] * y_b + g[:, :, 2] * y_c
    x = layer_norm(ALPHA * x + mixed @ w_o, ln2_g, ln2_b)
    x = layer_norm(ALPHA * x + 0.5 * swiglu(x, ffn2_w_gate, ffn2_w_up, ffn2_w_down), ln3_g, ln3_b)
    return x


def setup_inputs(seed: int = 0) -> dict:
    key = jax.random.key(seed)
    keys = iter(jax.random.split(key, 48))
    f32 = jnp.float32

    def nrm(shape, scale):
        return jax.random.normal(next(keys), shape, f32) * scale

    def gain(shape):
        return 1.0 + nrm(shape, 0.01)

    Dp = DEPTH
    n_idx = jnp.arange(S5_STATE, dtype=f32)
    inp = {}
    inp["x"] = nrm((BATCH, SEQ, D_MODEL), 1.0)
    inp["meta"] = nrm((N_META, D_MODEL), 1.0)
    inp["ffn1_w_gate"] = nrm((Dp, D_MODEL, D_FF), D_MODEL ** -0.5)
    inp["ffn1_w_up"] = nrm((Dp, D_MODEL, D_FF), D_MODEL ** -0.5)
    inp["ffn1_w_down"] = nrm((Dp, D_FF, D_MODEL), BETA * D_FF ** -0.5)
    inp["ln1_g"] = gain((Dp, D_MODEL))
    inp["ln1_b"] = nrm((Dp, D_MODEL), 0.01)
    inp["w_in"] = nrm((Dp, D_MODEL, D_IN), D_MODEL ** -0.5)
    inp["mla_q_norm_g"] = gain((Dp, Q_LORA_RANK))
    inp["mla_w_uq"] = nrm((Dp, Q_LORA_RANK, MLA_HEADS * QK_HEAD_DIM), Q_LORA_RANK ** -0.5)
    inp["mla_kv_norm_g"] = gain((Dp, KV_LORA_RANK))
    inp["mla_w_ukv"] = nrm((Dp, KV_LORA_RANK, MLA_HEADS * (QK_NOPE_DIM + V_HEAD_DIM)), KV_LORA_RANK ** -0.5)
    inp["mla_w_o"] = nrm((Dp, MLA_HEADS * V_HEAD_DIM, D_MODEL), (MLA_HEADS * V_HEAD_DIM) ** -0.5)
    inp["conv_w"] = nrm((Dp, CONV_K, CONV_WIDTH), CONV_K ** -0.5)
    inp["conv_b"] = nrm((Dp, CONV_WIDTH), 0.01)
    inp["conv_w_out"] = nrm((Dp, CONV_WIDTH, D_MODEL), CONV_WIDTH ** -0.5)
    inp["s5_a_re"] = -0.5 * jnp.exp(nrm((Dp, S5_GROUPS, S5_STATE), 0.01))
    inp["s5_a_im"] = math.pi * n_idx + nrm((Dp, S5_GROUPS, S5_STATE), 0.01)
    inp["s5_log_dt"] = jax.random.uniform(next(keys), (Dp, S5_GROUPS), f32, math.log(1e-3), math.log(1e-1))
    inp["s5_b_re"] = nrm((Dp, S5_GROUPS, S5_STATE, S5_GROUP), (2 * S5_GROUP) ** -0.5)
    inp["s5_b_im"] = nrm((Dp, S5_GROUPS, S5_STATE, S5_GROUP), (2 * S5_GROUP) ** -0.5)
    inp["s5_c_re"] = nrm((Dp, S5_GROUPS, S5_GROUP, S5_STATE), (2 * S5_STATE) ** -0.5)
    inp["s5_c_im"] = nrm((Dp, S5_GROUPS, S5_GROUP, S5_STATE), (2 * S5_STATE) ** -0.5)
    inp["s5_d"] = nrm((Dp, S5_GROUPS, S5_GROUP), 1.0)
    inp["s5_w_glu"] = nrm((Dp, S5_WIDTH, S5_WIDTH), S5_WIDTH ** -0.5)
    inp["s5_b_glu"] = nrm((Dp, S5_WIDTH), 0.01)
    inp["s5_w_out"] = nrm((Dp, S5_WIDTH, D_MODEL), S5_WIDTH ** -0.5)
    inp["w_o"] = nrm((Dp, D_MODEL, D_MODEL), BETA * D_MODEL ** -0.5)
    inp["ln2_g"] = gain((Dp, D_MODEL))
    inp["ln2_b"] = nrm((Dp, D_MODEL), 0.01)
    inp["ffn2_w_gate"] = nrm((Dp, D_MODEL, D_FF), D_MODEL ** -0.5)
    inp["ffn2_w_up"] = nrm((Dp, D_MODEL, D_FF), D_MODEL ** -0.5)
    inp["ffn2_w_down"] = nrm((Dp, D_FF, D_MODEL), BETA * D_FF ** -0.5)
    inp["ln3_g"] = gain((Dp, D_MODEL))
    inp["ln3_b"] = nrm((Dp, D_MODEL), 0.01)
    return inp


def reference(x, meta, ffn1_w_gate, ffn1_w_up, ffn1_w_down, ln1_g, ln1_b, w_in,
              mla_q_norm_g, mla_w_uq, mla_kv_norm_g, mla_w_ukv, mla_w_o,
              conv_w, conv_b, conv_w_out,
              s5_a_re, s5_a_im, s5_log_dt, s5_b_re, s5_b_im, s5_c_re, s5_c_im, s5_d, s5_w_glu, s5_b_glu, s5_w_out,
              w_o, ln2_g, ln2_b, ffn2_w_gate, ffn2_w_up, ffn2_w_down, ln3_g, ln3_b):
    B = x.shape[0]
    h = jnp.concatenate([jnp.broadcast_to(meta[None].astype(x.dtype), (B, N_META, D_MODEL)), x], axis=1)
    for i in range(DEPTH):
        h = hybrid_layer(h, ffn1_w_gate[i], ffn1_w_up[i], ffn1_w_down[i], ln1_g[i], ln1_b[i], w_in[i],
                         mla_q_norm_g[i], mla_w_uq[i], mla_kv_norm_g[i], mla_w_ukv[i], mla_w_o[i],
                         conv_w[i], conv_b[i], conv_w_out[i],
                         s5_a_re[i], s5_a_im[i], s5_log_dt[i], s5_b_re[i], s5_b_im[i], s5_c_re[i], s5_c_im[i],
                         s5_d[i], s5_w_glu[i], s5_b_glu[i], s5_w_out[i],
                         w_o[i], ln2_g[i], ln2_b[i], ffn2_w_gate[i], ffn2_w_up[i], ffn2_w_down[i], ln3_g[i], ln3_b[i])
    return h[:, N_META:]
```

```python
import functools
import math

import jax
import jax.numpy as jnp
from jax import lax
from jax.experimental import pallas as pl
from jax.experimental.pallas import tpu as pltpu

D_MODEL = 1024
BATCH = 8
SEQ = 2048
DEPTH = 2
N_META = 16
BLOCK_Q = 128
MIX_WIDTH = D_MODEL // 2
HEADS = 8
V_HEAD_DIM = 64
QK_NOPE_DIM = 64
QK_ROPE_DIM = 32
QK_HEAD_DIM = QK_NOPE_DIM + QK_ROPE_DIM
Q_LORA_RANK = 384
KV_LORA_RANK = 256
ROPE_BASE = 10000.0
S5_GROUP = 16
S5_GROUPS = 32
S5_STATE = 64
D_FF = 2816
ALPHA = (2.0 * DEPTH) ** 0.25
LN_EPS = 1e-5
RMS_EPS = 1e-6

HEAD_LANES = 128
QKV_WIDTH = HEADS * HEAD_LANES
PAD = (-(N_META + SEQ)) % BLOCK_Q
LP = PAD + N_META + SEQ
ROWS = LP * BATCH
S5_BLOCKS = 4
S5_BLOCK_STATES = (S5_GROUPS // S5_BLOCKS) * S5_STATE
S5_COLS = 2 * S5_GROUPS * S5_STATE

_OFF_CQ, _OFF_CKV, _OFF_KR, _OFF_CONV = 0, 384, 640, 672
_OFF_GATES = _OFF_CONV + 4 * MIX_WIDTH
D_IN = _OFF_GATES + 3 * D_MODEL

V7X_VMEM_BYTES = 64 * 1024 * 1024
VMEM_LIMIT = 56 * 1024 * 1024

F32 = jnp.float32
BF16 = jnp.bfloat16


def _dot(a, b):
    return jnp.dot(a, b, preferred_element_type=F32)


def _const_spec(shape):
    nd = len(shape)
    return pl.BlockSpec(shape, lambda *_: (0,) * nd, pipeline_mode=pl.Buffered(1))


def _layer_norm(y, g, b):
    mu = jnp.mean(y, axis=-1, keepdims=True)
    yc = y - mu
    var = jnp.mean(yc * yc, axis=-1, keepdims=True)
    return yc * lax.rsqrt(var + LN_EPS) * g + b


def _rms_norm(y, g):
    return y * lax.rsqrt(jnp.mean(y * y, axis=-1, keepdims=True) + RMS_EPS) * g


def _sigmoid(y):
    return 1.0 / (1.0 + jnp.exp(-y))


def _gelu_tanh(y):
    return 0.5 * y * (1.0 + jnp.tanh(math.sqrt(2.0 / math.pi) * (y + 0.044715 * (y * y * y))))


def _ffn_ln_kernel(x_ref, wg_ref, wu_ref, wd_ref, g_ref, b_ref, o_ref, *, ff_chunk):
    x = x_ref[...]
    xb = x.astype(BF16)
    acc = None
    for c in range(D_FF // ff_chunk):
        sl = slice(c * ff_chunk, (c + 1) * ff_chunk)
        gate = _dot(xb, wg_ref[:, sl])
        up = _dot(xb, wu_ref[:, sl])
        hmid = (gate * _sigmoid(gate) * up).astype(BF16)
        part = _dot(hmid, wd_ref[sl, :])
        acc = part if acc is None else acc + part
    o_ref[...] = _layer_norm(ALPHA * x + 0.5 * acc, g_ref[...], b_ref[...])


def _ffn_ln(x, wg, wu, wd, g, b, *, tm=512, ff_chunk=1408):
    rows = x.shape[0]
    return pl.pallas_call(
        functools.partial(_ffn_ln_kernel, ff_chunk=ff_chunk),
        grid=(rows // tm,),
        in_specs=[
            pl.BlockSpec((tm, D_MODEL), lambda i: (i, 0)),
            _const_spec((D_MODEL, D_FF)),
            _const_spec((D_MODEL, D_FF)),
            _const_spec((D_FF, D_MODEL)),
            _const_spec((1, D_MODEL)),
            _const_spec((1, D_MODEL)),
        ],
        out_specs=pl.BlockSpec((tm, D_MODEL), lambda i: (i, 0)),
        out_shape=jax.ShapeDtypeStruct((rows, D_MODEL), F32),
        compiler_params=pltpu.CompilerParams(
            dimension_semantics=("arbitrary",), vmem_limit_bytes=VMEM_LIMIT),
        name="ffn_ln",
    )(x, wg, wu, wd, g, b)


_WC_WIDTH = Q_LORA_RANK + KV_LORA_RANK + 2 * HEAD_LANES


def _qkv_kernel(x_ref, wc_ref, gq_ref, gkv_ref, wqa_ref, wqb_ref, wuk_ref, wuv_ref, tab_ref,
                q_ref, k_ref, v_ref):
    xb = x_ref[...].astype(BF16)
    c = _dot(xb, wc_ref[...])
    c_q = c[:, :Q_LORA_RANK]
    c_kv = c[:, Q_LORA_RANK:Q_LORA_RANK + KV_LORA_RANK]
    k_r = c[:, _WC_WIDTH - 2 * HEAD_LANES:_WC_WIDTH - HEAD_LANES]
    k_r_rot = c[:, _WC_WIDTH - HEAD_LANES:]
    qn = _rms_norm(c_q, gq_ref[...]).astype(BF16)
    kvn = _rms_norm(c_kv, gkv_ref[...]).astype(BF16)
    cos_q = tab_ref[:, 0 * HEAD_LANES:1 * HEAD_LANES]
    sin_q = tab_ref[:, 1 * HEAD_LANES:2 * HEAD_LANES]
    cos_k = tab_ref[:, 2 * HEAD_LANES:3 * HEAD_LANES]
    sin_k = tab_ref[:, 3 * HEAD_LANES:4 * HEAD_LANES]
    q_a = _dot(qn, wqa_ref[...])
    q_b = _dot(qn, wqb_ref[...])
    k_nope = _dot(kvn, wuk_ref[...])
    k_rope = k_r * cos_k + k_r_rot * sin_k
    for h in range(HEADS):
        sl = slice(h * HEAD_LANES, (h + 1) * HEAD_LANES)
        q_ref[:, sl] = (q_a[:, sl] * cos_q + q_b[:, sl] * sin_q).astype(BF16)
        k_ref[:, sl] = (k_nope[:, sl] + k_rope).astype(BF16)
    v_ref[...] = _dot(kvn, wuv_ref[...]).astype(BF16)


def _qkv(xn, wc, gq, gkv, wqa, wqb, wuk, wuv, tab, *, tm=512):
    rows = xn.shape[0]
    row_spec = lambda w: pl.BlockSpec((tm, w), lambda i: (i, 0))
    out = jax.ShapeDtypeStruct((rows, QKV_WIDTH), BF16)
    return pl.pallas_call(
        _qkv_kernel,
        grid=(rows // tm,),
        in_specs=[
            row_spec(D_MODEL),
            _const_spec((D_MODEL, _WC_WIDTH)),
            _const_spec((1, Q_LORA_RANK)),
            _const_spec((1, KV_LORA_RANK)),
            _const_spec((Q_LORA_RANK, QKV_WIDTH)),
            _const_spec((Q_LORA_RANK, QKV_WIDTH)),
            _const_spec((KV_LORA_RANK, QKV_WIDTH)),
            _const_spec((KV_LORA_RANK, QKV_WIDTH)),
            row_spec(4 * HEAD_LANES),
        ],
        out_specs=[row_spec(QKV_WIDTH)] * 3,
        out_shape=[out, out, out],
        compiler_params=pltpu.CompilerParams(
            dimension_semantics=("arbitrary",), vmem_limit_bytes=VMEM_LIMIT),
        name="qkv_proj",
    )(xn, wc, gq, gkv, wqa, wqb, wuk, wuv, tab)


def _attn_kernel(q_ref, k_ref, v_ref, o_ref):
    i = pl.program_id(1)
    q_pos = i * BLOCK_Q + lax.broadcasted_iota(jnp.int32, (BLOCK_Q, BLOCK_Q), 0)
    k_off = lax.broadcasted_iota(jnp.int32, (BLOCK_Q, BLOCK_Q), 1)
    outs = []
    for h in range(HEADS):
        sl = slice(h * HEAD_LANES, (h + 1) * HEAD_LANES)
        q = q_ref[:, sl]

        def body(j, carry, sl=sl, q=q):
            m, l, acc = carry
            r0 = pl.multiple_of(j * BLOCK_Q, BLOCK_Q)
            k = k_ref[pl.ds(r0, BLOCK_Q), sl]
            v = v_ref[pl.ds(r0, BLOCK_Q), sl]
            s = lax.dot_general(q, k, (((1,), (1,)), ((), ())), preferred_element_type=F32)
            k_pos = j * BLOCK_Q + k_off
            s = jnp.where((k_pos <= q_pos) & (k_pos >= PAD), s, -1e30)
            m_new = jnp.maximum(m, jnp.max(s, axis=-1, keepdims=True))
            a = jnp.exp(m - m_new)
            p = jnp.exp(s - m_new)
            l_new = a * l + jnp.sum(p, axis=-1, keepdims=True)
            acc_new = a * acc + _dot(p.astype(BF16), v)
            return m_new, l_new, acc_new

        init = (jnp.full((BLOCK_Q, 1), -1e30, F32), jnp.zeros((BLOCK_Q, 1), F32),
                jnp.zeros((BLOCK_Q, HEAD_LANES), F32))
        _, l, acc = lax.fori_loop(0, i + 1, body, init)
        outs.append(acc / l)
    for p in range(HEADS // 2):
        o_ref[:, p * HEAD_LANES:(p + 1) * HEAD_LANES] = (outs[2 * p] + outs[2 * p + 1]).astype(BF16)


def _attention(q, k, v):
    return pl.pallas_call(
        _attn_kernel,
        grid=(BATCH, LP // BLOCK_Q),
        in_specs=[
            pl.BlockSpec((None, BLOCK_Q, QKV_WIDTH), lambda b, i: (b, i, 0)),
            pl.BlockSpec((None, LP, QKV_WIDTH), lambda b, i: (b, 0, 0)),
            pl.BlockSpec((None, LP, QKV_WIDTH), lambda b, i: (b, 0, 0)),
        ],
        out_specs=pl.BlockSpec((None, BLOCK_Q, MIX_WIDTH), lambda b, i: (b, i, 0)),
        out_shape=jax.ShapeDtypeStruct((BATCH, LP, MIX_WIDTH), BF16),
        compiler_params=pltpu.CompilerParams(
            dimension_semantics=("arbitrary", "arbitrary"), vmem_limit_bytes=VMEM_LIMIT),
        name="mla_attention",
    )(q, k, v)


_MIX_IN = 4 * MIX_WIDTH + 3 * D_MODEL
_HALO = 2 * BATCH


def _mix_kernel(x_ref, att_ref, win_ref, cw_ref, cb_ref, cwo_ref, bblk_ref, ar_ref, ai_ref,
                cblk_ref, d_ref, wglu_ref, bglu_ref, swo_ref, mwo_ref, wo_ref, g_ref, b_ref,
                out_ref, bu_scr, st_scr, cbuf_scr, *, tm):
    pid = pl.program_id(0)
    steps = tm // BATCH
    W = MIX_WIDTH

    @pl.when(pid == 0)
    def _():
        st_scr[...] = jnp.zeros_like(st_scr)
        cbuf_scr[0:_HALO, :] = jnp.zeros((_HALO, W), F32)

    x = x_ref[...]
    xb = x.astype(BF16)
    row = pid * tm + lax.broadcasted_iota(jnp.int32, (tm, 1), 0)
    valid = row >= PAD * BATCH

    pc = _dot(xb, win_ref[:, 0:3 * W])
    u = jnp.where(valid, pc[:, 2 * W:3 * W] * pc[:, 0:W], 0.0)
    cbuf_scr[_HALO:_HALO + tm, :] = u
    y = (cb_ref[...] + cw_ref[0:1, :] * cbuf_scr[0:tm, :]
         + cw_ref[1:2, :] * cbuf_scr[BATCH:BATCH + tm, :] + cw_ref[2:3, :] * u)
    cbuf_scr[0:_HALO, :] = u[tm - _HALO:, :]
    y_b = _dot((pc[:, W:2 * W] * y).astype(BF16), cwo_ref[...])

    us = jnp.where(valid, _dot(xb, win_ref[:, 3 * W:4 * W]), 0.0)
    usb = us.astype(BF16)
    nb = S5_BLOCK_STATES
    for blk in range(S5_BLOCKS):
        bu_scr[:, 2 * nb * blk:2 * nb * (blk + 1)] = _dot(
            usb[:, blk * 128:(blk + 1) * 128], bblk_ref[blk])
    for blk in range(S5_BLOCKS):
        c_re = 2 * nb * blk
        c_im = c_re + nb
        a_re = ar_ref[blk]
        a_im = ai_ref[blk]

        def step(t, carry, c_re=c_re, c_im=c_im, a_re=a_re, a_im=a_im):
            s_re, s_im = carry
            r0 = pl.multiple_of(t * BATCH, BATCH)
            n_re = a_re * s_re - a_im * s_im + bu_scr[pl.ds(r0, BATCH), c_re:c_re + nb]
            n_im = a_re * s_im + a_im * s_re + bu_scr[pl.ds(r0, BATCH), c_im:c_im + nb]
            bu_scr[pl.ds(r0, BATCH), c_re:c_re + nb] = n_re
            bu_scr[pl.ds(r0, BATCH), c_im:c_im + nb] = n_im
            return n_re, n_im

        s_re, s_im = lax.fori_loop(
            0, steps, step, (st_scr[:, c_re:c_re + nb], st_scr[:, c_im:c_im + nb]), unroll=4)
        st_scr[:, c_re:c_re + nb] = s_re
        st_scr[:, c_im:c_im + nb] = s_im
    y = jnp.concatenate(
        [_dot(bu_scr[:, 2 * nb * blk:2 * nb * (blk + 1)].astype(BF16), cblk_ref[blk])
         for blk in range(S5_BLOCKS)], axis=1)
    y = _gelu_tanh(y + d_ref[...] * us)
    y = y * _sigmoid(_dot(y.astype(BF16), wglu_ref[...]) + bglu_ref[...])
    y_c = _dot(y.astype(BF16), swo_ref[...])

    y_a = _dot(att_ref[...], mwo_ref[...])

    g0 = 4 * W
    mixed = _sigmoid(_dot(xb, win_ref[:, g0:g0 + D_MODEL])) * y_a
    mixed += _sigmoid(_dot(xb, win_ref[:, g0 + D_MODEL:g0 + 2 * D_MODEL])) * y_b
    mixed += _sigmoid(_dot(xb, win_ref[:, g0 + 2 * D_MODEL:g0 + 3 * D_MODEL])) * y_c
    z = ALPHA * x + _dot(mixed.astype(BF16), wo_ref[...])
    out_ref[...] = _layer_norm(z, g_ref[...], b_ref[...])


def _mix(xn, att, win, cw, cb, cwo, bblk, ar, ai, cblk, d, wglu, bglu, swo, mwo, wo, g, b, *, tm=512):
    rows = xn.shape[0]
    consts = (win, cw, cb, cwo, bblk, ar, ai, cblk, d, wglu, bglu, swo, mwo, wo, g, b)
    return pl.pallas_call(
        functools.partial(_mix_kernel, tm=tm),
        grid=(rows // tm,),
        in_specs=[pl.BlockSpec((tm, D_MODEL), lambda i: (i, 0)),
                  pl.BlockSpec((tm, MIX_WIDTH), lambda i: (i, 0))]
        + [_const_spec(c.shape) for c in consts],
        out_specs=pl.BlockSpec((tm, D_MODEL), lambda i: (i, 0)),
        out_shape=jax.ShapeDtypeStruct((rows, D_MODEL), F32),
        scratch_shapes=[
            pltpu.VMEM((tm, S5_COLS), F32),
            pltpu.VMEM((BATCH, S5_COLS), F32),
            pltpu.VMEM((tm + _HALO, MIX_WIDTH), F32),
        ],
        compiler_params=pltpu.CompilerParams(
            dimension_semantics=("arbitrary",), vmem_limit_bytes=VMEM_LIMIT),
        name="mixers_merge",
    )(xn, att, *consts)


def _rope_tables():
    pos = (jnp.arange(LP) - PAD).astype(F32)
    inv_freq = ROPE_BASE ** (-jnp.arange(0, QK_ROPE_DIM, 2, dtype=F32) / QK_ROPE_DIM)
    ang = pos[:, None] * inv_freq[None, :]
    cos2 = jnp.tile(jnp.cos(ang), (1, 2))
    sin2 = jnp.tile(jnp.sin(ang), (1, 2))
    zn = jnp.zeros((LP, QK_NOPE_DIM), F32)
    zt = jnp.zeros((LP, HEAD_LANES - QK_HEAD_DIM), F32)
    scale = QK_HEAD_DIM ** -0.5
    cos_q = scale * jnp.concatenate([jnp.ones_like(zn), cos2, zt], axis=1)
    sin_q = scale * jnp.concatenate([zn, sin2, zt], axis=1)
    cos_k = jnp.concatenate([zn, cos2, zt], axis=1)
    sin_k = jnp.concatenate([zn, sin2, zt], axis=1)
    tab = jnp.concatenate([cos_q, sin_q, cos_k, sin_k], axis=1)
    return jnp.repeat(tab, BATCH, axis=0)


def _rot_half_cols(w):
    half = QK_ROPE_DIM // 2
    return jnp.concatenate([-w[..., half:], w[..., :half]], axis=-1)


def _qkv_weights(w_in, w_uq, w_ukv):
    zpad = lambda n: jnp.zeros((D_MODEL, n), F32)
    kr = w_in[:, _OFF_KR:_OFF_CONV]
    tail = HEAD_LANES - QK_HEAD_DIM
    wc = jnp.concatenate([
        w_in[:, :_OFF_KR],
        zpad(QK_NOPE_DIM), kr, zpad(tail),
        zpad(QK_NOPE_DIM), _rot_half_cols(kr), zpad(tail)], axis=1)
    uq = w_uq.reshape(Q_LORA_RANK, HEADS, QK_HEAD_DIM)
    zq = lambda n: jnp.zeros((Q_LORA_RANK, HEADS, n), F32)
    wqa = jnp.concatenate([uq, zq(tail)], axis=-1)
    wqb = jnp.concatenate([zq(QK_NOPE_DIM), _rot_half_cols(uq[..., QK_NOPE_DIM:]), zq(tail)], axis=-1)
    ukv = w_ukv.reshape(KV_LORA_RANK, HEADS, QK_NOPE_DIM + V_HEAD_DIM)
    zk = jnp.zeros((KV_LORA_RANK, HEADS, HEAD_LANES - QK_NOPE_DIM), F32)
    wuk = jnp.concatenate([ukv[..., :QK_NOPE_DIM], zk], axis=-1)
    vv = ukv[..., QK_NOPE_DIM:]
    zv = jnp.zeros_like(vv)
    odd = (jnp.arange(HEADS) % 2 == 1)[None, :, None]
    wuv = jnp.concatenate([jnp.where(odd, zv, vv), jnp.where(odd, vv, zv)], axis=-1)
    flat = lambda w: w.reshape(w.shape[0], QKV_WIDTH).astype(BF16)
    return wc.astype(BF16), flat(wqa), flat(wqb), flat(wuk), flat(wuv)


def _s5_weights(a_re, a_im, log_dt, b_re, b_im, c_re, c_im):
    dt = jnp.exp(log_dt)[:, None]
    mag = jnp.exp(dt * a_re)
    ab_re, ab_im = mag * jnp.cos(dt * a_im), mag * jnp.sin(dt * a_im)
    den = a_re * a_re + a_im * a_im
    nr, ni = ab_re - 1.0, ab_im
    coef_re = (nr * a_re + ni * a_im) / den
    coef_im = (ni * a_re - nr * a_im) / den
    bb_re = coef_re[..., None] * b_re - coef_im[..., None] * b_im
    bb_im = coef_re[..., None] * b_im + coef_im[..., None] * b_re
    gpb = S5_GROUPS // S5_BLOCKS
    eye = jnp.eye(gpb, dtype=F32)

    def in_blocks(bb):
        t = bb.transpose(0, 2, 1).reshape(S5_BLOCKS, gpb, S5_GROUP, S5_STATE)
        return jnp.einsum('bghn,gk->bghkn', t, eye).reshape(S5_BLOCKS, gpb * S5_GROUP, gpb * S5_STATE)

    def out_blocks(cc):
        t = cc.reshape(S5_BLOCKS, gpb, S5_GROUP, S5_STATE)
        return jnp.einsum('bghn,gk->bgnkh', t, eye).reshape(S5_BLOCKS, gpb * S5_STATE, gpb * S5_GROUP)

    bblk = jnp.concatenate([in_blocks(bb_re), in_blocks(bb_im)], axis=2).astype(BF16)
    cblk = jnp.concatenate([out_blocks(c_re), out_blocks(-c_im)], axis=1).astype(BF16)
    bcast = lambda a: jnp.broadcast_to(
        a.reshape(S5_BLOCKS, 1, S5_BLOCK_STATES), (S5_BLOCKS, BATCH, S5_BLOCK_STATES))
    return bblk, bcast(ab_re), bcast(ab_im), cblk


def kernel(x, meta, ffn1_w_gate, ffn1_w_up, ffn1_w_down, ln1_g, ln1_b, w_in, mla_q_norm_g, mla_w_uq, mla_kv_norm_g, mla_w_ukv, mla_w_o, conv_w, conv_b, conv_w_out, s5_a_re, s5_a_im, s5_log_dt, s5_b_re, s5_b_im, s5_c_re, s5_c_im, s5_d, s5_w_glu, s5_b_glu, s5_w_out, w_o, ln2_g, ln2_b, ffn2_w_gate, ffn2_w_up, ffn2_w_down, ln3_g, ln3_b):
    row = lambda v: v.reshape(1, -1).astype(F32)
    bf = lambda w: w.astype(BF16)
    h = jnp.concatenate([
        jnp.zeros((PAD, BATCH, D_MODEL), F32),
        jnp.broadcast_to(meta[:, None, :].astype(F32), (N_META, BATCH, D_MODEL)),
        jnp.transpose(x, (1, 0, 2))], axis=0).reshape(ROWS, D_MODEL)
    tab = _rope_tables()
    to_batch_major = lambda a: jnp.transpose(a.reshape(LP, BATCH, -1), (1, 0, 2))
    for i in range(DEPTH):
        xn = _ffn_ln(h, bf(ffn1_w_gate[i]), bf(ffn1_w_up[i]), bf(ffn1_w_down[i]), row(ln1_g[i]), row(ln1_b[i]))
        wc, wqa, wqb, wuk, wuv = _qkv_weights(w_in[i], mla_w_uq[i], mla_w_ukv[i])
        q, k, v = _qkv(xn, wc, row(mla_q_norm_g[i]), row(mla_kv_norm_g[i]), wqa, wqb, wuk, wuv, tab)
        att = _attention(to_batch_major(q), to_batch_major(k), to_batch_major(v))
        att = jnp.transpose(att, (1, 0, 2)).reshape(ROWS, MIX_WIDTH)
        bblk, ar, ai, cblk = _s5_weights(s5_a_re[i], s5_a_im[i], s5_log_dt[i], s5_b_re[i], s5_b_im[i],
                                         s5_c_re[i], s5_c_im[i])
        h = _mix(xn, att, bf(w_in[i][:, _OFF_CONV:]), conv_w[i].astype(F32), row(conv_b[i]),
                 bf(conv_w_out[i]), bblk, ar, ai, cblk, row(s5_d[i]), bf(s5_w_glu[i]), row(s5_b_glu[i]),
                 bf(s5_w_out[i]), bf(mla_w_o[i]), bf(w_o[i]), row(ln2_g[i]), row(ln2_b[i]))
        h = _ffn_ln(h, bf(ffn2_w_gate[i]), bf(ffn2_w_up[i]), bf(ffn2_w_down[i]), row(ln3_g[i]), row(ln3_b[i]))
    out = h.reshape(LP, BATCH, D_MODEL)[PAD + N_META:]
    return jnp.transpose(out, (1, 0, 2))
```

```python
import functools
import math

import jax
import jax.numpy as jnp
from jax import lax
from jax.experimental import pallas as pl
from jax.experimental.pallas import tpu as pltpu

D_MODEL = 1024
BATCH = 8
SEQ = 2048
DEPTH = 2
N_META = 16
BLOCK_Q = 128
MIX_WIDTH = D_MODEL // 2
HEADS = 8
V_HEAD_DIM = 64
QK_NOPE_DIM = 64
QK_ROPE_DIM = 32
QK_HEAD_DIM = QK_NOPE_DIM + QK_ROPE_DIM
Q_LORA_RANK = 384
KV_LORA_RANK = 256
ROPE_BASE = 10000.0
S5_GROUP = 16
S5_GROUPS = 32
S5_STATE = 64
D_FF = 2816
ALPHA = (2.0 * DEPTH) ** 0.25
LN_EPS = 1e-5
RMS_EPS = 1e-6

HEAD_LANES = 128
QKV_WIDTH = HEADS * HEAD_LANES
PAD = (-(N_META + SEQ)) % BLOCK_Q
LP = PAD + N_META + SEQ
ROWS = LP * BATCH
S5_BLOCKS = 4
S5_BLOCK_STATES = (S5_GROUPS // S5_BLOCKS) * S5_STATE
S5_COLS = 2 * S5_GROUPS * S5_STATE

_OFF_CQ, _OFF_CKV, _OFF_KR, _OFF_CONV = 0, 384, 640, 672
_OFF_GATES = _OFF_CONV + 4 * MIX_WIDTH
D_IN = _OFF_GATES + 3 * D_MODEL

V7X_VMEM_BYTES = 64 * 1024 * 1024
VMEM_LIMIT = 56 * 1024 * 1024

F32 = jnp.float32
BF16 = jnp.bfloat16


def _dot(a, b):
    return jnp.dot(a, b, preferred_element_type=F32)


def _const_spec(shape):
    nd = len(shape)
    return pl.BlockSpec(shape, lambda *_: (0,) * nd, pipeline_mode=pl.Buffered(1))


def _layer_norm(y, g, b):
    mu = jnp.mean(y, axis=-1, keepdims=True)
    yc = y - mu
    var = jnp.mean(yc * yc, axis=-1, keepdims=True)
    return yc * lax.rsqrt(var + LN_EPS) * g + b


def _rms_norm(y, g):
    return y * lax.rsqrt(jnp.mean(y * y, axis=-1, keepdims=True) + RMS_EPS) * g


def _sigmoid(y):
    return 1.0 / (1.0 + jnp.exp(-y))


def _gelu_tanh(y):
    return 0.5 * y * (1.0 + jnp.tanh(math.sqrt(2.0 / math.pi) * (y + 0.044715 * (y * y * y))))


def _ffn_ln_kernel(x_ref, wg_ref, wu_ref, wd_ref, g_ref, b_ref, o_ref, *, ff_chunk):
    x = x_ref[...]
    xb = x.astype(BF16)
    acc = None
    for c in range(D_FF // ff_chunk):
        sl = slice(c * ff_chunk, (c + 1) * ff_chunk)
        gate = _dot(xb, wg_ref[:, sl])
        up = _dot(xb, wu_ref[:, sl])
        hmid = (gate * _sigmoid(gate) * up).astype(BF16)
        part = _dot(hmid, wd_ref[sl, :])
        acc = part if acc is None else acc + part
    o_ref[...] = _layer_norm(ALPHA * x + 0.5 * acc, g_ref[...], b_ref[...])


def _ffn_ln(x, wg, wu, wd, g, b, *, tm=512, ff_chunk=1408):
    rows = x.shape[0]
    return pl.pallas_call(
        functools.partial(_ffn_ln_kernel, ff_chunk=ff_chunk),
        grid=(rows // tm,),
        in_specs=[
            pl.BlockSpec((tm, D_MODEL), lambda i: (i, 0)),
            _const_spec((D_MODEL, D_FF)),
            _const_spec((D_MODEL, D_FF)),
            _const_spec((D_FF, D_MODEL)),
            _const_spec((1, D_MODEL)),
            _const_spec((1, D_MODEL)),
        ],
        out_specs=pl.BlockSpec((tm, D_MODEL), lambda i: (i, 0)),
        out_shape=jax.ShapeDtypeStruct((rows, D_MODEL), F32),
        compiler_params=pltpu.CompilerParams(
            dimension_semantics=("arbitrary",), vmem_limit_bytes=VMEM_LIMIT),
        name="ffn_ln",
    )(x, wg, wu, wd, g, b)


_WC_WIDTH = Q_LORA_RANK + KV_LORA_RANK + 2 * HEAD_LANES


def _qkv_kernel(x_ref, wc_ref, gq_ref, gkv_ref, wqa_ref, wqb_ref, wuk_ref, wuv_ref, vones_ref, tab_ref,
                q_ref, k_ref, v_ref):
    xb = x_ref[...].astype(BF16)
    c = _dot(xb, wc_ref[...])
    c_q = c[:, :Q_LORA_RANK]
    c_kv = c[:, Q_LORA_RANK:Q_LORA_RANK + KV_LORA_RANK]
    k_r = c[:, _WC_WIDTH - 2 * HEAD_LANES:_WC_WIDTH - HEAD_LANES]
    k_r_rot = c[:, _WC_WIDTH - HEAD_LANES:]
    qn = _rms_norm(c_q, gq_ref[...]).astype(BF16)
    kvn = _rms_norm(c_kv, gkv_ref[...]).astype(BF16)
    cos_q = tab_ref[:, 0 * HEAD_LANES:1 * HEAD_LANES]
    sin_q = tab_ref[:, 1 * HEAD_LANES:2 * HEAD_LANES]
    cos_k = tab_ref[:, 2 * HEAD_LANES:3 * HEAD_LANES]
    sin_k = tab_ref[:, 3 * HEAD_LANES:4 * HEAD_LANES]
    q_a = _dot(qn, wqa_ref[...])
    q_b = _dot(qn, wqb_ref[...])
    k_nope = _dot(kvn, wuk_ref[...])
    k_rope = k_r * cos_k + k_r_rot * sin_k
    for h in range(HEADS):
        sl = slice(h * HEAD_LANES, (h + 1) * HEAD_LANES)
        q_ref[:, sl] = (q_a[:, sl] * cos_q + q_b[:, sl] * sin_q).astype(BF16)
        k_ref[:, sl] = (k_nope[:, sl] + k_rope).astype(BF16)
    v_ref[...] = (_dot(kvn, wuv_ref[...]) + vones_ref[...]).astype(BF16)


def _qkv(xn, wc, gq, gkv, wqa, wqb, wuk, wuv, vones, tab, *, tm=512):
    rows = xn.shape[0]
    row_spec = lambda w: pl.BlockSpec((tm, w), lambda i: (i, 0))
    out = lambda w: jax.ShapeDtypeStruct((rows, w), BF16)
    return pl.pallas_call(
        _qkv_kernel,
        grid=(rows // tm,),
        in_specs=[
            row_spec(D_MODEL),
            _const_spec((D_MODEL, _WC_WIDTH)),
            _const_spec((1, Q_LORA_RANK)),
            _const_spec((1, KV_LORA_RANK)),
            _const_spec((Q_LORA_RANK, QKV_WIDTH)),
            _const_spec((Q_LORA_RANK, QKV_WIDTH)),
            _const_spec((KV_LORA_RANK, QKV_WIDTH)),
            _const_spec((KV_LORA_RANK, HEADS * V_ROWS)),
            _const_spec((1, HEADS * V_ROWS)),
            row_spec(4 * HEAD_LANES),
        ],
        out_specs=[row_spec(QKV_WIDTH), row_spec(QKV_WIDTH), row_spec(HEADS * V_ROWS)],
        out_shape=[out(QKV_WIDTH), out(QKV_WIDTH), out(HEADS * V_ROWS)],
        compiler_params=pltpu.CompilerParams(
            dimension_semantics=("arbitrary",), vmem_limit_bytes=VMEM_LIMIT),
        name="qkv_proj",
    )(xn, wc, gq, gkv, wqa, wqb, wuk, wuv, vones, tab)


KV_CHUNK = 2 * BLOCK_Q
N_KV_CHUNKS = (LP // BLOCK_Q + 1) // 2
V_ROWS = V_HEAD_DIM + 16
LOG2E = math.log2(math.e)


def _attend(q_ref, k_ref, v_ref, m_scr, acc_scr, c, mask, first):
    scores = []
    for h in range(HEADS):
        q_t = q_ref[h * HEAD_LANES:(h + 1) * HEAD_LANES, :]
        k = k_ref[c, :, h * HEAD_LANES:(h + 1) * HEAD_LANES]
        scores.append(_dot(k, q_t))
    for h in range(HEADS):
        s_t = scores[h]
        if mask is not None:
            s_t = jnp.where(mask, s_t, -1e30)
        m_cur = jnp.max(s_t, axis=0, keepdims=True)
        if first:
            m_new = m_cur
        else:
            m_old = m_scr[h]
            m_new = jnp.maximum(m_old, m_cur)
        p_t = jnp.exp2(s_t - m_new).astype(BF16)
        pv = _dot(v_ref[c, h * V_ROWS:(h + 1) * V_ROWS, :], p_t)
        if first:
            acc_scr[h] = pv
        else:
            acc_scr[h] = jnp.exp2(m_old - m_new) * acc_scr[h] + pv
        m_scr[h] = m_new


def _attn_kernel(q_ref, k_ref, v_ref, o_ref, m_scr, acc_scr):
    i = pl.program_id(1)
    n_chunks = (i + 2) // 2
    q_pos = i * BLOCK_Q + lax.broadcasted_iota(jnp.int32, (KV_CHUNK, BLOCK_Q), 1)
    k_off = lax.broadcasted_iota(jnp.int32, (KV_CHUNK, BLOCK_Q), 0)

    def mask(c):
        k_pos = c * KV_CHUNK + k_off
        return (k_pos <= q_pos) & (k_pos >= PAD)

    refs = (q_ref, k_ref, v_ref, m_scr, acc_scr)
    _attend(*refs, 0, mask(0), True)

    def middle(c, carry):
        _attend(*refs, c, None, False)
        return carry

    lax.fori_loop(1, n_chunks - 1, middle, 0)

    @pl.when(n_chunks > 1)
    def _():
        _attend(*refs, n_chunks - 1, mask(n_chunks - 1), False)

    for h in range(HEADS):
        acc = acc_scr[h]
        o = acc[0:V_HEAD_DIM, :] / acc[V_HEAD_DIM:V_HEAD_DIM + 1, :]
        o_ref[h * V_HEAD_DIM:(h + 1) * V_HEAD_DIM, :] = o.astype(BF16)


def _attention(q_t, k, v_t):
    return pl.pallas_call(
        _attn_kernel,
        grid=(BATCH, LP // BLOCK_Q),
        in_specs=[
            pl.BlockSpec((None, QKV_WIDTH, BLOCK_Q), lambda b, i: (b, 0, i)),
            pl.BlockSpec((None, N_KV_CHUNKS, KV_CHUNK, QKV_WIDTH), lambda b, i: (b, 0, 0, 0)),
            pl.BlockSpec((None, N_KV_CHUNKS, HEADS * V_ROWS, KV_CHUNK), lambda b, i: (b, 0, 0, 0)),
        ],
        out_specs=pl.BlockSpec((None, MIX_WIDTH, BLOCK_Q), lambda b, i: (b, 0, i)),
        out_shape=jax.ShapeDtypeStruct((BATCH, MIX_WIDTH, LP), BF16),
        scratch_shapes=[pltpu.VMEM((HEADS, 1, BLOCK_Q), F32),
                        pltpu.VMEM((HEADS, V_ROWS, BLOCK_Q), F32)],
        compiler_params=pltpu.CompilerParams(
            dimension_semantics=("arbitrary", "arbitrary"), vmem_limit_bytes=VMEM_LIMIT),
        name="mla_attention",
    )(q_t, k, v_t)


_MIX_IN = 4 * MIX_WIDTH + 3 * D_MODEL
_HALO = 2 * BATCH


def _mix_kernel(x_ref, att_ref, win_ref, cw_ref, cb_ref, cwo_ref, bblk_ref, ar_ref, ai_ref,
                cblk_ref, d_ref, wglu_ref, bglu_ref, swo_ref, mwo_ref, wo_ref, g_ref, b_ref,
                out_ref, bu_scr, st_scr, cbuf_scr, *, tm):
    pid = pl.program_id(0)
    steps = tm // BATCH
    W = MIX_WIDTH

    @pl.when(pid == 0)
    def _():
        st_scr[...] = jnp.zeros_like(st_scr)
        cbuf_scr[0:_HALO, :] = jnp.zeros((_HALO, W), F32)

    x = x_ref[...]
    xb = x.astype(BF16)
    row = pid * tm + lax.broadcasted_iota(jnp.int32, (tm, 1), 0)
    valid = row >= PAD * BATCH

    pc = _dot(xb, win_ref[:, 0:3 * W])
    u = jnp.where(valid, pc[:, 2 * W:3 * W] * pc[:, 0:W], 0.0)
    cbuf_scr[_HALO:_HALO + tm, :] = u
    y = (cb_ref[...] + cw_ref[0:1, :] * cbuf_scr[0:tm, :]
         + cw_ref[1:2, :] * cbuf_scr[BATCH:BATCH + tm, :] + cw_ref[2:3, :] * u)
    cbuf_scr[0:_HALO, :] = u[tm - _HALO:, :]
    y_b = _dot((pc[:, W:2 * W] * y).astype(BF16), cwo_ref[...])

    us = jnp.where(valid, _dot(xb, win_ref[:, 3 * W:4 * W]), 0.0)
    usb = us.astype(BF16)
    nb = S5_BLOCK_STATES
    for blk in range(S5_BLOCKS):
        bu_scr[:, 2 * nb * blk:2 * nb * (blk + 1)] = _dot(
            usb[:, blk * 128:(blk + 1) * 128], bblk_ref[blk])
    for blk in range(S5_BLOCKS):
        c_re = 2 * nb * blk
        c_im = c_re + nb
        a_re = ar_ref[blk]
        a_im = ai_ref[blk]

        def step(t, carry, c_re=c_re, c_im=c_im, a_re=a_re, a_im=a_im):
            s_re, s_im = carry
            r0 = pl.multiple_of(t * BATCH, BATCH)
            n_re = a_re * s_re - a_im * s_im + bu_scr[pl.ds(r0, BATCH), c_re:c_re + nb]
            n_im = a_re * s_im + a_im * s_re + bu_scr[pl.ds(r0, BATCH), c_im:c_im + nb]
            bu_scr[pl.ds(r0, BATCH), c_re:c_re + nb] = n_re
            bu_scr[pl.ds(r0, BATCH), c_im:c_im + nb] = n_im
            return n_re, n_im

        s_re, s_im = lax.fori_loop(
            0, steps, step, (st_scr[:, c_re:c_re + nb], st_scr[:, c_im:c_im + nb]), unroll=4)
        st_scr[:, c_re:c_re + nb] = s_re
        st_scr[:, c_im:c_im + nb] = s_im
    y = jnp.concatenate(
        [_dot(bu_scr[:, 2 * nb * blk:2 * nb * (blk + 1)].astype(BF16), cblk_ref[blk])
         for blk in range(S5_BLOCKS)], axis=1)
    y = _gelu_tanh(y + d_ref[...] * us)
    y = y * _sigmoid(_dot(y.astype(BF16), wglu_ref[...]) + bglu_ref[...])
    y_c = _dot(y.astype(BF16), swo_ref[...])

    y_a = _dot(att_ref[...], mwo_ref[...])

    g0 = 4 * W
    mixed = _sigmoid(_dot(xb, win_ref[:, g0:g0 + D_MODEL])) * y_a
    mixed += _sigmoid(_dot(xb, win_ref[:, g0 + D_MODEL:g0 + 2 * D_MODEL])) * y_b
    mixed += _sigmoid(_dot(xb, win_ref[:, g0 + 2 * D_MODEL:g0 + 3 * D_MODEL])) * y_c
    z = ALPHA * x + _dot(mixed.astype(BF16), wo_ref[...])
    out_ref[...] = _layer_norm(z, g_ref[...], b_ref[...])


def _mix(xn, att, win, cw, cb, cwo, bblk, ar, ai, cblk, d, wglu, bglu, swo, mwo, wo, g, b, *, tm=512):
    rows = xn.shape[0]
    consts = (win, cw, cb, cwo, bblk, ar, ai, cblk, d, wglu, bglu, swo, mwo, wo, g, b)
    return pl.pallas_call(
        functools.partial(_mix_kernel, tm=tm),
        grid=(rows // tm,),
        in_specs=[pl.BlockSpec((tm, D_MODEL), lambda i: (i, 0)),
                  pl.BlockSpec((tm, MIX_WIDTH), lambda i: (i, 0))]
        + [_const_spec(c.shape) for c in consts],
        out_specs=pl.BlockSpec((tm, D_MODEL), lambda i: (i, 0)),
        out_shape=jax.ShapeDtypeStruct((rows, D_MODEL), F32),
        scratch_shapes=[
            pltpu.VMEM((tm, S5_COLS), F32),
            pltpu.VMEM((BATCH, S5_COLS), F32),
            pltpu.VMEM((tm + _HALO, MIX_WIDTH), F32),
        ],
        compiler_params=pltpu.CompilerParams(
            dimension_semantics=("arbitrary",), vmem_limit_bytes=VMEM_LIMIT),
        name="mixers_merge",
    )(xn, att, *consts)


def _rope_tables():
    pos = (jnp.arange(LP) - PAD).astype(F32)
    inv_freq = ROPE_BASE ** (-jnp.arange(0, QK_ROPE_DIM, 2, dtype=F32) / QK_ROPE_DIM)
    ang = pos[:, None] * inv_freq[None, :]
    cos2 = jnp.tile(jnp.cos(ang), (1, 2))
    sin2 = jnp.tile(jnp.sin(ang), (1, 2))
    zn = jnp.zeros((LP, QK_NOPE_DIM), F32)
    zt = jnp.zeros((LP, HEAD_LANES - QK_HEAD_DIM), F32)
    scale = QK_HEAD_DIM ** -0.5 * LOG2E
    cos_q = scale * jnp.concatenate([jnp.ones_like(zn), cos2, zt], axis=1)
    sin_q = scale * jnp.concatenate([zn, sin2, zt], axis=1)
    cos_k = jnp.concatenate([zn, cos2, zt], axis=1)
    sin_k = jnp.concatenate([zn, sin2, zt], axis=1)
    tab = jnp.concatenate([cos_q, sin_q, cos_k, sin_k], axis=1)
    return jnp.repeat(tab, BATCH, axis=0)


def _rot_half_cols(w):
    half = QK_ROPE_DIM // 2
    return jnp.concatenate([-w[..., half:], w[..., :half]], axis=-1)


def _qkv_weights(w_in, w_uq, w_ukv):
    zpad = lambda n: jnp.zeros((D_MODEL, n), F32)
    kr = w_in[:, _OFF_KR:_OFF_CONV]
    tail = HEAD_LANES - QK_HEAD_DIM
    wc = jnp.concatenate([
        w_in[:, :_OFF_KR],
        zpad(QK_NOPE_DIM), kr, zpad(tail),
        zpad(QK_NOPE_DIM), _rot_half_cols(kr), zpad(tail)], axis=1)
    uq = w_uq.reshape(Q_LORA_RANK, HEADS, QK_HEAD_DIM)
    zq = lambda n: jnp.zeros((Q_LORA_RANK, HEADS, n), F32)
    wqa = jnp.concatenate([uq, zq(tail)], axis=-1)
    wqb = jnp.concatenate([zq(QK_NOPE_DIM), _rot_half_cols(uq[..., QK_NOPE_DIM:]), zq(tail)], axis=-1)
    ukv = w_ukv.reshape(KV_LORA_RANK, HEADS, QK_NOPE_DIM + V_HEAD_DIM)
    zk = jnp.zeros((KV_LORA_RANK, HEADS, HEAD_LANES - QK_NOPE_DIM), F32)
    wuk = jnp.concatenate([ukv[..., :QK_NOPE_DIM], zk], axis=-1)
    wuv = jnp.concatenate(
        [ukv[..., QK_NOPE_DIM:], jnp.zeros((KV_LORA_RANK, HEADS, V_ROWS - V_HEAD_DIM), F32)], axis=-1)
    vones = jnp.zeros((HEADS, V_ROWS), F32).at[:, V_HEAD_DIM].set(1.0).reshape(1, HEADS * V_ROWS)
    flat = lambda w: w.reshape(w.shape[0], -1).astype(BF16)
    return wc.astype(BF16), flat(wqa), flat(wqb), flat(wuk), flat(wuv), vones


def _s5_weights(a_re, a_im, log_dt, b_re, b_im, c_re, c_im):
    dt = jnp.exp(log_dt)[:, None]
    mag = jnp.exp(dt * a_re)
    ab_re, ab_im = mag * jnp.cos(dt * a_im), mag * jnp.sin(dt * a_im)
    den = a_re * a_re + a_im * a_im
    nr, ni = ab_re - 1.0, ab_im
    coef_re = (nr * a_re + ni * a_im) / den
    coef_im = (ni * a_re - nr * a_im) / den
    bb_re = coef_re[..., None] * b_re - coef_im[..., None] * b_im
    bb_im = coef_re[..., None] * b_im + coef_im[..., None] * b_re
    gpb = S5_GROUPS // S5_BLOCKS
    eye = jnp.eye(gpb, dtype=F32)

    def in_blocks(bb):
        t = bb.transpose(0, 2, 1).reshape(S5_BLOCKS, gpb, S5_GROUP, S5_STATE)
        return jnp.einsum('bghn,gk->bghkn', t, eye).reshape(S5_BLOCKS, gpb * S5_GROUP, gpb * S5_STATE)

    def out_blocks(cc):
        t = cc.reshape(S5_BLOCKS, gpb, S5_GROUP, S5_STATE)
        return jnp.einsum('bghn,gk->bgnkh', t, eye).reshape(S5_BLOCKS, gpb * S5_STATE, gpb * S5_GROUP)

    bblk = jnp.concatenate([in_blocks(bb_re), in_blocks(bb_im)], axis=2).astype(BF16)
    cblk = jnp.concatenate([out_blocks(c_re), out_blocks(-c_im)], axis=1).astype(BF16)
    bcast = lambda a: jnp.broadcast_to(
        a.reshape(S5_BLOCKS, 1, S5_BLOCK_STATES), (S5_BLOCKS, BATCH, S5_BLOCK_STATES))
    return bblk, bcast(ab_re), bcast(ab_im), cblk


def kernel(x, meta, ffn1_w_gate, ffn1_w_up, ffn1_w_down, ln1_g, ln1_b, w_in, mla_q_norm_g, mla_w_uq, mla_kv_norm_g, mla_w_ukv, mla_w_o, conv_w, conv_b, conv_w_out, s5_a_re, s5_a_im, s5_log_dt, s5_b_re, s5_b_im, s5_c_re, s5_c_im, s5_d, s5_w_glu, s5_b_glu, s5_w_out, w_o, ln2_g, ln2_b, ffn2_w_gate, ffn2_w_up, ffn2_w_down, ln3_g, ln3_b):
    row = lambda v: v.reshape(1, -1).astype(F32)
    bf = lambda w: w.astype(BF16)
    h = jnp.concatenate([
        jnp.zeros((PAD, BATCH, D_MODEL), F32),
        jnp.broadcast_to(meta[:, None, :].astype(F32), (N_META, BATCH, D_MODEL)),
        jnp.transpose(x, (1, 0, 2))], axis=0).reshape(ROWS, D_MODEL)
    tab = _rope_tables()
    kv_pad = ((0, N_KV_CHUNKS * KV_CHUNK - LP), (0, 0), (0, 0))
    for i in range(DEPTH):
        xn = _ffn_ln(h, bf(ffn1_w_gate[i]), bf(ffn1_w_up[i]), bf(ffn1_w_down[i]), row(ln1_g[i]), row(ln1_b[i]))
        wc, wqa, wqb, wuk, wuv, vones = _qkv_weights(w_in[i], mla_w_uq[i], mla_w_ukv[i])
        q, k, v = _qkv(xn, wc, row(mla_q_norm_g[i]), row(mla_kv_norm_g[i]), wqa, wqb, wuk, wuv, vones, tab)
        q_t = jnp.transpose(q.reshape(LP, BATCH, QKV_WIDTH), (1, 2, 0))
        k_c = jnp.transpose(jnp.pad(k.reshape(LP, BATCH, QKV_WIDTH), kv_pad), (1, 0, 2)).reshape(
            BATCH, N_KV_CHUNKS, KV_CHUNK, QKV_WIDTH)
        v_t = jnp.transpose(jnp.pad(v.reshape(LP, BATCH, HEADS * V_ROWS), kv_pad).reshape(
            N_KV_CHUNKS, KV_CHUNK, BATCH, HEADS * V_ROWS), (2, 0, 3, 1))
        att = jnp.transpose(_attention(q_t, k_c, v_t), (2, 0, 1)).reshape(ROWS, MIX_WIDTH)
        bblk, ar, ai, cblk = _s5_weights(s5_a_re[i], s5_a_im[i], s5_log_dt[i], s5_b_re[i], s5_b_im[i],
                                         s5_c_re[i], s5_c_im[i])
        h = _mix(xn, att, bf(w_in[i][:, _OFF_CONV:]), conv_w[i].astype(F32), row(conv_b[i]),
                 bf(conv_w_out[i]), bblk, ar, ai, cblk, row(s5_d[i]), bf(s5_w_glu[i]), row(s5_b_glu[i]),
                 bf(s5_w_out[i]), bf(mla_w_o[i]), bf(w_o[i]), row(ln2_g[i]), row(ln2_b[i]))
        h = _ffn_ln(h, bf(ffn2_w_gate[i]), bf(ffn2_w_up[i]), bf(ffn2_w_down[i]), row(ln3_g[i]), row(ln3_b[i]))
    out = h.reshape(LP, BATCH, D_MODEL)[PAD + N_META:]
    return jnp.transpose(out, (1, 0, 2))
```

```python
import functools
import math

import jax
import jax.numpy as jnp
from jax import lax
from jax.experimental import pallas as pl
from jax.experimental.pallas import tpu as pltpu

D_MODEL = 1024
BATCH = 8
SEQ = 2048
DEPTH = 2
N_META = 16
BLOCK_Q = 128
MIX_WIDTH = D_MODEL // 2
HEADS = 8
V_HEAD_DIM = 64
QK_NOPE_DIM = 64
QK_ROPE_DIM = 32
QK_HEAD_DIM = QK_NOPE_DIM + QK_ROPE_DIM
Q_LORA_RANK = 384
KV_LORA_RANK = 256
ROPE_BASE = 10000.0
S5_GROUP = 16
S5_GROUPS = 32
S5_STATE = 64
D_FF = 2816
ALPHA = (2.0 * DEPTH) ** 0.25
LN_EPS = 1e-5
RMS_EPS = 1e-6

HEAD_LANES = 128
QKV_WIDTH = HEADS * HEAD_LANES
PAD = (-(N_META + SEQ)) % BLOCK_Q
LP = PAD + N_META + SEQ
ROWS = LP * BATCH
S5_BLOCKS = 4
S5_BLOCK_STATES = (S5_GROUPS // S5_BLOCKS) * S5_STATE
S5_COLS = 2 * S5_GROUPS * S5_STATE

_OFF_CQ, _OFF_CKV, _OFF_KR, _OFF_CONV = 0, 384, 640, 672
_OFF_GATES = _OFF_CONV + 4 * MIX_WIDTH
D_IN = _OFF_GATES + 3 * D_MODEL

V7X_VMEM_BYTES = 64 * 1024 * 1024
VMEM_LIMIT = 56 * 1024 * 1024

F32 = jnp.float32
BF16 = jnp.bfloat16


def _dot(a, b):
    return jnp.dot(a, b, preferred_element_type=F32)


def _const_spec(shape):
    nd = len(shape)
    return pl.BlockSpec(shape, lambda *_: (0,) * nd, pipeline_mode=pl.Buffered(1))


def _layer_norm(y, g, b):
    mu = jnp.mean(y, axis=-1, keepdims=True)
    yc = y - mu
    var = jnp.mean(yc * yc, axis=-1, keepdims=True)
    return yc * lax.rsqrt(var + LN_EPS) * g + b


def _rms_norm(y, g):
    return y * lax.rsqrt(jnp.mean(y * y, axis=-1, keepdims=True) + RMS_EPS) * g


def _sigmoid(y):
    return 1.0 / (1.0 + jnp.exp(-y))


def _gelu_tanh(y):
    return 0.5 * y * (1.0 + jnp.tanh(math.sqrt(2.0 / math.pi) * (y + 0.044715 * (y * y * y))))


def _ffn_ln_kernel(x_ref, wg_ref, wu_ref, wd_ref, g_ref, b_ref, o_ref, *, ff_chunk):
    x = x_ref[...]
    xb = x.astype(BF16)
    acc = None
    for c in range(D_FF // ff_chunk):
        sl = slice(c * ff_chunk, (c + 1) * ff_chunk)
        gate = _dot(xb, wg_ref[:, sl])
        up = _dot(xb, wu_ref[:, sl])
        hmid = (gate * _sigmoid(gate) * up).astype(BF16)
        part = _dot(hmid, wd_ref[sl, :])
        acc = part if acc is None else acc + part
    o_ref[...] = _layer_norm(ALPHA * x + 0.5 * acc, g_ref[...], b_ref[...])


def _ffn_ln(x, wg, wu, wd, g, b, *, tm=512, ff_chunk=1408):
    rows = x.shape[0]
    return pl.pallas_call(
        functools.partial(_ffn_ln_kernel, ff_chunk=ff_chunk),
        grid=(rows // tm,),
        in_specs=[
            pl.BlockSpec((tm, D_MODEL), lambda i: (i, 0)),
            _const_spec((D_MODEL, D_FF)),
            _const_spec((D_MODEL, D_FF)),
            _const_spec((D_FF, D_MODEL)),
            _const_spec((1, D_MODEL)),
            _const_spec((1, D_MODEL)),
        ],
        out_specs=pl.BlockSpec((tm, D_MODEL), lambda i: (i, 0)),
        out_shape=jax.ShapeDtypeStruct((rows, D_MODEL), F32),
        compiler_params=pltpu.CompilerParams(
            dimension_semantics=("arbitrary",), vmem_limit_bytes=VMEM_LIMIT),
        name="ffn_ln",
    )(x, wg, wu, wd, g, b)


_WC_WIDTH = Q_LORA_RANK + KV_LORA_RANK + 2 * HEAD_LANES


def _batch_rows(slab_ref, j, b, steps):
    return slab_ref[j, pl.ds(b, steps, stride=BATCH), :]


def _qkv_kernel(x_ref, wc_ref, gq_ref, gkv_ref, wqa_ref, wqb_ref, wuk_ref, wuv_ref, vones_ref, tab_ref,
                q_ref, k_ref, v_ref, slab_ref):
    steps = BLOCK_Q
    v_slabs = HEADS * V_ROWS // HEAD_LANES

    @pl.when(pl.program_id(0) >= LP // BLOCK_Q)
    def _():
        q_ref[...] = jnp.zeros_like(q_ref)
        k_ref[...] = jnp.zeros_like(k_ref)
        v_ref[...] = jnp.zeros_like(v_ref)

    @pl.when(pl.program_id(0) < LP // BLOCK_Q)
    def _():
        xb = x_ref[...].astype(BF16)
        c = _dot(xb, wc_ref[...])
        c_q = c[:, :Q_LORA_RANK]
        c_kv = c[:, Q_LORA_RANK:Q_LORA_RANK + KV_LORA_RANK]
        k_r = c[:, _WC_WIDTH - 2 * HEAD_LANES:_WC_WIDTH - HEAD_LANES]
        k_r_rot = c[:, _WC_WIDTH - HEAD_LANES:]
        qn = _rms_norm(c_q, gq_ref[...]).astype(BF16)
        kvn = _rms_norm(c_kv, gkv_ref[...]).astype(BF16)
        cos_q = tab_ref[:, 0 * HEAD_LANES:1 * HEAD_LANES]
        sin_q = tab_ref[:, 1 * HEAD_LANES:2 * HEAD_LANES]
        cos_k = tab_ref[:, 2 * HEAD_LANES:3 * HEAD_LANES]
        sin_k = tab_ref[:, 3 * HEAD_LANES:4 * HEAD_LANES]
        k_rope = k_r * cos_k + k_r_rot * sin_k

        def scatter(out_ref):
            for b in range(BATCH):
                for j in range(HEADS):
                    out_ref[b, :, j * HEAD_LANES:(j + 1) * HEAD_LANES] = _batch_rows(
                        slab_ref, j, b, steps).astype(BF16)

        for h in range(HEADS):
            sl = slice(h * HEAD_LANES, (h + 1) * HEAD_LANES)
            slab_ref[h] = _dot(qn, wqa_ref[:, sl]) * cos_q + _dot(qn, wqb_ref[:, sl]) * sin_q
        scatter(q_ref)
        for h in range(HEADS):
            sl = slice(h * HEAD_LANES, (h + 1) * HEAD_LANES)
            slab_ref[h] = _dot(kvn, wuk_ref[:, sl]) + k_rope
        scatter(k_ref)
        for j in range(v_slabs):
            sl = slice(j * HEAD_LANES, (j + 1) * HEAD_LANES)
            slab_ref[j] = _dot(kvn, wuv_ref[:, sl]) + vones_ref[:, sl]
        for b in range(BATCH):
            v_b = jnp.concatenate([_batch_rows(slab_ref, j, b, steps) for j in range(v_slabs)], axis=1)
            v_ref[b, 0] = v_b.T.astype(BF16)


def _qkv(xn, wc, gq, gkv, wqa, wqb, wuk, wuv, vones, tab):
    tm = BLOCK_Q * BATCH
    n_real = LP // BLOCK_Q
    n_tiles = 2 * N_KV_CHUNKS
    in_row = lambda w: pl.BlockSpec((tm, w), lambda i: (jnp.minimum(i, n_real - 1), 0))
    seq = lambda w: pl.BlockSpec((BATCH, BLOCK_Q, w), lambda i: (0, i, 0))
    seq_shape = jax.ShapeDtypeStruct((BATCH, n_tiles * BLOCK_Q, QKV_WIDTH), BF16)
    return pl.pallas_call(
        _qkv_kernel,
        grid=(n_tiles,),
        in_specs=[
            in_row(D_MODEL),
            _const_spec((D_MODEL, _WC_WIDTH)),
            _const_spec((1, Q_LORA_RANK)),
            _const_spec((1, KV_LORA_RANK)),
            _const_spec((Q_LORA_RANK, QKV_WIDTH)),
            _const_spec((Q_LORA_RANK, QKV_WIDTH)),
            _const_spec((KV_LORA_RANK, QKV_WIDTH)),
            _const_spec((KV_LORA_RANK, HEADS * V_ROWS)),
            _const_spec((1, HEADS * V_ROWS)),
            in_row(4 * HEAD_LANES),
        ],
        out_specs=[seq(QKV_WIDTH), seq(QKV_WIDTH),
                   pl.BlockSpec((BATCH, 1, HEADS * V_ROWS, BLOCK_Q), lambda i: (0, i, 0, 0))],
        out_shape=[seq_shape, seq_shape,
                   jax.ShapeDtypeStruct((BATCH, n_tiles, HEADS * V_ROWS, BLOCK_Q), BF16)],
        scratch_shapes=[pltpu.VMEM((HEADS, tm, HEAD_LANES), F32)],
        compiler_params=pltpu.CompilerParams(
            dimension_semantics=("arbitrary",), vmem_limit_bytes=VMEM_LIMIT),
        name="qkv_proj",
    )(xn, wc, gq, gkv, wqa, wqb, wuk, wuv, vones, tab)


KV_CHUNK = 2 * BLOCK_Q
N_KV_CHUNKS = (LP // BLOCK_Q + 1) // 2
V_ROWS = V_HEAD_DIM + 16
LOG2E = math.log2(math.e)


def _attend(q_ref, k_ref, v_ref, m_scr, acc_scr, c, mask, first):
    scores = []
    k0 = pl.multiple_of(c * KV_CHUNK, KV_CHUNK)
    for h in range(HEADS):
        sl = slice(h * HEAD_LANES, (h + 1) * HEAD_LANES)
        scores.append(lax.dot_general(
            k_ref[pl.ds(k0, KV_CHUNK), sl], q_ref[:, sl], (((1,), (1,)), ((), ())),
            preferred_element_type=F32))
    for h in range(HEADS):
        s_t = scores[h]
        if mask is not None:
            s_t = jnp.where(mask, s_t, -1e30)
        m_cur = jnp.max(s_t, axis=0, keepdims=True)
        if first:
            m_new = m_cur
        else:
            m_old = m_scr[h]
            m_new = jnp.maximum(m_old, m_cur)
        p_t = jnp.exp2(s_t - m_new).astype(BF16)
        v_t = jnp.concatenate([v_ref[2 * c, h * V_ROWS:(h + 1) * V_ROWS, :],
                               v_ref[2 * c + 1, h * V_ROWS:(h + 1) * V_ROWS, :]], axis=1)
        pv = _dot(v_t, p_t)
        if first:
            acc_scr[h] = pv
        else:
            acc_scr[h] = jnp.exp2(m_old - m_new) * acc_scr[h] + pv
        m_scr[h] = m_new


def _attn_kernel(q_ref, k_ref, v_ref, o_ref, m_scr, acc_scr):
    i = pl.program_id(1)
    n_chunks = (i + 2) // 2
    q_pos = i * BLOCK_Q + lax.broadcasted_iota(jnp.int32, (KV_CHUNK, BLOCK_Q), 1)
    k_off = lax.broadcasted_iota(jnp.int32, (KV_CHUNK, BLOCK_Q), 0)

    def mask(c):
        k_pos = c * KV_CHUNK + k_off
        return (k_pos <= q_pos) & (k_pos >= PAD)

    refs = (q_ref, k_ref, v_ref, m_scr, acc_scr)
    _attend(*refs, 0, mask(0), True)

    def middle(c, carry):
        _attend(*refs, c, None, False)
        return carry

    lax.fori_loop(1, n_chunks - 1, middle, 0)

    @pl.when(n_chunks > 1)
    def _():
        _attend(*refs, n_chunks - 1, mask(n_chunks - 1), False)

    o_t = jnp.concatenate(
        [acc_scr[h, 0:V_HEAD_DIM, :] / acc_scr[h, V_HEAD_DIM:V_HEAD_DIM + 1, :] for h in range(HEADS)],
        axis=0)
    o_ref[...] = o_t.T.astype(BF16)


def _attention(q, k, v_t):
    n_blocks = 2 * N_KV_CHUNKS
    return pl.pallas_call(
        _attn_kernel,
        grid=(BATCH, LP // BLOCK_Q),
        in_specs=[
            pl.BlockSpec((None, BLOCK_Q, QKV_WIDTH), lambda b, i: (b, i, 0)),
            pl.BlockSpec((None, n_blocks * BLOCK_Q, QKV_WIDTH), lambda b, i: (b, 0, 0)),
            pl.BlockSpec((None, n_blocks, HEADS * V_ROWS, BLOCK_Q), lambda b, i: (b, 0, 0, 0)),
        ],
        out_specs=pl.BlockSpec((None, BLOCK_Q, MIX_WIDTH), lambda b, i: (b, i, 0)),
        out_shape=jax.ShapeDtypeStruct((BATCH, LP, MIX_WIDTH), BF16),
        scratch_shapes=[pltpu.VMEM((HEADS, 1, BLOCK_Q), F32),
                        pltpu.VMEM((HEADS, V_ROWS, BLOCK_Q), F32)],
        compiler_params=pltpu.CompilerParams(
            dimension_semantics=("arbitrary", "arbitrary"), vmem_limit_bytes=VMEM_LIMIT),
        name="mla_attention",
    )(q, k, v_t)


_MIX_IN = 4 * MIX_WIDTH + 3 * D_MODEL
_HALO = 2 * BATCH


def _mix_kernel(x_ref, att_ref, win_ref, cw_ref, cb_ref, cwo_ref, bblk_ref, ar_ref, ai_ref,
                cblk_ref, d_ref, wglu_ref, bglu_ref, swo_ref, mwo_ref, wo_ref, g_ref, b_ref,
                out_ref, bu_scr, st_scr, cbuf_scr, slab_ref, *, tm):
    pid = pl.program_id(0)
    steps = tm // BATCH
    W = MIX_WIDTH

    @pl.when(pid == 0)
    def _():
        st_scr[...] = jnp.zeros_like(st_scr)
        cbuf_scr[0:_HALO, :] = jnp.zeros((_HALO, W), F32)

    x = x_ref[...]
    xb = x.astype(BF16)
    row = pid * tm + lax.broadcasted_iota(jnp.int32, (tm, 1), 0)
    valid = row >= PAD * BATCH

    pc = _dot(xb, win_ref[:, 0:3 * W])
    u = jnp.where(valid, pc[:, 2 * W:3 * W] * pc[:, 0:W], 0.0)
    cbuf_scr[_HALO:_HALO + tm, :] = u
    y = (cb_ref[...] + cw_ref[0:1, :] * cbuf_scr[0:tm, :]
         + cw_ref[1:2, :] * cbuf_scr[BATCH:BATCH + tm, :] + cw_ref[2:3, :] * u)
    cbuf_scr[0:_HALO, :] = u[tm - _HALO:, :]
    y_b = _dot((pc[:, W:2 * W] * y).astype(BF16), cwo_ref[...])

    us = jnp.where(valid, _dot(xb, win_ref[:, 3 * W:4 * W]), 0.0)
    usb = us.astype(BF16)
    nb = S5_BLOCK_STATES
    for blk in range(S5_BLOCKS):
        bu_scr[:, 2 * nb * blk:2 * nb * (blk + 1)] = _dot(
            usb[:, blk * 128:(blk + 1) * 128], bblk_ref[blk])
    for blk in range(S5_BLOCKS):
        c_re = 2 * nb * blk
        c_im = c_re + nb
        a_re = ar_ref[blk]
        a_im = ai_ref[blk]

        def step(t, carry, c_re=c_re, c_im=c_im, a_re=a_re, a_im=a_im):
            s_re, s_im = carry
            r0 = pl.multiple_of(t * BATCH, BATCH)
            n_re = a_re * s_re - a_im * s_im + bu_scr[pl.ds(r0, BATCH), c_re:c_re + nb]
            n_im = a_re * s_im + a_im * s_re + bu_scr[pl.ds(r0, BATCH), c_im:c_im + nb]
            bu_scr[pl.ds(r0, BATCH), c_re:c_re + nb] = n_re
            bu_scr[pl.ds(r0, BATCH), c_im:c_im + nb] = n_im
            return n_re, n_im

        s_re, s_im = lax.fori_loop(
            0, steps, step, (st_scr[:, c_re:c_re + nb], st_scr[:, c_im:c_im + nb]), unroll=4)
        st_scr[:, c_re:c_re + nb] = s_re
        st_scr[:, c_im:c_im + nb] = s_im
    y = jnp.concatenate(
        [_dot(bu_scr[:, 2 * nb * blk:2 * nb * (blk + 1)].astype(BF16), cblk_ref[blk])
         for blk in range(S5_BLOCKS)], axis=1)
    y = _gelu_tanh(y + d_ref[...] * us)
    y = y * _sigmoid(_dot(y.astype(BF16), wglu_ref[...]) + bglu_ref[...])
    y_c = _dot(y.astype(BF16), swo_ref[...])

    att_slabs = W // HEAD_LANES
    for b in range(BATCH):
        att_b = att_ref[b].astype(F32)
        for j in range(att_slabs):
            slab_ref[j, pl.ds(b, steps, stride=BATCH), :] = att_b[:, j * HEAD_LANES:(j + 1) * HEAD_LANES]
    att = jnp.concatenate([slab_ref[j] for j in range(att_slabs)], axis=1)
    y_a = _dot(att.astype(BF16), mwo_ref[...])

    g0 = 4 * W
    mixed = _sigmoid(_dot(xb, win_ref[:, g0:g0 + D_MODEL])) * y_a
    mixed += _sigmoid(_dot(xb, win_ref[:, g0 + D_MODEL:g0 + 2 * D_MODEL])) * y_b
    mixed += _sigmoid(_dot(xb, win_ref[:, g0 + 2 * D_MODEL:g0 + 3 * D_MODEL])) * y_c
    z = ALPHA * x + _dot(mixed.astype(BF16), wo_ref[...])
    out_ref[...] = _layer_norm(z, g_ref[...], b_ref[...])


def _mix(xn, att, win, cw, cb, cwo, bblk, ar, ai, cblk, d, wglu, bglu, swo, mwo, wo, g, b, *, tm=512):
    rows = xn.shape[0]
    consts = (win, cw, cb, cwo, bblk, ar, ai, cblk, d, wglu, bglu, swo, mwo, wo, g, b)
    return pl.pallas_call(
        functools.partial(_mix_kernel, tm=tm),
        grid=(rows // tm,),
        in_specs=[pl.BlockSpec((tm, D_MODEL), lambda i: (i, 0)),
                  pl.BlockSpec((BATCH, tm // BATCH, MIX_WIDTH), lambda i: (0, i, 0))]
        + [_const_spec(c.shape) for c in consts],
        out_specs=pl.BlockSpec((tm, D_MODEL), lambda i: (i, 0)),
        out_shape=jax.ShapeDtypeStruct((rows, D_MODEL), F32),
        scratch_shapes=[
            pltpu.VMEM((tm, S5_COLS), F32),
            pltpu.VMEM((BATCH, S5_COLS), F32),
            pltpu.VMEM((tm + _HALO, MIX_WIDTH), F32),
            pltpu.VMEM((MIX_WIDTH // HEAD_LANES, tm, HEAD_LANES), F32),
        ],
        compiler_params=pltpu.CompilerParams(
            dimension_semantics=("arbitrary",), vmem_limit_bytes=VMEM_LIMIT),
        name="mixers_merge",
    )(xn, att, *consts)


def _rope_tables():
    pos = (jnp.arange(LP) - PAD).astype(F32)
    inv_freq = ROPE_BASE ** (-jnp.arange(0, QK_ROPE_DIM, 2, dtype=F32) / QK_ROPE_DIM)
    ang = pos[:, None] * inv_freq[None, :]
    cos2 = jnp.tile(jnp.cos(ang), (1, 2))
    sin2 = jnp.tile(jnp.sin(ang), (1, 2))
    zn = jnp.zeros((LP, QK_NOPE_DIM), F32)
    zt = jnp.zeros((LP, HEAD_LANES - QK_HEAD_DIM), F32)
    scale = QK_HEAD_DIM ** -0.5 * LOG2E
    cos_q = scale * jnp.concatenate([jnp.ones_like(zn), cos2, zt], axis=1)
    sin_q = scale * jnp.concatenate([zn, sin2, zt], axis=1)
    cos_k = jnp.concatenate([zn, cos2, zt], axis=1)
    sin_k = jnp.concatenate([zn, sin2, zt], axis=1)
    tab = jnp.concatenate([cos_q, sin_q, cos_k, sin_k], axis=1)
    return jnp.repeat(tab, BATCH, axis=0)


def _rot_half_cols(w):
    half = QK_ROPE_DIM // 2
    return jnp.concatenate([-w[..., half:], w[..., :half]], axis=-1)


def _qkv_weights(w_in, w_uq, w_ukv):
    zpad = lambda n: jnp.zeros((D_MODEL, n), F32)
    kr = w_in[:, _OFF_KR:_OFF_CONV]
    tail = HEAD_LANES - QK_HEAD_DIM
    wc = jnp.concatenate([
        w_in[:, :_OFF_KR],
        zpad(QK_NOPE_DIM), kr, zpad(tail),
        zpad(QK_NOPE_DIM), _rot_half_cols(kr), zpad(tail)], axis=1)
    uq = w_uq.reshape(Q_LORA_RANK, HEADS, QK_HEAD_DIM)
    zq = lambda n: jnp.zeros((Q_LORA_RANK, HEADS, n), F32)
    wqa = jnp.concatenate([uq, zq(tail)], axis=-1)
    wqb = jnp.concatenate([zq(QK_NOPE_DIM), _rot_half_cols(uq[..., QK_NOPE_DIM:]), zq(tail)], axis=-1)
    ukv = w_ukv.reshape(KV_LORA_RANK, HEADS, QK_NOPE_DIM + V_HEAD_DIM)
    zk = jnp.zeros((KV_LORA_RANK, HEADS, HEAD_LANES - QK_NOPE_DIM), F32)
    wuk = jnp.concatenate([ukv[..., :QK_NOPE_DIM], zk], axis=-1)
    wuv = jnp.concatenate(
        [ukv[..., QK_NOPE_DIM:], jnp.zeros((KV_LORA_RANK, HEADS, V_ROWS - V_HEAD_DIM), F32)], axis=-1)
    vones = jnp.zeros((HEADS, V_ROWS), F32).at[:, V_HEAD_DIM].set(1.0).reshape(1, HEADS * V_ROWS)
    flat = lambda w: w.reshape(w.shape[0], -1).astype(BF16)
    return wc.astype(BF16), flat(wqa), flat(wqb), flat(wuk), flat(wuv), vones


def _s5_weights(a_re, a_im, log_dt, b_re, b_im, c_re, c_im):
    dt = jnp.exp(log_dt)[:, None]
    mag = jnp.exp(dt * a_re)
    ab_re, ab_im = mag * jnp.cos(dt * a_im), mag * jnp.sin(dt * a_im)
    den = a_re * a_re + a_im * a_im
    nr, ni = ab_re - 1.0, ab_im
    coef_re = (nr * a_re + ni * a_im) / den
    coef_im = (ni * a_re - nr * a_im) / den
    bb_re = coef_re[..., None] * b_re - coef_im[..., None] * b_im
    bb_im = coef_re[..., None] * b_im + coef_im[..., None] * b_re
    gpb = S5_GROUPS // S5_BLOCKS
    eye = jnp.eye(gpb, dtype=F32)

    def in_blocks(bb):
        t = bb.transpose(0, 2, 1).reshape(S5_BLOCKS, gpb, S5_GROUP, S5_STATE)
        return jnp.einsum('bghn,gk->bghkn', t, eye).reshape(S5_BLOCKS, gpb * S5_GROUP, gpb * S5_STATE)

    def out_blocks(cc):
        t = cc.reshape(S5_BLOCKS, gpb, S5_GROUP, S5_STATE)
        return jnp.einsum('bghn,gk->bgnkh', t, eye).reshape(S5_BLOCKS, gpb * S5_STATE, gpb * S5_GROUP)

    bblk = jnp.concatenate([in_blocks(bb_re), in_blocks(bb_im)], axis=2).astype(BF16)
    cblk = jnp.concatenate([out_blocks(c_re), out_blocks(-c_im)], axis=1).astype(BF16)
    bcast = lambda a: jnp.broadcast_to(
        a.reshape(S5_BLOCKS, 1, S5_BLOCK_STATES), (S5_BLOCKS, BATCH, S5_BLOCK_STATES))
    return bblk, bcast(ab_re), bcast(ab_im), cblk


def kernel(x, meta, ffn1_w_gate, ffn1_w_up, ffn1_w_down, ln1_g, ln1_b, w_in, mla_q_norm_g, mla_w_uq, mla_kv_norm_g, mla_w_ukv, mla_w_o, conv_w, conv_b, conv_w_out, s5_a_re, s5_a_im, s5_log_dt, s5_b_re, s5_b_im, s5_c_re, s5_c_im, s5_d, s5_w_glu, s5_b_glu, s5_w_out, w_o, ln2_g, ln2_b, ffn2_w_gate, ffn2_w_up, ffn2_w_down, ln3_g, ln3_b):
    row = lambda v: v.reshape(1, -1).astype(F32)
    bf = lambda w: w.astype(BF16)
    h = jnp.concatenate([
        jnp.zeros((PAD, BATCH, D_MODEL), F32),
        jnp.broadcast_to(meta[:, None, :].astype(F32), (N_META, BATCH, D_MODEL)),
        jnp.transpose(x, (1, 0, 2))], axis=0).reshape(ROWS, D_MODEL)
    tab = _rope_tables()
    for i in range(DEPTH):
        xn = _ffn_ln(h, bf(ffn1_w_gate[i]), bf(ffn1_w_up[i]), bf(ffn1_w_down[i]), row(ln1_g[i]), row(ln1_b[i]))
        wc, wqa, wqb, wuk, wuv, vones = _qkv_weights(w_in[i], mla_w_uq[i], mla_w_ukv[i])
        q, k, v = _qkv(xn, wc, row(mla_q_norm_g[i]), row(mla_kv_norm_g[i]), wqa, wqb, wuk, wuv, vones, tab)
        att = _attention(q, k, v)
        bblk, ar, ai, cblk = _s5_weights(s5_a_re[i], s5_a_im[i], s5_log_dt[i], s5_b_re[i], s5_b_im[i],
                                         s5_c_re[i], s5_c_im[i])
        h = _mix(xn, att, bf(w_in[i][:, _OFF_CONV:]), conv_w[i].astype(F32), row(conv_b[i]),
                 bf(conv_w_out[i]), bblk, ar, ai, cblk, row(s5_d[i]), bf(s5_w_glu[i]), row(s5_b_glu[i]),
                 bf(s5_w_out[i]), bf(mla_w_o[i]), bf(w_o[i]), row(ln2_g[i]), row(ln2_b[i]))
        h = _ffn_ln(h, bf(ffn2_w_gate[i]), bf(ffn2_w_up[i]), bf(ffn2_w_down[i]), row(ln3_g[i]), row(ln3_b[i]))
    out = h.reshape(LP, BATCH, D_MODEL)[PAD + N_META:]
    return jnp.transpose(out, (1, 0, 2))
```

```python
import functools
import math

import jax
import jax.numpy as jnp
from jax import lax
from jax.experimental import pallas as pl
from jax.experimental.pallas import tpu as pltpu

D_MODEL = 1024
BATCH = 8
SEQ = 2048
DEPTH = 2
N_META = 16
BLOCK_Q = 128
MIX_WIDTH = D_MODEL // 2
HEADS = 8
V_HEAD_DIM = 64
QK_NOPE_DIM = 64
QK_ROPE_DIM = 32
QK_HEAD_DIM = QK_NOPE_DIM + QK_ROPE_DIM
Q_LORA_RANK = 384
KV_LORA_RANK = 256
ROPE_BASE = 10000.0
S5_GROUP = 16
S5_GROUPS = 32
S5_STATE = 64
D_FF = 2816
ALPHA = (2.0 * DEPTH) ** 0.25
LN_EPS = 1e-5
RMS_EPS = 1e-6

HEAD_LANES = 128
QKV_WIDTH = HEADS * HEAD_LANES
PAD = (-(N_META + SEQ)) % BLOCK_Q
LP = PAD + N_META + SEQ
ROWS = LP * BATCH
S5_BLOCKS = 4
S5_BLOCK_STATES = (S5_GROUPS // S5_BLOCKS) * S5_STATE
S5_COLS = 2 * S5_GROUPS * S5_STATE

_OFF_CQ, _OFF_CKV, _OFF_KR, _OFF_CONV = 0, 384, 640, 672
_OFF_GATES = _OFF_CONV + 4 * MIX_WIDTH
D_IN = _OFF_GATES + 3 * D_MODEL

V7X_VMEM_BYTES = 64 * 1024 * 1024
VMEM_LIMIT = 56 * 1024 * 1024

F32 = jnp.float32
BF16 = jnp.bfloat16


def _dot(a, b):
    return jnp.dot(a, b, preferred_element_type=F32)


def _const_spec(shape):
    nd = len(shape)
    return pl.BlockSpec(shape, lambda *_: (0,) * nd, pipeline_mode=pl.Buffered(1))


def _layer_norm(y, g, b):
    mu = jnp.mean(y, axis=-1, keepdims=True)
    yc = y - mu
    var = jnp.mean(yc * yc, axis=-1, keepdims=True)
    return yc * lax.rsqrt(var + LN_EPS) * g + b


def _rms_norm(y, g):
    return y * lax.rsqrt(jnp.mean(y * y, axis=-1, keepdims=True) + RMS_EPS) * g


def _sigmoid(y):
    return 1.0 / (1.0 + jnp.exp(-y))


def _gelu_tanh(y):
    return 0.5 * y * (1.0 + jnp.tanh(math.sqrt(2.0 / math.pi) * (y + 0.044715 * (y * y * y))))


def _ffn_ln_kernel(x_ref, wg_ref, wu_ref, wd_ref, g_ref, b_ref, o_ref, *, ff_chunk):
    x = x_ref[...]
    xb = x.astype(BF16)
    acc = None
    for c in range(D_FF // ff_chunk):
        sl = slice(c * ff_chunk, (c + 1) * ff_chunk)
        gate = _dot(xb, wg_ref[:, sl])
        up = _dot(xb, wu_ref[:, sl])
        hmid = (gate * _sigmoid(gate) * up).astype(BF16)
        part = _dot(hmid, wd_ref[sl, :])
        acc = part if acc is None else acc + part
    o_ref[...] = _layer_norm(ALPHA * x + 0.5 * acc, g_ref[...], b_ref[...])


def _ffn_ln(x, wg, wu, wd, g, b, *, tm=512, ff_chunk=1408):
    rows = x.shape[0]
    return pl.pallas_call(
        functools.partial(_ffn_ln_kernel, ff_chunk=ff_chunk),
        grid=(rows // tm,),
        in_specs=[
            pl.BlockSpec((tm, D_MODEL), lambda i: (i, 0)),
            _const_spec((D_MODEL, D_FF)),
            _const_spec((D_MODEL, D_FF)),
            _const_spec((D_FF, D_MODEL)),
            _const_spec((1, D_MODEL)),
            _const_spec((1, D_MODEL)),
        ],
        out_specs=pl.BlockSpec((tm, D_MODEL), lambda i: (i, 0)),
        out_shape=jax.ShapeDtypeStruct((rows, D_MODEL), F32),
        compiler_params=pltpu.CompilerParams(
            dimension_semantics=("arbitrary",), vmem_limit_bytes=VMEM_LIMIT),
        name="ffn_ln",
    )(x, wg, wu, wd, g, b)


_WC_WIDTH = Q_LORA_RANK + KV_LORA_RANK + 2 * HEAD_LANES


def _batch_rows(slab_ref, j, b, steps):
    return slab_ref[j, pl.ds(b, steps, stride=BATCH), :]


def _qkv_kernel(x_ref, wc_ref, gq_ref, gkv_ref, wqa_ref, wqb_ref, wuk_ref, wuv_ref, vones_ref, tab_ref,
                q_ref, k_ref, v_ref, slab_ref):
    steps = BLOCK_Q
    v_slabs = HEADS * V_ROWS // HEAD_LANES

    @pl.when(pl.program_id(0) == 0)
    def _():
        q_ref[...] = jnp.zeros_like(q_ref)
        k_ref[...] = jnp.zeros_like(k_ref)
        v_ref[...] = jnp.zeros_like(v_ref)

    @pl.when(pl.program_id(0) > 0)
    def _():
        xb = x_ref[...].astype(BF16)
        c = _dot(xb, wc_ref[...])
        c_q = c[:, :Q_LORA_RANK]
        c_kv = c[:, Q_LORA_RANK:Q_LORA_RANK + KV_LORA_RANK]
        k_r = c[:, _WC_WIDTH - 2 * HEAD_LANES:_WC_WIDTH - HEAD_LANES]
        k_r_rot = c[:, _WC_WIDTH - HEAD_LANES:]
        qn = _rms_norm(c_q, gq_ref[...]).astype(BF16)
        kvn = _rms_norm(c_kv, gkv_ref[...]).astype(BF16)
        cos_q = tab_ref[:, 0 * HEAD_LANES:1 * HEAD_LANES]
        sin_q = tab_ref[:, 1 * HEAD_LANES:2 * HEAD_LANES]
        cos_k = tab_ref[:, 2 * HEAD_LANES:3 * HEAD_LANES]
        sin_k = tab_ref[:, 3 * HEAD_LANES:4 * HEAD_LANES]
        k_rope = k_r * cos_k + k_r_rot * sin_k

        def scatter(out_ref):
            for b in range(BATCH):
                for j in range(HEADS):
                    out_ref[b, :, j * HEAD_LANES:(j + 1) * HEAD_LANES] = _batch_rows(
                        slab_ref, j, b, steps).astype(BF16)

        for h in range(HEADS):
            sl = slice(h * HEAD_LANES, (h + 1) * HEAD_LANES)
            slab_ref[h] = _dot(qn, wqa_ref[:, sl]) * cos_q + _dot(qn, wqb_ref[:, sl]) * sin_q
        scatter(q_ref)
        for h in range(HEADS):
            sl = slice(h * HEAD_LANES, (h + 1) * HEAD_LANES)
            slab_ref[h] = _dot(kvn, wuk_ref[:, sl]) + k_rope
        scatter(k_ref)
        for j in range(v_slabs):
            sl = slice(j * HEAD_LANES, (j + 1) * HEAD_LANES)
            slab_ref[j] = _dot(kvn, wuv_ref[:, sl]) + vones_ref[:, sl]
        for b in range(BATCH):
            v_b = jnp.concatenate([_batch_rows(slab_ref, j, b, steps) for j in range(v_slabs)], axis=1)
            v_ref[b, 0] = v_b.T.astype(BF16)


def _qkv(xn, wc, gq, gkv, wqa, wqb, wuk, wuv, vones, tab):
    tm = BLOCK_Q * BATCH
    n_tiles = ATT_LEN // BLOCK_Q
    in_row = lambda w: pl.BlockSpec((tm, w), lambda i: (jnp.maximum(i - 1, 0), 0))
    seq = lambda w: pl.BlockSpec((BATCH, BLOCK_Q, w), lambda i: (0, i, 0))
    seq_shape = jax.ShapeDtypeStruct((BATCH, ATT_LEN, QKV_WIDTH), BF16)
    return pl.pallas_call(
        _qkv_kernel,
        grid=(n_tiles,),
        in_specs=[
            in_row(D_MODEL),
            _const_spec((D_MODEL, _WC_WIDTH)),
            _const_spec((1, Q_LORA_RANK)),
            _const_spec((1, KV_LORA_RANK)),
            _const_spec((Q_LORA_RANK, QKV_WIDTH)),
            _const_spec((Q_LORA_RANK, QKV_WIDTH)),
            _const_spec((KV_LORA_RANK, QKV_WIDTH)),
            _const_spec((KV_LORA_RANK, HEADS * V_ROWS)),
            _const_spec((1, HEADS * V_ROWS)),
            in_row(4 * HEAD_LANES),
        ],
        out_specs=[seq(QKV_WIDTH), seq(QKV_WIDTH),
                   pl.BlockSpec((BATCH, 1, HEADS * V_ROWS, BLOCK_Q), lambda i: (0, i, 0, 0))],
        out_shape=[seq_shape, seq_shape,
                   jax.ShapeDtypeStruct((BATCH, n_tiles, HEADS * V_ROWS, BLOCK_Q), BF16)],
        scratch_shapes=[pltpu.VMEM((HEADS, tm, HEAD_LANES), F32)],
        compiler_params=pltpu.CompilerParams(
            dimension_semantics=("arbitrary",), vmem_limit_bytes=VMEM_LIMIT),
        name="qkv_proj",
    )(xn, wc, gq, gkv, wqa, wqb, wuk, wuv, vones, tab)


ATT_TILE = 2 * BLOCK_Q
ATT_PAD = ATT_TILE - N_META
ATT_LEN = ATT_PAD + N_META + SEQ
N_ATT_TILES = ATT_LEN // ATT_TILE
V_ROWS = V_HEAD_DIM + 16
LOG2E = math.log2(math.e)


def _attend(q_ref, k_ref, v_ref, m_scr, acc_scr, c, mask, first):
    scores = []
    k0 = pl.multiple_of(c * ATT_TILE, ATT_TILE)
    for h in range(HEADS):
        sl = slice(h * HEAD_LANES, (h + 1) * HEAD_LANES)
        scores.append(lax.dot_general(
            k_ref[pl.ds(k0, ATT_TILE), sl], q_ref[:, sl], (((1,), (1,)), ((), ())),
            preferred_element_type=F32))
    for h in range(HEADS):
        s_t = scores[h]
        if mask is not None:
            s_t = jnp.where(mask, s_t, -1e30)
        m_cur = jnp.max(s_t, axis=0, keepdims=True)
        if first:
            m_new = m_cur
        else:
            m_old = m_scr[h]
            m_new = jnp.maximum(m_old, m_cur)
        p_t = jnp.exp2(s_t - m_new).astype(BF16)
        v_t = jnp.concatenate([v_ref[2 * c, h * V_ROWS:(h + 1) * V_ROWS, :],
                               v_ref[2 * c + 1, h * V_ROWS:(h + 1) * V_ROWS, :]], axis=1)
        pv = _dot(v_t, p_t)
        if first:
            acc_scr[h] = pv
        else:
            acc_scr[h] = jnp.exp2(m_old - m_new) * acc_scr[h] + pv
        m_scr[h] = m_new


def _attn_kernel(q_ref, k_ref, v_ref, o_ref, m_scr, acc_scr):
    i = pl.program_id(1)
    q_pos = i * ATT_TILE + lax.broadcasted_iota(jnp.int32, (ATT_TILE, ATT_TILE), 1)
    k_off = lax.broadcasted_iota(jnp.int32, (ATT_TILE, ATT_TILE), 0)

    def mask(c):
        k_pos = c * ATT_TILE + k_off
        return (k_pos <= q_pos) & (k_pos >= ATT_PAD)

    refs = (q_ref, k_ref, v_ref, m_scr, acc_scr)
    _attend(*refs, 0, mask(0), True)

    def middle(c, carry):
        _attend(*refs, c, None, False)
        return carry

    lax.fori_loop(1, i, middle, 0)

    @pl.when(i > 0)
    def _():
        _attend(*refs, i, mask(i), False)

    o_t = jnp.concatenate(
        [acc_scr[h, 0:V_HEAD_DIM, :] / acc_scr[h, V_HEAD_DIM:V_HEAD_DIM + 1, :] for h in range(HEADS)],
        axis=0)
    o_ref[...] = o_t.T.astype(BF16)


def _attention(q, k, v_t):
    return pl.pallas_call(
        _attn_kernel,
        grid=(BATCH, N_ATT_TILES),
        in_specs=[
            pl.BlockSpec((None, ATT_TILE, QKV_WIDTH), lambda b, i: (b, i, 0)),
            pl.BlockSpec((None, ATT_LEN, QKV_WIDTH), lambda b, i: (b, 0, 0)),
            pl.BlockSpec((None, ATT_LEN // BLOCK_Q, HEADS * V_ROWS, BLOCK_Q), lambda b, i: (b, 0, 0, 0)),
        ],
        out_specs=pl.BlockSpec((None, ATT_TILE, MIX_WIDTH), lambda b, i: (b, i, 0)),
        out_shape=jax.ShapeDtypeStruct((BATCH, ATT_LEN, MIX_WIDTH), BF16),
        scratch_shapes=[pltpu.VMEM((HEADS, 1, ATT_TILE), F32),
                        pltpu.VMEM((HEADS, V_ROWS, ATT_TILE), F32)],
        compiler_params=pltpu.CompilerParams(
            dimension_semantics=("arbitrary", "arbitrary"), vmem_limit_bytes=VMEM_LIMIT),
        name="mla_attention",
    )(q, k, v_t)


_MIX_IN = 4 * MIX_WIDTH + 3 * D_MODEL
_HALO = 2 * BATCH


def _mix_kernel(x_ref, att_ref, win_ref, cw_ref, cb_ref, cwo_ref, bblk_ref, ar_ref, ai_ref,
                cblk_ref, d_ref, wglu_ref, bglu_ref, swo_ref, mwo_ref, wo_ref, g_ref, b_ref,
                out_ref, bu_scr, st_scr, cbuf_scr, slab_ref, *, tm):
    pid = pl.program_id(0)
    steps = tm // BATCH
    W = MIX_WIDTH

    @pl.when(pid == 0)
    def _():
        st_scr[...] = jnp.zeros_like(st_scr)
        cbuf_scr[0:_HALO, :] = jnp.zeros((_HALO, W), F32)

    x = x_ref[...]
    xb = x.astype(BF16)
    row = pid * tm + lax.broadcasted_iota(jnp.int32, (tm, 1), 0)
    valid = row >= PAD * BATCH

    pc = _dot(xb, win_ref[:, 0:3 * W])
    u = jnp.where(valid, pc[:, 2 * W:3 * W] * pc[:, 0:W], 0.0)
    cbuf_scr[_HALO:_HALO + tm, :] = u
    y = (cb_ref[...] + cw_ref[0:1, :] * cbuf_scr[0:tm, :]
         + cw_ref[1:2, :] * cbuf_scr[BATCH:BATCH + tm, :] + cw_ref[2:3, :] * u)
    cbuf_scr[0:_HALO, :] = u[tm - _HALO:, :]
    y_b = _dot((pc[:, W:2 * W] * y).astype(BF16), cwo_ref[...])

    us = jnp.where(valid, _dot(xb, win_ref[:, 3 * W:4 * W]), 0.0)
    usb = us.astype(BF16)
    nb = S5_BLOCK_STATES
    for blk in range(S5_BLOCKS):
        bu_scr[:, 2 * nb * blk:2 * nb * (blk + 1)] = _dot(
            usb[:, blk * 128:(blk + 1) * 128], bblk_ref[blk])
    for blk in range(S5_BLOCKS):
        c_re = 2 * nb * blk
        c_im = c_re + nb
        a_re = ar_ref[blk]
        a_im = ai_ref[blk]

        def step(t, carry, c_re=c_re, c_im=c_im, a_re=a_re, a_im=a_im):
            s_re, s_im = carry
            r0 = pl.multiple_of(t * BATCH, BATCH)
            n_re = a_re * s_re - a_im * s_im + bu_scr[pl.ds(r0, BATCH), c_re:c_re + nb]
            n_im = a_re * s_im + a_im * s_re + bu_scr[pl.ds(r0, BATCH), c_im:c_im + nb]
            bu_scr[pl.ds(r0, BATCH), c_re:c_re + nb] = n_re
            bu_scr[pl.ds(r0, BATCH), c_im:c_im + nb] = n_im
            return n_re, n_im

        s_re, s_im = lax.fori_loop(
            0, steps, step, (st_scr[:, c_re:c_re + nb], st_scr[:, c_im:c_im + nb]), unroll=True)
        st_scr[:, c_re:c_re + nb] = s_re
        st_scr[:, c_im:c_im + nb] = s_im
    y = jnp.concatenate(
        [_dot(bu_scr[:, 2 * nb * blk:2 * nb * (blk + 1)].astype(BF16), cblk_ref[blk])
         for blk in range(S5_BLOCKS)], axis=1)
    y = _gelu_tanh(y + d_ref[...] * us)
    y = y * _sigmoid(_dot(y.astype(BF16), wglu_ref[...]) + bglu_ref[...])
    y_c = _dot(y.astype(BF16), swo_ref[...])

    att_slabs = W // HEAD_LANES
    for b in range(BATCH):
        att_b = att_ref[b].astype(F32)
        for j in range(att_slabs):
            slab_ref[j, pl.ds(b, steps, stride=BATCH), :] = att_b[:, j * HEAD_LANES:(j + 1) * HEAD_LANES]
    att = jnp.concatenate([slab_ref[j] for j in range(att_slabs)], axis=1)
    y_a = _dot(att.astype(BF16), mwo_ref[...])

    g0 = 4 * W
    mixed = _sigmoid(_dot(xb, win_ref[:, g0:g0 + D_MODEL])) * y_a
    mixed += _sigmoid(_dot(xb, win_ref[:, g0 + D_MODEL:g0 + 2 * D_MODEL])) * y_b
    mixed += _sigmoid(_dot(xb, win_ref[:, g0 + 2 * D_MODEL:g0 + 3 * D_MODEL])) * y_c
    z = ALPHA * x + _dot(mixed.astype(BF16), wo_ref[...])
    out_ref[...] = _layer_norm(z, g_ref[...], b_ref[...])


def _mix(xn, att, win, cw, cb, cwo, bblk, ar, ai, cblk, d, wglu, bglu, swo, mwo, wo, g, b, *, tm=512):
    rows = xn.shape[0]
    consts = (win, cw, cb, cwo, bblk, ar, ai, cblk, d, wglu, bglu, swo, mwo, wo, g, b)
    return pl.pallas_call(
        functools.partial(_mix_kernel, tm=tm),
        grid=(rows // tm,),
        in_specs=[pl.BlockSpec((tm, D_MODEL), lambda i: (i, 0)),
                  pl.BlockSpec((BATCH, tm // BATCH, MIX_WIDTH),
                               lambda i: (0, i + BLOCK_Q * BATCH // tm, 0))]
        + [_const_spec(c.shape) for c in consts],
        out_specs=pl.BlockSpec((tm, D_MODEL), lambda i: (i, 0)),
        out_shape=jax.ShapeDtypeStruct((rows, D_MODEL), F32),
        scratch_shapes=[
            pltpu.VMEM((tm, S5_COLS), F32),
            pltpu.VMEM((BATCH, S5_COLS), F32),
            pltpu.VMEM((tm + _HALO, MIX_WIDTH), F32),
            pltpu.VMEM((MIX_WIDTH // HEAD_LANES, tm, HEAD_LANES), F32),
        ],
        compiler_params=pltpu.CompilerParams(
            dimension_semantics=("arbitrary",), vmem_limit_bytes=VMEM_LIMIT),
        name="mixers_merge",
    )(xn, att, *consts)


def _rope_tables():
    pos = (jnp.arange(LP) - PAD).astype(F32)
    inv_freq = ROPE_BASE ** (-jnp.arange(0, QK_ROPE_DIM, 2, dtype=F32) / QK_ROPE_DIM)
    ang = pos[:, None] * inv_freq[None, :]
    cos2 = jnp.tile(jnp.cos(ang), (1, 2))
    sin2 = jnp.tile(jnp.sin(ang), (1, 2))
    zn = jnp.zeros((LP, QK_NOPE_DIM), F32)
    zt = jnp.zeros((LP, HEAD_LANES - QK_HEAD_DIM), F32)
    scale = QK_HEAD_DIM ** -0.5 * LOG2E
    cos_q = scale * jnp.concatenate([jnp.ones_like(zn), cos2, zt], axis=1)
    sin_q = scale * jnp.concatenate([zn, sin2, zt], axis=1)
    cos_k = jnp.concatenate([zn, cos2, zt], axis=1)
    sin_k = jnp.concatenate([zn, sin2, zt], axis=1)
    tab = jnp.concatenate([cos_q, sin_q, cos_k, sin_k], axis=1)
    return jnp.repeat(tab, BATCH, axis=0)


def _rot_half_cols(w):
    half = QK_ROPE_DIM // 2
    return jnp.concatenate([-w[..., half:], w[..., :half]], axis=-1)


def _qkv_weights(w_in, w_uq, w_ukv):
    zpad = lambda n: jnp.zeros((D_MODEL, n), F32)
    kr = w_in[:, _OFF_KR:_OFF_CONV]
    tail = HEAD_LANES - QK_HEAD_DIM
    wc = jnp.concatenate([
        w_in[:, :_OFF_KR],
        zpad(QK_NOPE_DIM), kr, zpad(tail),
        zpad(QK_NOPE_DIM), _rot_half_cols(kr), zpad(tail)], axis=1)
    uq = w_uq.reshape(Q_LORA_RANK, HEADS, QK_HEAD_DIM)
    zq = lambda n: jnp.zeros((Q_LORA_RANK, HEADS, n), F32)
    wqa = jnp.concatenate([uq, zq(tail)], axis=-1)
    wqb = jnp.concatenate([zq(QK_NOPE_DIM), _rot_half_cols(uq[..., QK_NOPE_DIM:]), zq(tail)], axis=-1)
    ukv = w_ukv.reshape(KV_LORA_RANK, HEADS, QK_NOPE_DIM + V_HEAD_DIM)
    zk = jnp.zeros((KV_LORA_RANK, HEADS, HEAD_LANES - QK_NOPE_DIM), F32)
    wuk = jnp.concatenate([ukv[..., :QK_NOPE_DIM], zk], axis=-1)
    wuv = jnp.concatenate(
        [ukv[..., QK_NOPE_DIM:], jnp.zeros((KV_LORA_RANK, HEADS, V_ROWS - V_HEAD_DIM), F32)], axis=-1)
    vones = jnp.zeros((HEADS, V_ROWS), F32).at[:, V_HEAD_DIM].set(1.0).reshape(1, HEADS * V_ROWS)
    flat = lambda w: w.reshape(w.shape[0], -1).astype(BF16)
    return wc.astype(BF16), flat(wqa), flat(wqb), flat(wuk), flat(wuv), vones


def _s5_weights(a_re, a_im, log_dt, b_re, b_im, c_re, c_im):
    dt = jnp.exp(log_dt)[:, None]
    mag = jnp.exp(dt * a_re)
    ab_re, ab_im = mag * jnp.cos(dt * a_im), mag * jnp.sin(dt * a_im)
    den = a_re * a_re + a_im * a_im
    nr, ni = ab_re - 1.0, ab_im
    coef_re = (nr * a_re + ni * a_im) / den
    coef_im = (ni * a_re - nr * a_im) / den
    bb_re = coef_re[..., None] * b_re - coef_im[..., None] * b_im
    bb_im = coef_re[..., None] * b_im + coef_im[..., None] * b_re
    gpb = S5_GROUPS // S5_BLOCKS
    eye = jnp.eye(gpb, dtype=F32)

    def in_blocks(bb):
        t = bb.transpose(0, 2, 1).reshape(S5_BLOCKS, gpb, S5_GROUP, S5_STATE)
        return jnp.einsum('bghn,gk->bghkn', t, eye).reshape(S5_BLOCKS, gpb * S5_GROUP, gpb * S5_STATE)

    def out_blocks(cc):
        t = cc.reshape(S5_BLOCKS, gpb, S5_GROUP, S5_STATE)
        return jnp.einsum('bghn,gk->bgnkh', t, eye).reshape(S5_BLOCKS, gpb * S5_STATE, gpb * S5_GROUP)

    bblk = jnp.concatenate([in_blocks(bb_re), in_blocks(bb_im)], axis=2).astype(BF16)
    cblk = jnp.concatenate([out_blocks(c_re), out_blocks(-c_im)], axis=1).astype(BF16)
    bcast = lambda a: jnp.broadcast_to(
        a.reshape(S5_BLOCKS, 1, S5_BLOCK_STATES), (S5_BLOCKS, BATCH, S5_BLOCK_STATES))
    return bblk, bcast(ab_re), bcast(ab_im), cblk


def kernel(x, meta, ffn1_w_gate, ffn1_w_up, ffn1_w_down, ln1_g, ln1_b, w_in, mla_q_norm_g, mla_w_uq, mla_kv_norm_g, mla_w_ukv, mla_w_o, conv_w, conv_b, conv_w_out, s5_a_re, s5_a_im, s5_log_dt, s5_b_re, s5_b_im, s5_c_re, s5_c_im, s5_d, s5_w_glu, s5_b_glu, s5_w_out, w_o, ln2_g, ln2_b, ffn2_w_gate, ffn2_w_up, ffn2_w_down, ln3_g, ln3_b):
    row = lambda v: v.reshape(1, -1).astype(F32)
    bf = lambda w: w.astype(BF16)
    h = jnp.concatenate([
        jnp.zeros((PAD, BATCH, D_MODEL), F32),
        jnp.broadcast_to(meta[:, None, :].astype(F32), (N_META, BATCH, D_MODEL)),
        jnp.transpose(x, (1, 0, 2))], axis=0).reshape(ROWS, D_MODEL)
    tab = _rope_tables()
    for i in range(DEPTH):
        xn = _ffn_ln(h, bf(ffn1_w_gate[i]), bf(ffn1_w_up[i]), bf(ffn1_w_down[i]), row(ln1_g[i]), row(ln1_b[i]))
        wc, wqa, wqb, wuk, wuv, vones = _qkv_weights(w_in[i], mla_w_uq[i], mla_w_ukv[i])
        q, k, v = _qkv(xn, wc, row(mla_q_norm_g[i]), row(mla_kv_norm_g[i]), wqa, wqb, wuk, wuv, vones, tab)
        att = _attention(q, k, v)
        bblk, ar, ai, cblk = _s5_weights(s5_a_re[i], s5_a_im[i], s5_log_dt[i], s5_b_re[i], s5_b_im[i],
                                         s5_c_re[i], s5_c_im[i])
        h = _mix(xn, att, bf(w_in[i][:, _OFF_CONV:]), conv_w[i].astype(F32), row(conv_b[i]),
                 bf(conv_w_out[i]), bblk, ar, ai, cblk, row(s5_d[i]), bf(s5_w_glu[i]), row(s5_b_glu[i]),
                 bf(s5_w_out[i]), bf(mla_w_o[i]), bf(w_o[i]), row(ln2_g[i]), row(ln2_b[i]))
        h = _ffn_ln(h, bf(ffn2_w_gate[i]), bf(ffn2_w_up[i]), bf(ffn2_w_down[i]), row(ln3_g[i]), row(ln3_b[i]))
    out = h.reshape(LP, BATCH, D_MODEL)[PAD + N_META:]
    return jnp.transpose(out, (1, 0, 2))
```

```python
import functools
import math

import jax
import jax.numpy as jnp
from jax import lax
from jax.experimental import pallas as pl
from jax.experimental.pallas import tpu as pltpu

D_MODEL = 1024
BATCH = 8
SEQ = 2048
DEPTH = 2
N_META = 16
BLOCK_Q = 128
MIX_WIDTH = D_MODEL // 2
HEADS = 8
V_HEAD_DIM = 64
QK_NOPE_DIM = 64
QK_ROPE_DIM = 32
QK_HEAD_DIM = QK_NOPE_DIM + QK_ROPE_DIM
Q_LORA_RANK = 384
KV_LORA_RANK = 256
ROPE_BASE = 10000.0
S5_GROUP = 16
S5_GROUPS = 32
S5_STATE = 64
D_FF = 2816
ALPHA = (2.0 * DEPTH) ** 0.25
LN_EPS = 1e-5
RMS_EPS = 1e-6

HEAD_LANES = 128
QKV_WIDTH = HEADS * HEAD_LANES
PAD = BLOCK_Q - N_META
LB = SEQ + BLOCK_Q
ROWS = LB * BATCH
TOK_TILE = 512
ATT_TILE = 2 * BLOCK_Q
N_SEQ_TILES = SEQ // ATT_TILE
ATT_LEN = (N_SEQ_TILES + 1) * ATT_TILE
V_ROWS = V_HEAD_DIM + 16
LOG2E = math.log2(math.e)
S5_BLOCKS = 4
S5_BLOCK_STATES = (S5_GROUPS // S5_BLOCKS) * S5_STATE
S5_COLS = 2 * S5_GROUPS * S5_STATE
MIX_STEPS = 64
N_MIX_TILES = LB // MIX_STEPS

_OFF_CQ, _OFF_CKV, _OFF_KR, _OFF_CONV = 0, 384, 640, 672
_OFF_GATES = _OFF_CONV + 4 * MIX_WIDTH
D_IN = _OFF_GATES + 3 * D_MODEL

V7X_VMEM_BYTES = 64 * 1024 * 1024
VMEM_LIMIT = 56 * 1024 * 1024

F32 = jnp.float32
BF16 = jnp.bfloat16


def _dot(a, b):
    return jnp.dot(a, b, preferred_element_type=F32)


def _const_spec(shape):
    nd = len(shape)
    return pl.BlockSpec(shape, lambda *_: (0,) * nd, pipeline_mode=pl.Buffered(1))


def _layer_norm(y, g, b):
    mu = jnp.mean(y, axis=-1, keepdims=True)
    yc = y - mu
    var = jnp.mean(yc * yc, axis=-1, keepdims=True)
    return yc * lax.rsqrt(var + LN_EPS) * g + b


def _rms_norm(y, g):
    return y * lax.rsqrt(jnp.mean(y * y, axis=-1, keepdims=True) + RMS_EPS) * g


def _sigmoid(y):
    return 1.0 / (1.0 + jnp.exp(-y))


def _gelu_tanh(y):
    return 0.5 * y * (1.0 + jnp.tanh(math.sqrt(2.0 / math.pi) * (y + 0.044715 * (y * y * y))))


def _ffn_ln_kernel(x_ref, wg_ref, wu_ref, wd_ref, g_ref, b_ref, o_ref, *, ff_chunk):
    x = x_ref[...]
    xb = x.astype(BF16)
    acc = None
    for c in range(D_FF // ff_chunk):
        sl = slice(c * ff_chunk, (c + 1) * ff_chunk)
        gate = _dot(xb, wg_ref[:, sl])
        up = _dot(xb, wu_ref[:, sl])
        hmid = (gate * _sigmoid(gate) * up).astype(BF16)
        part = _dot(hmid, wd_ref[sl, :])
        acc = part if acc is None else acc + part
    o_ref[...] = _layer_norm(ALPHA * x + 0.5 * acc, g_ref[...], b_ref[...])


def _ffn_ln(x, wg, wu, wd, g, b, *, out_rows=LB, tm=TOK_TILE, ff_chunk=1408):
    weights = [_const_spec((D_MODEL, D_FF)), _const_spec((D_MODEL, D_FF)), _const_spec((D_FF, D_MODEL)),
               _const_spec((1, D_MODEL)), _const_spec((1, D_MODEL))]
    params = dict(name="ffn_ln")
    kern = functools.partial(_ffn_ln_kernel, ff_chunk=ff_chunk)
    if out_rows == LB:
        flat = pl.BlockSpec((tm, D_MODEL), lambda i: (i, 0))
        out = pl.pallas_call(
            kern, grid=(ROWS // tm,), in_specs=[flat] + weights, out_specs=flat,
            out_shape=jax.ShapeDtypeStruct((ROWS, D_MODEL), F32),
            compiler_params=pltpu.CompilerParams(
                dimension_semantics=("arbitrary",), vmem_limit_bytes=VMEM_LIMIT),
            **params)(x.reshape(ROWS, D_MODEL), wg, wu, wd, g, b)
        return out.reshape(BATCH, LB, D_MODEL)
    per_seq = pl.BlockSpec((None, tm, D_MODEL), lambda s, i: (s, i, 0))
    return pl.pallas_call(
        kern, grid=(BATCH, out_rows // tm), in_specs=[per_seq] + weights, out_specs=per_seq,
        out_shape=jax.ShapeDtypeStruct((BATCH, out_rows, D_MODEL), F32),
        compiler_params=pltpu.CompilerParams(
            dimension_semantics=("arbitrary", "arbitrary"), vmem_limit_bytes=VMEM_LIMIT),
        **params)(x, wg, wu, wd, g, b)


_WC_WIDTH = Q_LORA_RANK + KV_LORA_RANK + 2 * HEAD_LANES
_N_TOK_TILES = -(-LB // TOK_TILE)


def _qkv_kernel(x_ref, wc_ref, gq_ref, gkv_ref, wqa_ref, wqb_ref, wuk_ref, wuv_ref, vones_ref, tab_ref,
                q_ref, k_ref, v_ref):
    j = pl.program_id(1)
    row = lax.broadcasted_iota(jnp.int32, (TOK_TILE, 1), 0)
    keep = (j < _N_TOK_TILES - 1) | (row < LB - (_N_TOK_TILES - 1) * TOK_TILE)
    xb = jnp.where(keep, x_ref[...], 0.0).astype(BF16)
    c = _dot(xb, wc_ref[...])
    c_q = c[:, :Q_LORA_RANK]
    c_kv = c[:, Q_LORA_RANK:Q_LORA_RANK + KV_LORA_RANK]
    k_r = c[:, _WC_WIDTH - 2 * HEAD_LANES:_WC_WIDTH - HEAD_LANES]
    k_r_rot = c[:, _WC_WIDTH - HEAD_LANES:]
    qn = _rms_norm(c_q, gq_ref[...]).astype(BF16)
    kvn = _rms_norm(c_kv, gkv_ref[...]).astype(BF16)
    cos_q = tab_ref[:, 0 * HEAD_LANES:1 * HEAD_LANES]
    sin_q = tab_ref[:, 1 * HEAD_LANES:2 * HEAD_LANES]
    cos_k = tab_ref[:, 2 * HEAD_LANES:3 * HEAD_LANES]
    sin_k = tab_ref[:, 3 * HEAD_LANES:4 * HEAD_LANES]
    k_rope = k_r * cos_k + k_r_rot * sin_k
    q_a = _dot(qn, wqa_ref[...])
    q_b = _dot(qn, wqb_ref[...])
    k_nope = _dot(kvn, wuk_ref[...])
    for h in range(HEADS):
        sl = slice(h * HEAD_LANES, (h + 1) * HEAD_LANES)
        q_ref[:, sl] = (q_a[:, sl] * cos_q + q_b[:, sl] * sin_q).astype(BF16)
        k_ref[:, sl] = (k_nope[:, sl] + k_rope).astype(BF16)
    v = _dot(kvn, wuv_ref[...]) + vones_ref[...]
    for t in range(TOK_TILE // ATT_TILE):
        v_ref[t] = v[t * ATT_TILE:(t + 1) * ATT_TILE, :].T.astype(BF16)


def _qkv(xn, wc, gq, gkv, wqa, wqb, wuk, wuv, vones, tab):
    seq = lambda w: pl.BlockSpec((None, TOK_TILE, w), lambda s, j: (s, j, 0))
    seq_shape = jax.ShapeDtypeStruct((BATCH, ATT_LEN, QKV_WIDTH), BF16)
    v_tiles = TOK_TILE // ATT_TILE
    return pl.pallas_call(
        _qkv_kernel,
        grid=(BATCH, _N_TOK_TILES),
        in_specs=[
            seq(D_MODEL),
            _const_spec((D_MODEL, _WC_WIDTH)),
            _const_spec((1, Q_LORA_RANK)),
            _const_spec((1, KV_LORA_RANK)),
            _const_spec((Q_LORA_RANK, QKV_WIDTH)),
            _const_spec((Q_LORA_RANK, QKV_WIDTH)),
            _const_spec((KV_LORA_RANK, QKV_WIDTH)),
            _const_spec((KV_LORA_RANK, HEADS * V_ROWS)),
            _const_spec((1, HEADS * V_ROWS)),
            pl.BlockSpec((TOK_TILE, 4 * HEAD_LANES), lambda s, j: (j, 0)),
        ],
        out_specs=[seq(QKV_WIDTH), seq(QKV_WIDTH),
                   pl.BlockSpec((None, v_tiles, HEADS * V_ROWS, ATT_TILE), lambda s, j: (s, j, 0, 0))],
        out_shape=[seq_shape, seq_shape,
                   jax.ShapeDtypeStruct((BATCH, N_SEQ_TILES + 1, HEADS * V_ROWS, ATT_TILE), BF16)],
        compiler_params=pltpu.CompilerParams(
            dimension_semantics=("arbitrary", "arbitrary"), vmem_limit_bytes=VMEM_LIMIT),
        name="qkv_proj",
    )(xn, wc, gq, gkv, wqa, wqb, wuk, wuv, vones, tab)


def _attend(q_ref, k_ref, v_ref, m_scr, acc_scr, c, mask, first):
    scores = []
    k0 = c * ATT_TILE if isinstance(c, int) else pl.multiple_of(c * ATT_TILE, ATT_TILE)
    for h in range(HEADS):
        sl = slice(h * HEAD_LANES, (h + 1) * HEAD_LANES)
        scores.append(lax.dot_general(
            k_ref[pl.ds(k0, ATT_TILE), sl], q_ref[:, sl], (((1,), (1,)), ((), ())),
            preferred_element_type=F32))
    for h in range(HEADS):
        s_t = scores[h]
        if mask is not None:
            s_t = jnp.where(mask, s_t, -1e30)
        m_cur = jnp.max(s_t, axis=0, keepdims=True)
        if first:
            m_new = m_cur
        else:
            m_old = m_scr[h]
            m_new = jnp.maximum(m_old, m_cur)
        p_t = jnp.exp2(s_t - m_new).astype(BF16)
        pv = _dot(v_ref[c, h * V_ROWS:(h + 1) * V_ROWS, :], p_t)
        if first:
            acc_scr[h] = pv
        else:
            acc_scr[h] = jnp.exp2(m_old - m_new) * acc_scr[h] + pv
        m_scr[h] = m_new


def _attn_kernel(q_ref, k_ref, v_ref, o_ref, m_scr, acc_scr):
    i = pl.program_id(1)
    is_seq = i < N_SEQ_TILES
    q_off = lax.broadcasted_iota(jnp.int32, (ATT_TILE, ATT_TILE), 1)
    k_off = lax.broadcasted_iota(jnp.int32, (ATT_TILE, ATT_TILE), 0)
    causal = k_off <= q_off
    meta_keys = (k_off >= PAD) & (k_off < BLOCK_Q) & (causal | is_seq)

    refs = (q_ref, k_ref, v_ref, m_scr, acc_scr)
    _attend(*refs, N_SEQ_TILES, meta_keys, True)

    def earlier(c, carry):
        _attend(*refs, c, None, False)
        return carry

    lax.fori_loop(0, jnp.where(is_seq, i, 0), earlier, 0)

    @pl.when(is_seq)
    def _():
        _attend(*refs, i, causal, False)

    o_t = jnp.concatenate(
        [acc_scr[h, 0:V_HEAD_DIM, :] / acc_scr[h, V_HEAD_DIM:V_HEAD_DIM + 1, :] for h in range(HEADS)],
        axis=0)
    o_ref[...] = o_t.T.astype(BF16)


def _attention(q, k, v_t):
    return pl.pallas_call(
        _attn_kernel,
        grid=(BATCH, N_SEQ_TILES + 1),
        in_specs=[
            pl.BlockSpec((None, ATT_TILE, QKV_WIDTH), lambda s, i: (s, i, 0)),
            pl.BlockSpec((None, ATT_LEN, QKV_WIDTH), lambda s, i: (s, 0, 0)),
            pl.BlockSpec((None, N_SEQ_TILES + 1, HEADS * V_ROWS, ATT_TILE), lambda s, i: (s, 0, 0, 0)),
        ],
        out_specs=pl.BlockSpec((None, ATT_TILE, MIX_WIDTH), lambda s, i: (s, i, 0)),
        out_shape=jax.ShapeDtypeStruct((BATCH, ATT_LEN, MIX_WIDTH), BF16),
        scratch_shapes=[pltpu.VMEM((HEADS, 1, ATT_TILE), F32),
                        pltpu.VMEM((HEADS, V_ROWS, ATT_TILE), F32)],
        compiler_params=pltpu.CompilerParams(
            dimension_semantics=("arbitrary", "arbitrary"), vmem_limit_bytes=VMEM_LIMIT),
        name="mla_attention",
    )(q, k, v_t)


_HALO = 2 * BATCH
_D_SLABS = D_MODEL // HEAD_LANES


def _to_time_major(src_ref, slab_ref, width):
    n = width // HEAD_LANES
    for b in range(BATCH):
        for j in range(n):
            slab_ref[j, pl.ds(b, MIX_STEPS, stride=BATCH), :] = (
                src_ref[b, :, j * HEAD_LANES:(j + 1) * HEAD_LANES].astype(F32))
    return jnp.concatenate([slab_ref[j] for j in range(n)], axis=1)


def _mix_kernel(x_ref, att_ref, win_ref, cw_ref, cb_ref, cwo_ref, bblk_ref, ar_ref, ai_ref,
                cblk_ref, d_ref, wglu_ref, bglu_ref, swo_ref, mwo_ref, wo_ref, g_ref, b_ref,
                out_ref, bu_scr, st_scr, cbuf_scr, slab_ref):
    pid = pl.program_id(0)
    tm = MIX_STEPS * BATCH
    W = MIX_WIDTH

    @pl.when(pid == 0)
    def _():
        st_scr[...] = jnp.zeros_like(st_scr)
        cbuf_scr[0:_HALO, :] = jnp.zeros((_HALO, W), F32)

    x = _to_time_major(x_ref, slab_ref, D_MODEL)
    xb = x.astype(BF16)
    att = _to_time_major(att_ref, slab_ref, W).astype(BF16)
    row = pid * tm + lax.broadcasted_iota(jnp.int32, (tm, 1), 0)
    valid = row >= PAD * BATCH

    pc = _dot(xb, win_ref[:, 0:3 * W])
    u = jnp.where(valid, pc[:, 2 * W:3 * W] * pc[:, 0:W], 0.0)
    cbuf_scr[_HALO:_HALO + tm, :] = u
    y = (cb_ref[...] + cw_ref[0:1, :] * cbuf_scr[0:tm, :]
         + cw_ref[1:2, :] * cbuf_scr[BATCH:BATCH + tm, :] + cw_ref[2:3, :] * u)
    cbuf_scr[0:_HALO, :] = u[tm - _HALO:, :]
    y_b = _dot((pc[:, W:2 * W] * y).astype(BF16), cwo_ref[...])

    us = jnp.where(valid, _dot(xb, win_ref[:, 3 * W:4 * W]), 0.0)
    usb = us.astype(BF16)
    nb = S5_BLOCK_STATES
    for blk in range(S5_BLOCKS):
        bu_scr[:, 2 * nb * blk:2 * nb * (blk + 1)] = _dot(
            usb[:, blk * 128:(blk + 1) * 128], bblk_ref[blk])
    for blk in range(S5_BLOCKS):
        c_re = 2 * nb * blk
        c_im = c_re + nb
        a_re = ar_ref[blk]
        a_im = ai_ref[blk]

        def step(t, carry, c_re=c_re, c_im=c_im, a_re=a_re, a_im=a_im):
            s_re, s_im = carry
            r0 = pl.multiple_of(t * BATCH, BATCH)
            n_re = a_re * s_re - a_im * s_im + bu_scr[pl.ds(r0, BATCH), c_re:c_re + nb]
            n_im = a_re * s_im + a_im * s_re + bu_scr[pl.ds(r0, BATCH), c_im:c_im + nb]
            bu_scr[pl.ds(r0, BATCH), c_re:c_re + nb] = n_re
            bu_scr[pl.ds(r0, BATCH), c_im:c_im + nb] = n_im
            return n_re, n_im

        s_re, s_im = lax.fori_loop(
            0, MIX_STEPS, step, (st_scr[:, c_re:c_re + nb], st_scr[:, c_im:c_im + nb]), unroll=True)
        st_scr[:, c_re:c_re + nb] = s_re
        st_scr[:, c_im:c_im + nb] = s_im
    y = jnp.concatenate(
        [_dot(bu_scr[:, 2 * nb * blk:2 * nb * (blk + 1)].astype(BF16), cblk_ref[blk])
         for blk in range(S5_BLOCKS)], axis=1)
    y = _gelu_tanh(y + d_ref[...] * us)
    y = y * _sigmoid(_dot(y.astype(BF16), wglu_ref[...]) + bglu_ref[...])
    y_c = _dot(y.astype(BF16), swo_ref[...])

    y_a = _dot(att, mwo_ref[...])

    g0 = 4 * W
    mixed = _sigmoid(_dot(xb, win_ref[:, g0:g0 + D_MODEL])) * y_a
    mixed += _sigmoid(_dot(xb, win_ref[:, g0 + D_MODEL:g0 + 2 * D_MODEL])) * y_b
    mixed += _sigmoid(_dot(xb, win_ref[:, g0 + 2 * D_MODEL:g0 + 3 * D_MODEL])) * y_c
    z = _layer_norm(ALPHA * x + _dot(mixed.astype(BF16), wo_ref[...]), g_ref[...], b_ref[...])

    for j in range(_D_SLABS):
        slab_ref[j] = z[:, j * HEAD_LANES:(j + 1) * HEAD_LANES]
    for b in range(BATCH):
        for j in range(_D_SLABS):
            out_ref[b, :, j * HEAD_LANES:(j + 1) * HEAD_LANES] = slab_ref[
                j, pl.ds(b, MIX_STEPS, stride=BATCH), :]


def _mix(xn, att, win, cw, cb, cwo, bblk, ar, ai, cblk, d, wglu, bglu, swo, mwo, wo, g, b):
    consts = (win, cw, cb, cwo, bblk, ar, ai, cblk, d, wglu, bglu, swo, mwo, wo, g, b)
    tm = MIX_STEPS * BATCH
    meta_tiles = BLOCK_Q // MIX_STEPS
    tile = lambda w: pl.BlockSpec((BATCH, MIX_STEPS, w),
                                  lambda i: (0, (i + N_MIX_TILES - meta_tiles) % N_MIX_TILES, 0))
    return pl.pallas_call(
        _mix_kernel,
        grid=(N_MIX_TILES,),
        in_specs=[tile(D_MODEL), tile(MIX_WIDTH)] + [_const_spec(c.shape) for c in consts],
        out_specs=tile(D_MODEL),
        out_shape=jax.ShapeDtypeStruct((BATCH, LB, D_MODEL), F32),
        scratch_shapes=[
            pltpu.VMEM((tm, S5_COLS), F32),
            pltpu.VMEM((BATCH, S5_COLS), F32),
            pltpu.VMEM((tm + _HALO, MIX_WIDTH), F32),
            pltpu.VMEM((_D_SLABS, tm, HEAD_LANES), F32),
        ],
        compiler_params=pltpu.CompilerParams(
            dimension_semantics=("arbitrary",), vmem_limit_bytes=VMEM_LIMIT),
        name="mixers_merge",
    )(xn, att, *consts)


def _rope_tables():
    n = _N_TOK_TILES * TOK_TILE
    r = jnp.arange(n)
    pos = jnp.where(r < SEQ, r + N_META, r - SEQ - PAD).astype(F32)
    inv_freq = ROPE_BASE ** (-jnp.arange(0, QK_ROPE_DIM, 2, dtype=F32) / QK_ROPE_DIM)
    ang = pos[:, None] * inv_freq[None, :]
    cos2 = jnp.tile(jnp.cos(ang), (1, 2))
    sin2 = jnp.tile(jnp.sin(ang), (1, 2))
    zn = jnp.zeros((n, QK_NOPE_DIM), F32)
    zt = jnp.zeros((n, HEAD_LANES - QK_HEAD_DIM), F32)
    scale = QK_HEAD_DIM ** -0.5 * LOG2E
    cos_q = scale * jnp.concatenate([jnp.ones_like(zn), cos2, zt], axis=1)
    sin_q = scale * jnp.concatenate([zn, sin2, zt], axis=1)
    cos_k = jnp.concatenate([zn, cos2, zt], axis=1)
    sin_k = jnp.concatenate([zn, sin2, zt], axis=1)
    return jnp.concatenate([cos_q, sin_q, cos_k, sin_k], axis=1)


def _rot_half_cols(w):
    half = QK_ROPE_DIM // 2
    return jnp.concatenate([-w[..., half:], w[..., :half]], axis=-1)


def _qkv_weights(w_in, w_uq, w_ukv):
    zpad = lambda n: jnp.zeros((D_MODEL, n), F32)
    kr = w_in[:, _OFF_KR:_OFF_CONV]
    tail = HEAD_LANES - QK_HEAD_DIM
    wc = jnp.concatenate([
        w_in[:, :_OFF_KR],
        zpad(QK_NOPE_DIM), kr, zpad(tail),
        zpad(QK_NOPE_DIM), _rot_half_cols(kr), zpad(tail)], axis=1)
    uq = w_uq.reshape(Q_LORA_RANK, HEADS, QK_HEAD_DIM)
    zq = lambda n: jnp.zeros((Q_LORA_RANK, HEADS, n), F32)
    wqa = jnp.concatenate([uq, zq(tail)], axis=-1)
    wqb = jnp.concatenate([zq(QK_NOPE_DIM), _rot_half_cols(uq[..., QK_NOPE_DIM:]), zq(tail)], axis=-1)
    ukv = w_ukv.reshape(KV_LORA_RANK, HEADS, QK_NOPE_DIM + V_HEAD_DIM)
    zk = jnp.zeros((KV_LORA_RANK, HEADS, HEAD_LANES - QK_NOPE_DIM), F32)
    wuk = jnp.concatenate([ukv[..., :QK_NOPE_DIM], zk], axis=-1)
    wuv = jnp.concatenate(
        [ukv[..., QK_NOPE_DIM:], jnp.zeros((KV_LORA_RANK, HEADS, V_ROWS - V_HEAD_DIM), F32)], axis=-1)
    vones = jnp.zeros((HEADS, V_ROWS), F32).at[:, V_HEAD_DIM].set(1.0).reshape(1, HEADS * V_ROWS)
    flat = lambda w: w.reshape(w.shape[0], -1).astype(BF16)
    return wc.astype(BF16), flat(wqa), flat(wqb), flat(wuk), flat(wuv), vones


def _s5_weights(a_re, a_im, log_dt, b_re, b_im, c_re, c_im):
    dt = jnp.exp(log_dt)[:, None]
    mag = jnp.exp(dt * a_re)
    ab_re, ab_im = mag * jnp.cos(dt * a_im), mag * jnp.sin(dt * a_im)
    den = a_re * a_re + a_im * a_im
    nr, ni = ab_re - 1.0, ab_im
    coef_re = (nr * a_re + ni * a_im) / den
    coef_im = (ni * a_re - nr * a_im) / den
    bb_re = coef_re[..., None] * b_re - coef_im[..., None] * b_im
    bb_im = coef_re[..., None] * b_im + coef_im[..., None] * b_re
    gpb = S5_GROUPS // S5_BLOCKS
    eye = jnp.eye(gpb, dtype=F32)

    def in_blocks(bb):
        t = bb.transpose(0, 2, 1).reshape(S5_BLOCKS, gpb, S5_GROUP, S5_STATE)
        return jnp.einsum('bghn,gk->bghkn', t, eye).reshape(S5_BLOCKS, gpb * S5_GROUP, gpb * S5_STATE)

    def out_blocks(cc):
        t = cc.reshape(S5_BLOCKS, gpb, S5_GROUP, S5_STATE)
        return jnp.einsum('bghn,gk->bgnkh', t, eye).reshape(S5_BLOCKS, gpb * S5_STATE, gpb * S5_GROUP)

    bblk = jnp.concatenate([in_blocks(bb_re), in_blocks(bb_im)], axis=2).astype(BF16)
    cblk = jnp.concatenate([out_blocks(c_re), out_blocks(-c_im)], axis=1).astype(BF16)
    bcast = lambda a: jnp.broadcast_to(
        a.reshape(S5_BLOCKS, 1, S5_BLOCK_STATES), (S5_BLOCKS, BATCH, S5_BLOCK_STATES))
    return bblk, bcast(ab_re), bcast(ab_im), cblk


def kernel(x, meta, ffn1_w_gate, ffn1_w_up, ffn1_w_down, ln1_g, ln1_b, w_in, mla_q_norm_g, mla_w_uq, mla_kv_norm_g, mla_w_ukv, mla_w_o, conv_w, conv_b, conv_w_out, s5_a_re, s5_a_im, s5_log_dt, s5_b_re, s5_b_im, s5_c_re, s5_c_im, s5_d, s5_w_glu, s5_b_glu, s5_w_out, w_o, ln2_g, ln2_b, ffn2_w_gate, ffn2_w_up, ffn2_w_down, ln3_g, ln3_b):
    row = lambda v: v.reshape(1, -1).astype(F32)
    bf = lambda w: w.astype(BF16)
    h = jnp.concatenate([
        x.astype(F32),
        jnp.zeros((BATCH, PAD, D_MODEL), F32),
        jnp.broadcast_to(meta[None].astype(F32), (BATCH, N_META, D_MODEL))], axis=1)
    tab = _rope_tables()
    for i in range(DEPTH):
        xn = _ffn_ln(h, bf(ffn1_w_gate[i]), bf(ffn1_w_up[i]), bf(ffn1_w_down[i]), row(ln1_g[i]), row(ln1_b[i]))
        wc, wqa, wqb, wuk, wuv, vones = _qkv_weights(w_in[i], mla_w_uq[i], mla_w_ukv[i])
        q, k, v = _qkv(xn, wc, row(mla_q_norm_g[i]), row(mla_kv_norm_g[i]), wqa, wqb, wuk, wuv, vones, tab)
        att = _attention(q, k, v)
        bblk, ar, ai, cblk = _s5_weights(s5_a_re[i], s5_a_im[i], s5_log_dt[i], s5_b_re[i], s5_b_im[i],
                                         s5_c_re[i], s5_c_im[i])
        h = _mix(xn, att, bf(w_in[i][:, _OFF_CONV:]), conv_w[i].astype(F32), row(conv_b[i]),
                 bf(conv_w_out[i]), bblk, ar, ai, cblk, row(s5_d[i]), bf(s5_w_glu[i]), row(s5_b_glu[i]),
                 bf(s5_w_out[i]), bf(mla_w_o[i]), bf(w_o[i]), row(ln2_g[i]), row(ln2_b[i]))
        last = i == DEPTH - 1
        h = _ffn_ln(h, bf(ffn2_w_gate[i]), bf(ffn2_w_up[i]), bf(ffn2_w_down[i]), row(ln3_g[i]), row(ln3_b[i]),
                    out_rows=SEQ if last else LB)
    return h
```

```python
import functools
import math

import jax
import jax.numpy as jnp
from jax import lax
from jax.experimental import pallas as pl
from jax.experimental.pallas import tpu as pltpu

D_MODEL = 1024
BATCH = 8
SEQ = 2048
DEPTH = 2
N_META = 16
BLOCK_Q = 128
MIX_WIDTH = D_MODEL // 2
HEADS = 8
V_HEAD_DIM = 64
QK_NOPE_DIM = 64
QK_ROPE_DIM = 32
QK_HEAD_DIM = QK_NOPE_DIM + QK_ROPE_DIM
Q_LORA_RANK = 384
KV_LORA_RANK = 256
ROPE_BASE = 10000.0
S5_GROUP = 16
S5_GROUPS = 32
S5_STATE = 64
D_FF = 2816
ALPHA = (2.0 * DEPTH) ** 0.25
LN_EPS = 1e-5
RMS_EPS = 1e-6

HEAD_LANES = 128
QKV_WIDTH = HEADS * HEAD_LANES
PAD = BLOCK_Q - N_META
LB = SEQ + BLOCK_Q
ROWS = LB * BATCH
TOK_TILE = 512
ATT_TILE = 2 * BLOCK_Q
N_SEQ_TILES = SEQ // ATT_TILE
ATT_LEN = (N_SEQ_TILES + 1) * ATT_TILE
V_ROWS = V_HEAD_DIM + 16
LOG2E = math.log2(math.e)
S5_BLOCKS = 4
S5_BLOCK_STATES = (S5_GROUPS // S5_BLOCKS) * S5_STATE
S5_COLS = 2 * S5_GROUPS * S5_STATE
MIX_STEPS = 64
N_MIX_TILES = LB // MIX_STEPS

_OFF_CQ, _OFF_CKV, _OFF_KR, _OFF_CONV = 0, 384, 640, 672
_OFF_GATES = _OFF_CONV + 4 * MIX_WIDTH
D_IN = _OFF_GATES + 3 * D_MODEL

V7X_VMEM_BYTES = 64 * 1024 * 1024
VMEM_LIMIT = 56 * 1024 * 1024

F32 = jnp.float32
BF16 = jnp.bfloat16


def _dot(a, b):
    return jnp.dot(a, b, preferred_element_type=F32)


def _const_spec(shape):
    nd = len(shape)
    return pl.BlockSpec(shape, lambda *_: (0,) * nd, pipeline_mode=pl.Buffered(1))


def _layer_norm(y, g, b):
    mu = jnp.mean(y, axis=-1, keepdims=True)
    yc = y - mu
    var = jnp.mean(yc * yc, axis=-1, keepdims=True)
    return yc * lax.rsqrt(var + LN_EPS) * g + b


def _rms_norm(y, g):
    return y * lax.rsqrt(jnp.mean(y * y, axis=-1, keepdims=True) + RMS_EPS) * g


def _sigmoid(y):
    return 1.0 / (1.0 + jnp.exp(-y))


def _gelu_tanh(y):
    return 0.5 * y * (1.0 + jnp.tanh(math.sqrt(2.0 / math.pi) * (y + 0.044715 * (y * y * y))))


def _ffn_ln_kernel(x_ref, wg_ref, wu_ref, wd_ref, g_ref, b_ref, o_ref, *, ff_chunk):
    x = x_ref[...]
    xb = x.astype(BF16)
    acc = None
    for c in range(D_FF // ff_chunk):
        sl = slice(c * ff_chunk, (c + 1) * ff_chunk)
        gate = _dot(xb, wg_ref[:, sl])
        up = _dot(xb, wu_ref[:, sl])
        hmid = (gate * _sigmoid(gate) * up).astype(BF16)
        part = _dot(hmid, wd_ref[sl, :])
        acc = part if acc is None else acc + part
    o_ref[...] = _layer_norm(ALPHA * x + 0.5 * acc, g_ref[...], b_ref[...])


def _ffn_ln(x, wg, wu, wd, g, b, *, out_rows=LB, tm=TOK_TILE, ff_chunk=1408):
    weights = [_const_spec((D_MODEL, D_FF)), _const_spec((D_MODEL, D_FF)), _const_spec((D_FF, D_MODEL)),
               _const_spec((1, D_MODEL)), _const_spec((1, D_MODEL))]
    params = dict(name="ffn_ln")
    kern = functools.partial(_ffn_ln_kernel, ff_chunk=ff_chunk)
    if out_rows == LB:
        flat = pl.BlockSpec((tm, D_MODEL), lambda i: (i, 0))
        out = pl.pallas_call(
            kern, grid=(ROWS // tm,), in_specs=[flat] + weights, out_specs=flat,
            out_shape=jax.ShapeDtypeStruct((ROWS, D_MODEL), F32),
            compiler_params=pltpu.CompilerParams(
                dimension_semantics=("arbitrary",), vmem_limit_bytes=VMEM_LIMIT),
            **params)(x.reshape(ROWS, D_MODEL), wg, wu, wd, g, b)
        return out.reshape(BATCH, LB, D_MODEL)
    per_seq = pl.BlockSpec((None, tm, D_MODEL), lambda s, i: (s, i, 0))
    return pl.pallas_call(
        kern, grid=(BATCH, out_rows // tm), in_specs=[per_seq] + weights, out_specs=per_seq,
        out_shape=jax.ShapeDtypeStruct((BATCH, out_rows, D_MODEL), F32),
        compiler_params=pltpu.CompilerParams(
            dimension_semantics=("arbitrary", "arbitrary"), vmem_limit_bytes=VMEM_LIMIT),
        **params)(x, wg, wu, wd, g, b)


_WC_WIDTH = Q_LORA_RANK + KV_LORA_RANK + 2 * HEAD_LANES
_N_TOK_TILES = -(-LB // TOK_TILE)


def _qkv_kernel(x_ref, wc_ref, gq_ref, gkv_ref, wqa_ref, wqb_ref, wuk_ref, wuv_ref, vones_ref, tab_ref,
                q_ref, k_ref, v_ref):
    j = pl.program_id(1)
    row = lax.broadcasted_iota(jnp.int32, (TOK_TILE, 1), 0)
    keep = (j < _N_TOK_TILES - 1) | (row < LB - (_N_TOK_TILES - 1) * TOK_TILE)
    xb = jnp.where(keep, x_ref[...], 0.0).astype(BF16)
    c = _dot(xb, wc_ref[...])
    c_q = c[:, :Q_LORA_RANK]
    c_kv = c[:, Q_LORA_RANK:Q_LORA_RANK + KV_LORA_RANK]
    k_r = c[:, _WC_WIDTH - 2 * HEAD_LANES:_WC_WIDTH - HEAD_LANES]
    k_r_rot = c[:, _WC_WIDTH - HEAD_LANES:]
    qn = _rms_norm(c_q, gq_ref[...]).astype(BF16)
    kvn = _rms_norm(c_kv, gkv_ref[...]).astype(BF16)
    cos_q = tab_ref[:, 0 * HEAD_LANES:1 * HEAD_LANES]
    sin_q = tab_ref[:, 1 * HEAD_LANES:2 * HEAD_LANES]
    cos_k = tab_ref[:, 2 * HEAD_LANES:3 * HEAD_LANES]
    sin_k = tab_ref[:, 3 * HEAD_LANES:4 * HEAD_LANES]
    k_rope = k_r * cos_k + k_r_rot * sin_k
    q_a = _dot(qn, wqa_ref[...])
    q_b = _dot(qn, wqb_ref[...])
    k_nope = _dot(kvn, wuk_ref[...])
    for h in range(HEADS):
        sl = slice(h * HEAD_LANES, (h + 1) * HEAD_LANES)
        q_ref[:, sl] = (q_a[:, sl] * cos_q + q_b[:, sl] * sin_q).astype(BF16)
        k_ref[:, sl] = (k_nope[:, sl] + k_rope).astype(BF16)
    v = _dot(kvn, wuv_ref[...]) + vones_ref[...]
    for t in range(TOK_TILE // ATT_TILE):
        v_ref[t] = v[t * ATT_TILE:(t + 1) * ATT_TILE, :].T.astype(BF16)


def _qkv(xn, wc, gq, gkv, wqa, wqb, wuk, wuv, vones, tab):
    seq = lambda w: pl.BlockSpec((None, TOK_TILE, w), lambda s, j: (s, j, 0))
    seq_shape = jax.ShapeDtypeStruct((BATCH, ATT_LEN, QKV_WIDTH), BF16)
    v_tiles = TOK_TILE // ATT_TILE
    return pl.pallas_call(
        _qkv_kernel,
        grid=(BATCH, _N_TOK_TILES),
        in_specs=[
            seq(D_MODEL),
            _const_spec((D_MODEL, _WC_WIDTH)),
            _const_spec((1, Q_LORA_RANK)),
            _const_spec((1, KV_LORA_RANK)),
            _const_spec((Q_LORA_RANK, QKV_WIDTH)),
            _const_spec((Q_LORA_RANK, QKV_WIDTH)),
            _const_spec((KV_LORA_RANK, QKV_WIDTH)),
            _const_spec((KV_LORA_RANK, HEADS * V_ROWS)),
            _const_spec((1, HEADS * V_ROWS)),
            pl.BlockSpec((TOK_TILE, 4 * HEAD_LANES), lambda s, j: (j, 0)),
        ],
        out_specs=[seq(QKV_WIDTH), seq(QKV_WIDTH),
                   pl.BlockSpec((None, v_tiles, HEADS * V_ROWS, ATT_TILE), lambda s, j: (s, j, 0, 0))],
        out_shape=[seq_shape, seq_shape,
                   jax.ShapeDtypeStruct((BATCH, N_SEQ_TILES + 1, HEADS * V_ROWS, ATT_TILE), BF16)],
        compiler_params=pltpu.CompilerParams(
            dimension_semantics=("arbitrary", "arbitrary"), vmem_limit_bytes=VMEM_LIMIT),
        name="qkv_proj",
    )(xn, wc, gq, gkv, wqa, wqb, wuk, wuv, vones, tab)


def _scores(q_ref, k_ref, s_scr, smax_scr, slot, c, mask):
    k0 = c * ATT_TILE if isinstance(c, int) else pl.multiple_of(c * ATT_TILE, ATT_TILE)
    for h in range(HEADS):
        sl = slice(h * HEAD_LANES, (h + 1) * HEAD_LANES)
        s_t = lax.dot_general(
            k_ref[pl.ds(k0, ATT_TILE), sl], q_ref[:, sl], (((1,), (1,)), ((), ())),
            preferred_element_type=F32)
        if mask is not None:
            s_t = jnp.where(mask, s_t, -1e30)
        s_scr[slot, h] = s_t
        smax_scr[slot, h] = jnp.max(s_t, axis=0, keepdims=True)


def _attend(v_ref, s_scr, smax_scr, m_scr, acc_scr, slot, c, first):
    for h in range(HEADS):
        s_t = s_scr[slot, h]
        m_cur = smax_scr[slot, h]
        if first:
            m_new = m_cur
        else:
            m_old = m_scr[h]
            m_new = jnp.maximum(m_old, m_cur)
        p_t = jnp.exp2(s_t - m_new).astype(BF16)
        pv = _dot(v_ref[c, h * V_ROWS:(h + 1) * V_ROWS, :], p_t)
        if first:
            acc_scr[h] = pv
        else:
            acc_scr[h] = jnp.exp2(m_old - m_new) * acc_scr[h] + pv
        m_scr[h] = m_new


def _attn_kernel(q_ref, k_ref, v_ref, o_ref, s_scr, smax_scr, m_scr, acc_scr):
    i = pl.program_id(1)
    is_seq = i < N_SEQ_TILES
    q_off = lax.broadcasted_iota(jnp.int32, (ATT_TILE, ATT_TILE), 1)
    k_off = lax.broadcasted_iota(jnp.int32, (ATT_TILE, ATT_TILE), 0)
    causal = k_off <= q_off
    meta_keys = (k_off >= PAD) & (k_off < BLOCK_Q) & (causal | is_seq)

    state = (v_ref, s_scr, smax_scr, m_scr, acc_scr)
    scores = functools.partial(_scores, q_ref, k_ref, s_scr, smax_scr)
    scores(0, N_SEQ_TILES, meta_keys)
    scores(1, 0, causal | (i > 0))
    _attend(*state, 0, N_SEQ_TILES, True)

    def two_tiles(pp, carry):
        p = 1 + 2 * pp
        scores(0, p, None)
        _attend(*state, 1, p - 1, False)
        scores(1, p + 1, causal | (p + 1 < i))
        _attend(*state, 0, p, False)
        return carry

    lax.fori_loop(0, jnp.where(is_seq, i // 2, 0), two_tiles, 0)

    @pl.when(is_seq & (i % 2 == 1))
    def _():
        scores(0, i, causal)
        _attend(*state, 1, i - 1, False)
        _attend(*state, 0, i, False)

    @pl.when(is_seq & (i % 2 == 0))
    def _():
        _attend(*state, 1, i, False)

    o_t = jnp.concatenate(
        [acc_scr[h, 0:V_HEAD_DIM, :] / acc_scr[h, V_HEAD_DIM:V_HEAD_DIM + 1, :] for h in range(HEADS)],
        axis=0)
    o_ref[...] = o_t.T.astype(BF16)


def _attention(q, k, v_t):
    return pl.pallas_call(
        _attn_kernel,
        grid=(BATCH, N_SEQ_TILES + 1),
        in_specs=[
            pl.BlockSpec((None, ATT_TILE, QKV_WIDTH), lambda s, i: (s, i, 0)),
            pl.BlockSpec((None, ATT_LEN, QKV_WIDTH), lambda s, i: (s, 0, 0)),
            pl.BlockSpec((None, N_SEQ_TILES + 1, HEADS * V_ROWS, ATT_TILE), lambda s, i: (s, 0, 0, 0)),
        ],
        out_specs=pl.BlockSpec((None, ATT_TILE, MIX_WIDTH), lambda s, i: (s, i, 0)),
        out_shape=jax.ShapeDtypeStruct((BATCH, ATT_LEN, MIX_WIDTH), BF16),
        scratch_shapes=[pltpu.VMEM((2, HEADS, ATT_TILE, ATT_TILE), F32),
                        pltpu.VMEM((2, HEADS, 1, ATT_TILE), F32),
                        pltpu.VMEM((HEADS, 1, ATT_TILE), F32),
                        pltpu.VMEM((HEADS, V_ROWS, ATT_TILE), F32)],
        compiler_params=pltpu.CompilerParams(
            dimension_semantics=("arbitrary", "arbitrary"), vmem_limit_bytes=VMEM_LIMIT),
        name="mla_attention",
    )(q, k, v_t)


_HALO = 2 * BATCH
_D_SLABS = D_MODEL // HEAD_LANES


def _to_time_major(src_ref, slab_ref, width):
    n = width // HEAD_LANES
    for b in range(BATCH):
        for j in range(n):
            slab_ref[j, pl.ds(b, MIX_STEPS, stride=BATCH), :] = (
                src_ref[b, :, j * HEAD_LANES:(j + 1) * HEAD_LANES].astype(F32))
    return jnp.concatenate([slab_ref[j] for j in range(n)], axis=1)


def _mix_kernel(x_ref, att_ref, win_ref, cw_ref, cb_ref, cwo_ref, bblk_ref, ar_ref, ai_ref,
                cblk_ref, d_ref, wglu_ref, bglu_ref, swo_ref, mwo_ref, wo_ref, g_ref, b_ref,
                out_ref, bu_scr, st_scr, cbuf_scr, slab_ref):
    pid = pl.program_id(0)
    tm = MIX_STEPS * BATCH
    W = MIX_WIDTH

    @pl.when(pid == 0)
    def _():
        st_scr[...] = jnp.zeros_like(st_scr)
        cbuf_scr[0:_HALO, :] = jnp.zeros((_HALO, W), F32)

    x = _to_time_major(x_ref, slab_ref, D_MODEL)
    xb = x.astype(BF16)
    att = _to_time_major(att_ref, slab_ref, W).astype(BF16)
    row = pid * tm + lax.broadcasted_iota(jnp.int32, (tm, 1), 0)
    valid = row >= PAD * BATCH

    pc = _dot(xb, win_ref[:, 0:3 * W])
    u = jnp.where(valid, pc[:, 2 * W:3 * W] * pc[:, 0:W], 0.0)
    cbuf_scr[_HALO:_HALO + tm, :] = u
    y = (cb_ref[...] + cw_ref[0:1, :] * cbuf_scr[0:tm, :]
         + cw_ref[1:2, :] * cbuf_scr[BATCH:BATCH + tm, :] + cw_ref[2:3, :] * u)
    cbuf_scr[0:_HALO, :] = u[tm - _HALO:, :]
    y_b = _dot((pc[:, W:2 * W] * y).astype(BF16), cwo_ref[...])

    us = jnp.where(valid, _dot(xb, win_ref[:, 3 * W:4 * W]), 0.0)
    usb = us.astype(BF16)
    nb = S5_BLOCK_STATES
    for blk in range(S5_BLOCKS):
        bu_scr[:, 2 * nb * blk:2 * nb * (blk + 1)] = _dot(
            usb[:, blk * 128:(blk + 1) * 128], bblk_ref[blk])
    for blk in range(S5_BLOCKS):
        c_re = 2 * nb * blk
        c_im = c_re + nb
        a_re = ar_ref[blk]
        a_im = ai_ref[blk]

        def step(t, carry, c_re=c_re, c_im=c_im, a_re=a_re, a_im=a_im):
            s_re, s_im = carry
            r0 = pl.multiple_of(t * BATCH, BATCH)
            n_re = a_re * s_re - a_im * s_im + bu_scr[pl.ds(r0, BATCH), c_re:c_re + nb]
            n_im = a_re * s_im + a_im * s_re + bu_scr[pl.ds(r0, BATCH), c_im:c_im + nb]
            bu_scr[pl.ds(r0, BATCH), c_re:c_re + nb] = n_re
            bu_scr[pl.ds(r0, BATCH), c_im:c_im + nb] = n_im
            return n_re, n_im

        s_re, s_im = lax.fori_loop(
            0, MIX_STEPS, step, (st_scr[:, c_re:c_re + nb], st_scr[:, c_im:c_im + nb]), unroll=True)
        st_scr[:, c_re:c_re + nb] = s_re
        st_scr[:, c_im:c_im + nb] = s_im
    y = jnp.concatenate(
        [_dot(bu_scr[:, 2 * nb * blk:2 * nb * (blk + 1)].astype(BF16), cblk_ref[blk])
         for blk in range(S5_BLOCKS)], axis=1)
    y = _gelu_tanh(y + d_ref[...] * us)
    y = y * _sigmoid(_dot(y.astype(BF16), wglu_ref[...]) + bglu_ref[...])
    y_c = _dot(y.astype(BF16), swo_ref[...])

    y_a = _dot(att, mwo_ref[...])

    g0 = 4 * W
    mixed = _sigmoid(_dot(xb, win_ref[:, g0:g0 + D_MODEL])) * y_a
    mixed += _sigmoid(_dot(xb, win_ref[:, g0 + D_MODEL:g0 + 2 * D_MODEL])) * y_b
    mixed += _sigmoid(_dot(xb, win_ref[:, g0 + 2 * D_MODEL:g0 + 3 * D_MODEL])) * y_c
    z = _layer_norm(ALPHA * x + _dot(mixed.astype(BF16), wo_ref[...]), g_ref[...], b_ref[...])

    for j in range(_D_SLABS):
        slab_ref[j] = z[:, j * HEAD_LANES:(j + 1) * HEAD_LANES]
    for b in range(BATCH):
        for j in range(_D_SLABS):
            out_ref[b, :, j * HEAD_LANES:(j + 1) * HEAD_LANES] = slab_ref[
                j, pl.ds(b, MIX_STEPS, stride=BATCH), :]


def _mix(xn, att, win, cw, cb, cwo, bblk, ar, ai, cblk, d, wglu, bglu, swo, mwo, wo, g, b):
    consts = (win, cw, cb, cwo, bblk, ar, ai, cblk, d, wglu, bglu, swo, mwo, wo, g, b)
    tm = MIX_STEPS * BATCH
    meta_tiles = BLOCK_Q // MIX_STEPS
    tile = lambda w: pl.BlockSpec((BATCH, MIX_STEPS, w),
                                  lambda i: (0, (i + N_MIX_TILES - meta_tiles) % N_MIX_TILES, 0))
    return pl.pallas_call(
        _mix_kernel,
        grid=(N_MIX_TILES,),
        in_specs=[tile(D_MODEL), tile(MIX_WIDTH)] + [_const_spec(c.shape) for c in consts],
        out_specs=tile(D_MODEL),
        out_shape=jax.ShapeDtypeStruct((BATCH, LB, D_MODEL), F32),
        scratch_shapes=[
            pltpu.VMEM((tm, S5_COLS), F32),
            pltpu.VMEM((BATCH, S5_COLS), F32),
            pltpu.VMEM((tm + _HALO, MIX_WIDTH), F32),
            pltpu.VMEM((_D_SLABS, tm, HEAD_LANES), F32),
        ],
        compiler_params=pltpu.CompilerParams(
            dimension_semantics=("arbitrary",), vmem_limit_bytes=VMEM_LIMIT),
        name="mixers_merge",
    )(xn, att, *consts)


def _rope_tables():
    n = _N_TOK_TILES * TOK_TILE
    r = jnp.arange(n)
    pos = jnp.where(r < SEQ, r + N_META, r - SEQ - PAD).astype(F32)
    inv_freq = ROPE_BASE ** (-jnp.arange(0, QK_ROPE_DIM, 2, dtype=F32) / QK_ROPE_DIM)
    ang = pos[:, None] * inv_freq[None, :]
    cos2 = jnp.tile(jnp.cos(ang), (1, 2))
    sin2 = jnp.tile(jnp.sin(ang), (1, 2))
    zn = jnp.zeros((n, QK_NOPE_DIM), F32)
    zt = jnp.zeros((n, HEAD_LANES - QK_HEAD_DIM), F32)
    scale = QK_HEAD_DIM ** -0.5 * LOG2E
    cos_q = scale * jnp.concatenate([jnp.ones_like(zn), cos2, zt], axis=1)
    sin_q = scale * jnp.concatenate([zn, sin2, zt], axis=1)
    cos_k = jnp.concatenate([zn, cos2, zt], axis=1)
    sin_k = jnp.concatenate([zn, sin2, zt], axis=1)
    return jnp.concatenate([cos_q, sin_q, cos_k, sin_k], axis=1)


def _rot_half_cols(w):
    half = QK_ROPE_DIM // 2
    return jnp.concatenate([-w[..., half:], w[..., :half]], axis=-1)


def _qkv_weights(w_in, w_uq, w_ukv):
    zpad = lambda n: jnp.zeros((D_MODEL, n), F32)
    kr = w_in[:, _OFF_KR:_OFF_CONV]
    tail = HEAD_LANES - QK_HEAD_DIM
    wc = jnp.concatenate([
        w_in[:, :_OFF_KR],
        zpad(QK_NOPE_DIM), kr, zpad(tail),
        zpad(QK_NOPE_DIM), _rot_half_cols(kr), zpad(tail)], axis=1)
    uq = w_uq.reshape(Q_LORA_RANK, HEADS, QK_HEAD_DIM)
    zq = lambda n: jnp.zeros((Q_LORA_RANK, HEADS, n), F32)
    wqa = jnp.concatenate([uq, zq(tail)], axis=-1)
    wqb = jnp.concatenate([zq(QK_NOPE_DIM), _rot_half_cols(uq[..., QK_NOPE_DIM:]), zq(tail)], axis=-1)
    ukv = w_ukv.reshape(KV_LORA_RANK, HEADS, QK_NOPE_DIM + V_HEAD_DIM)
    zk = jnp.zeros((KV_LORA_RANK, HEADS, HEAD_LANES - QK_NOPE_DIM), F32)
    wuk = jnp.concatenate([ukv[..., :QK_NOPE_DIM], zk], axis=-1)
    wuv = jnp.concatenate(
        [ukv[..., QK_NOPE_DIM:], jnp.zeros((KV_LORA_RANK, HEADS, V_ROWS - V_HEAD_DIM), F32)], axis=-1)
    vones = jnp.zeros((HEADS, V_ROWS), F32).at[:, V_HEAD_DIM].set(1.0).reshape(1, HEADS * V_ROWS)
    flat = lambda w: w.reshape(w.shape[0], -1).astype(BF16)
    return wc.astype(BF16), flat(wqa), flat(wqb), flat(wuk), flat(wuv), vones


def _s5_weights(a_re, a_im, log_dt, b_re, b_im, c_re, c_im):
    dt = jnp.exp(log_dt)[:, None]
    mag = jnp.exp(dt * a_re)
    ab_re, ab_im = mag * jnp.cos(dt * a_im), mag * jnp.sin(dt * a_im)
    den = a_re * a_re + a_im * a_im
    nr, ni = ab_re - 1.0, ab_im
    coef_re = (nr * a_re + ni * a_im) / den
    coef_im = (ni * a_re - nr * a_im) / den
    bb_re = coef_re[..., None] * b_re - coef_im[..., None] * b_im
    bb_im = coef_re[..., None] * b_im + coef_im[..., None] * b_re
    gpb = S5_GROUPS // S5_BLOCKS
    eye = jnp.eye(gpb, dtype=F32)

    def in_blocks(bb):
        t = bb.transpose(0, 2, 1).reshape(S5_BLOCKS, gpb, S5_GROUP, S5_STATE)
        return jnp.einsum('bghn,gk->bghkn', t, eye).reshape(S5_BLOCKS, gpb * S5_GROUP, gpb * S5_STATE)

    def out_blocks(cc):
        t = cc.reshape(S5_BLOCKS, gpb, S5_GROUP, S5_STATE)
        return jnp.einsum('bghn,gk->bgnkh', t, eye).reshape(S5_BLOCKS, gpb * S5_STATE, gpb * S5_GROUP)

    bblk = jnp.concatenate([in_blocks(bb_re), in_blocks(bb_im)], axis=2).astype(BF16)
    cblk = jnp.concatenate([out_blocks(c_re), out_blocks(-c_im)], axis=1).astype(BF16)
    bcast = lambda a: jnp.broadcast_to(
        a.reshape(S5_BLOCKS, 1, S5_BLOCK_STATES), (S5_BLOCKS, BATCH, S5_BLOCK_STATES))
    return bblk, bcast(ab_re), bcast(ab_im), cblk


def kernel(x, meta, ffn1_w_gate, ffn1_w_up, ffn1_w_down, ln1_g, ln1_b, w_in, mla_q_norm_g, mla_w_uq, mla_kv_norm_g, mla_w_ukv, mla_w_o, conv_w, conv_b, conv_w_out, s5_a_re, s5_a_im, s5_log_dt, s5_b_re, s5_b_im, s5_c_re, s5_c_im, s5_d, s5_w_glu, s5_b_glu, s5_w_out, w_o, ln2_g, ln2_b, ffn2_w_gate, ffn2_w_up, ffn2_w_down, ln3_g, ln3_b):
    row = lambda v: v.reshape(1, -1).astype(F32)
    bf = lambda w: w.astype(BF16)
    h = jnp.concatenate([
        x.astype(F32),
        jnp.zeros((BATCH, PAD, D_MODEL), F32),
        jnp.broadcast_to(meta[None].astype(F32), (BATCH, N_META, D_MODEL))], axis=1)
    tab = _rope_tables()
    for i in range(DEPTH):
        xn = _ffn_ln(h, bf(ffn1_w_gate[i]), bf(ffn1_w_up[i]), bf(ffn1_w_down[i]), row(ln1_g[i]), row(ln1_b[i]))
        wc, wqa, wqb, wuk, wuv, vones = _qkv_weights(w_in[i], mla_w_uq[i], mla_w_ukv[i])
        q, k, v = _qkv(xn, wc, row(mla_q_norm_g[i]), row(mla_kv_norm_g[i]), wqa, wqb, wuk, wuv, vones, tab)
        att = _attention(q, k, v)
        bblk, ar, ai, cblk = _s5_weights(s5_a_re[i], s5_a_im[i], s5_log_dt[i], s5_b_re[i], s5_b_im[i],
                                         s5_c_re[i], s5_c_im[i])
        h = _mix(xn, att, bf(w_in[i][:, _OFF_CONV:]), conv_w[i].astype(F32), row(conv_b[i]),
                 bf(conv_w_out[i]), bblk, ar, ai, cblk, row(s5_d[i]), bf(s5_w_glu[i]), row(s5_b_glu[i]),
                 bf(s5_w_out[i]), bf(mla_w_o[i]), bf(w_o[i]), row(ln2_g[i]), row(ln2_b[i]))
        last = i == DEPTH - 1
        h = _ffn_ln(h, bf(ffn2_w_gate[i]), bf(ffn2_w_up[i]), bf(ffn2_w_down[i]), row(ln3_g[i]), row(ln3_b[i]),
                    out_rows=SEQ if last else LB)
    return h
```

```python
import functools
import math

import jax
import jax.numpy as jnp
from jax import lax
from jax.experimental import pallas as pl
from jax.experimental.pallas import tpu as pltpu

D_MODEL = 1024
BATCH = 8
SEQ = 2048
DEPTH = 2
N_META = 16
BLOCK_Q = 128
MIX_WIDTH = D_MODEL // 2
HEADS = 8
V_HEAD_DIM = 64
QK_NOPE_DIM = 64
QK_ROPE_DIM = 32
QK_HEAD_DIM = QK_NOPE_DIM + QK_ROPE_DIM
Q_LORA_RANK = 384
KV_LORA_RANK = 256
ROPE_BASE = 10000.0
S5_GROUP = 16
S5_GROUPS = 32
S5_STATE = 64
D_FF = 2816
ALPHA = (2.0 * DEPTH) ** 0.25
LN_EPS = 1e-5
RMS_EPS = 1e-6

HEAD_LANES = 128
QKV_WIDTH = HEADS * HEAD_LANES
PAD = BLOCK_Q - N_META
LB = SEQ + BLOCK_Q
ROWS = LB * BATCH
TOK_TILE = 512
ATT_TILE = 2 * BLOCK_Q
N_SEQ_TILES = SEQ // ATT_TILE
ATT_LEN = (N_SEQ_TILES + 1) * ATT_TILE
V_ROWS = V_HEAD_DIM + 16
LOG2E = math.log2(math.e)
S5_BLOCKS = 4
S5_BLOCK_STATES = (S5_GROUPS // S5_BLOCKS) * S5_STATE
S5_COLS = 2 * S5_GROUPS * S5_STATE
MIX_STEPS = 64
N_MIX_TILES = LB // MIX_STEPS

_OFF_CQ, _OFF_CKV, _OFF_KR, _OFF_CONV = 0, 384, 640, 672
_OFF_GATES = _OFF_CONV + 4 * MIX_WIDTH
D_IN = _OFF_GATES + 3 * D_MODEL

V7X_VMEM_BYTES = 64 * 1024 * 1024
VMEM_LIMIT = 56 * 1024 * 1024

F32 = jnp.float32
BF16 = jnp.bfloat16


def _dot(a, b):
    return jnp.dot(a, b, preferred_element_type=F32)


def _const_spec(shape):
    nd = len(shape)
    return pl.BlockSpec(shape, lambda *_: (0,) * nd, pipeline_mode=pl.Buffered(1))


def _layer_norm(y, g, b):
    mu = jnp.mean(y, axis=-1, keepdims=True)
    yc = y - mu
    var = jnp.mean(yc * yc, axis=-1, keepdims=True)
    return yc * lax.rsqrt(var + LN_EPS) * g + b


def _rms_norm(y, g):
    return y * lax.rsqrt(jnp.mean(y * y, axis=-1, keepdims=True) + RMS_EPS) * g


def _sigmoid(y):
    return 1.0 / (1.0 + jnp.exp(-y))


def _gelu_tanh(y):
    return 0.5 * y * (1.0 + jnp.tanh(math.sqrt(2.0 / math.pi) * (y + 0.044715 * (y * y * y))))


def _ffn_ln_kernel(x_ref, wg_ref, wu_ref, wd_ref, g_ref, b_ref, o_ref, *, ff_chunk, sub_rows):
    for r in range(x_ref.shape[0] // sub_rows):
        rows = slice(r * sub_rows, (r + 1) * sub_rows)
        x = x_ref[rows, :]
        xb = x.astype(BF16)
        acc = None
        for c in range(D_FF // ff_chunk):
            sl = slice(c * ff_chunk, (c + 1) * ff_chunk)
            gate = _dot(xb, wg_ref[:, sl])
            up = _dot(xb, wu_ref[:, sl])
            hmid = (gate * _sigmoid(gate) * up).astype(BF16)
            part = _dot(hmid, wd_ref[sl, :])
            acc = part if acc is None else acc + part
        o_ref[rows, :] = _layer_norm(ALPHA * x + 0.5 * acc, g_ref[...], b_ref[...])


def _ffn_ln(x, wg, wu, wd, g, b, *, out_rows=LB, tm=2 * TOK_TILE, ff_chunk=1408):
    weights = [_const_spec((D_MODEL, D_FF)), _const_spec((D_MODEL, D_FF)), _const_spec((D_FF, D_MODEL)),
               _const_spec((1, D_MODEL)), _const_spec((1, D_MODEL))]
    params = dict(name="ffn_ln")
    kern = functools.partial(_ffn_ln_kernel, ff_chunk=ff_chunk, sub_rows=TOK_TILE)
    if out_rows == LB:
        flat = pl.BlockSpec((tm, D_MODEL), lambda i: (i, 0))
        out = pl.pallas_call(
            kern, grid=(ROWS // tm,), in_specs=[flat] + weights, out_specs=flat,
            out_shape=jax.ShapeDtypeStruct((ROWS, D_MODEL), F32),
            compiler_params=pltpu.CompilerParams(
                dimension_semantics=("arbitrary",), vmem_limit_bytes=VMEM_LIMIT),
            **params)(x.reshape(ROWS, D_MODEL), wg, wu, wd, g, b)
        return out.reshape(BATCH, LB, D_MODEL)
    per_seq = pl.BlockSpec((None, tm, D_MODEL), lambda s, i: (s, i, 0))
    return pl.pallas_call(
        kern, grid=(BATCH, out_rows // tm), in_specs=[per_seq] + weights, out_specs=per_seq,
        out_shape=jax.ShapeDtypeStruct((BATCH, out_rows, D_MODEL), F32),
        compiler_params=pltpu.CompilerParams(
            dimension_semantics=("arbitrary", "arbitrary"), vmem_limit_bytes=VMEM_LIMIT),
        **params)(x, wg, wu, wd, g, b)


_WC_WIDTH = Q_LORA_RANK + KV_LORA_RANK + 2 * HEAD_LANES
_N_TOK_TILES = -(-LB // TOK_TILE)


def _qkv_kernel(x_ref, wc_ref, gq_ref, gkv_ref, wqa_ref, wqb_ref, wuk_ref, wuv_ref, vones_ref, tab_ref,
                q_ref, k_ref, v_ref):
    j = pl.program_id(1)
    row = lax.broadcasted_iota(jnp.int32, (TOK_TILE, 1), 0)
    keep = (j < _N_TOK_TILES - 1) | (row < LB - (_N_TOK_TILES - 1) * TOK_TILE)
    xb = jnp.where(keep, x_ref[...], 0.0).astype(BF16)
    c = _dot(xb, wc_ref[...])
    c_q = c[:, :Q_LORA_RANK]
    c_kv = c[:, Q_LORA_RANK:Q_LORA_RANK + KV_LORA_RANK]
    k_r = c[:, _WC_WIDTH - 2 * HEAD_LANES:_WC_WIDTH - HEAD_LANES]
    k_r_rot = c[:, _WC_WIDTH - HEAD_LANES:]
    qn = _rms_norm(c_q, gq_ref[...]).astype(BF16)
    kvn = _rms_norm(c_kv, gkv_ref[...]).astype(BF16)
    cos_q = tab_ref[:, 0 * HEAD_LANES:1 * HEAD_LANES]
    sin_q = tab_ref[:, 1 * HEAD_LANES:2 * HEAD_LANES]
    cos_k = tab_ref[:, 2 * HEAD_LANES:3 * HEAD_LANES]
    sin_k = tab_ref[:, 3 * HEAD_LANES:4 * HEAD_LANES]
    k_rope = k_r * cos_k + k_r_rot * sin_k
    q_a = _dot(qn, wqa_ref[...])
    q_b = _dot(qn, wqb_ref[...])
    k_nope = _dot(kvn, wuk_ref[...])
    for h in range(HEADS):
        sl = slice(h * HEAD_LANES, (h + 1) * HEAD_LANES)
        q_ref[:, sl] = (q_a[:, sl] * cos_q + q_b[:, sl] * sin_q).astype(BF16)
        k_ref[:, sl] = (k_nope[:, sl] + k_rope).astype(BF16)
    v = _dot(kvn, wuv_ref[...]) + vones_ref[...]
    for t in range(TOK_TILE // ATT_TILE):
        v_ref[t] = v[t * ATT_TILE:(t + 1) * ATT_TILE, :].T.astype(BF16)


def _qkv(xn, wc, gq, gkv, wqa, wqb, wuk, wuv, vones, tab):
    seq = lambda w: pl.BlockSpec((None, TOK_TILE, w), lambda s, j: (s, j, 0))
    seq_shape = jax.ShapeDtypeStruct((BATCH, ATT_LEN, QKV_WIDTH), BF16)
    v_tiles = TOK_TILE // ATT_TILE
    return pl.pallas_call(
        _qkv_kernel,
        grid=(BATCH, _N_TOK_TILES),
        in_specs=[
            seq(D_MODEL),
            _const_spec((D_MODEL, _WC_WIDTH)),
            _const_spec((1, Q_LORA_RANK)),
            _const_spec((1, KV_LORA_RANK)),
            _const_spec((Q_LORA_RANK, QKV_WIDTH)),
            _const_spec((Q_LORA_RANK, QKV_WIDTH)),
            _const_spec((KV_LORA_RANK, QKV_WIDTH)),
            _const_spec((KV_LORA_RANK, HEADS * V_ROWS)),
            _const_spec((1, HEADS * V_ROWS)),
            pl.BlockSpec((TOK_TILE, 4 * HEAD_LANES), lambda s, j: (j, 0)),
        ],
        out_specs=[seq(QKV_WIDTH), seq(QKV_WIDTH),
                   pl.BlockSpec((None, v_tiles, HEADS * V_ROWS, ATT_TILE), lambda s, j: (s, j, 0, 0))],
        out_shape=[seq_shape, seq_shape,
                   jax.ShapeDtypeStruct((BATCH, N_SEQ_TILES + 1, HEADS * V_ROWS, ATT_TILE), BF16)],
        compiler_params=pltpu.CompilerParams(
            dimension_semantics=("arbitrary", "arbitrary"), vmem_limit_bytes=VMEM_LIMIT),
        name="qkv_proj",
    )(xn, wc, gq, gkv, wqa, wqb, wuk, wuv, vones, tab)


def _scores(q_ref, k_ref, s_scr, smax_scr, slot, c, mask):
    k0 = c * ATT_TILE if isinstance(c, int) else pl.multiple_of(c * ATT_TILE, ATT_TILE)
    for h in range(HEADS):
        sl = slice(h * HEAD_LANES, (h + 1) * HEAD_LANES)
        s_t = lax.dot_general(
            k_ref[pl.ds(k0, ATT_TILE), sl], q_ref[:, sl], (((1,), (1,)), ((), ())),
            preferred_element_type=F32)
        if mask is not None:
            s_t = jnp.where(mask, s_t, -1e30)
        s_scr[slot, h] = s_t
        smax_scr[slot, h] = jnp.max(s_t, axis=0, keepdims=True)


def _attend(v_ref, s_scr, smax_scr, m_scr, acc_scr, slot, c):
    for h in range(HEADS):
        m_old = m_scr[h]
        m_new = jnp.maximum(m_old, smax_scr[slot, h])
        p_t = jnp.exp2(s_scr[slot, h] - m_new).astype(BF16)
        pv = _dot(v_ref[c, h * V_ROWS:(h + 1) * V_ROWS, :], p_t)
        acc_scr[h] = jnp.exp2(m_old - m_new) * acc_scr[h] + pv
        m_scr[h] = m_new


_META_ROW0 = N_SEQ_TILES * ATT_TILE


def _scores_meta(q_ref, k_ref, sm_scr, m_scr, mask):
    for h in range(HEADS):
        sl = slice(h * HEAD_LANES, (h + 1) * HEAD_LANES)
        s_t = lax.dot_general(
            k_ref[_META_ROW0:_META_ROW0 + BLOCK_Q, sl], q_ref[:, sl], (((1,), (1,)), ((), ())),
            preferred_element_type=F32)
        s_t = jnp.where(mask, s_t, -1e30)
        sm_scr[h] = s_t
        m_scr[h] = jnp.max(s_t, axis=0, keepdims=True)


def _attend_meta(v_ref, sm_scr, m_scr, acc_scr):
    for h in range(HEADS):
        p_t = jnp.exp2(sm_scr[h] - m_scr[h]).astype(BF16)
        acc_scr[h] = _dot(v_ref[N_SEQ_TILES, h * V_ROWS:(h + 1) * V_ROWS, 0:BLOCK_Q], p_t)


def _attn_kernel(q_ref, k_ref, v_ref, o_ref, s_scr, smax_scr, sm_scr, m_scr, acc_scr):
    i = pl.program_id(1)
    is_seq = i < N_SEQ_TILES
    q_off = lax.broadcasted_iota(jnp.int32, (ATT_TILE, ATT_TILE), 1)
    k_off = lax.broadcasted_iota(jnp.int32, (ATT_TILE, ATT_TILE), 0)
    causal = k_off <= q_off
    mq_off = lax.broadcasted_iota(jnp.int32, (BLOCK_Q, ATT_TILE), 1)
    mk_off = lax.broadcasted_iota(jnp.int32, (BLOCK_Q, ATT_TILE), 0)
    meta_mask = (mk_off >= PAD) & ((mk_off <= mq_off) | is_seq)

    state = (v_ref, s_scr, smax_scr, m_scr, acc_scr)
    scores = functools.partial(_scores, q_ref, k_ref, s_scr, smax_scr)
    attend = functools.partial(_attend, *state)
    _scores_meta(q_ref, k_ref, sm_scr, m_scr, meta_mask)
    scores(0, 0, causal | (i > 0))
    _attend_meta(v_ref, sm_scr, m_scr, acc_scr)

    def two_tiles(pp, carry):
        p = 1 + 2 * pp
        scores(1, p, None)
        attend(0, p - 1)
        scores(0, p + 1, None)
        attend(1, p)
        return carry

    lax.fori_loop(0, jnp.where(is_seq, (i - 1) // 2, 0), two_tiles, 0)

    @pl.when(is_seq & (i == 0))
    def _():
        attend(0, 0)

    @pl.when(is_seq & (i % 2 == 1))
    def _():
        scores(1, i, causal)
        attend(0, i - 1)
        attend(1, i)

    @pl.when(is_seq & (i % 2 == 0) & (i > 0))
    def _():
        scores(1, i - 1, None)
        attend(0, i - 2)
        scores(0, i, causal)
        attend(1, i - 1)
        attend(0, i)

    o_t = jnp.concatenate(
        [acc_scr[h, 0:V_HEAD_DIM, :] / acc_scr[h, V_HEAD_DIM:V_HEAD_DIM + 1, :] for h in range(HEADS)],
        axis=0)
    o_ref[...] = o_t.T.astype(BF16)


def _attention(q, k, v_t):
    return pl.pallas_call(
        _attn_kernel,
        grid=(BATCH, N_SEQ_TILES + 1),
        in_specs=[
            pl.BlockSpec((None, ATT_TILE, QKV_WIDTH), lambda s, i: (s, i, 0)),
            pl.BlockSpec((None, ATT_LEN, QKV_WIDTH), lambda s, i: (s, 0, 0)),
            pl.BlockSpec((None, N_SEQ_TILES + 1, HEADS * V_ROWS, ATT_TILE), lambda s, i: (s, 0, 0, 0)),
        ],
        out_specs=pl.BlockSpec((None, ATT_TILE, MIX_WIDTH), lambda s, i: (s, i, 0)),
        out_shape=jax.ShapeDtypeStruct((BATCH, ATT_LEN, MIX_WIDTH), BF16),
        scratch_shapes=[pltpu.VMEM((2, HEADS, ATT_TILE, ATT_TILE), F32),
                        pltpu.VMEM((2, HEADS, 1, ATT_TILE), F32),
                        pltpu.VMEM((HEADS, BLOCK_Q, ATT_TILE), F32),
                        pltpu.VMEM((HEADS, 1, ATT_TILE), F32),
                        pltpu.VMEM((HEADS, V_ROWS, ATT_TILE), F32)],
        compiler_params=pltpu.CompilerParams(
            dimension_semantics=("arbitrary", "arbitrary"), vmem_limit_bytes=VMEM_LIMIT),
        name="mla_attention",
    )(q, k, v_t)


_HALO = 2 * BATCH
_D_SLABS = D_MODEL // HEAD_LANES


def _to_time_major(src_ref, slab_ref, width):
    n = width // HEAD_LANES
    for b in range(BATCH):
        for j in range(n):
            slab_ref[j, pl.ds(b, MIX_STEPS, stride=BATCH), :] = (
                src_ref[b, :, j * HEAD_LANES:(j + 1) * HEAD_LANES].astype(F32))
    return jnp.concatenate([slab_ref[j] for j in range(n)], axis=1)


def _mix_kernel(x_ref, att_ref, win_ref, cw_ref, cb_ref, cwo_ref, bblk_ref, ar_ref, ai_ref,
                cblk_ref, d_ref, wglu_ref, bglu_ref, swo_ref, mwo_ref, wo_ref, g_ref, b_ref,
                out_ref, bu_scr, st_scr, cbuf_scr, slab_ref):
    pid = pl.program_id(0)
    tm = MIX_STEPS * BATCH
    W = MIX_WIDTH

    @pl.when(pid == 0)
    def _():
        st_scr[...] = jnp.zeros_like(st_scr)
        cbuf_scr[0:_HALO, :] = jnp.zeros((_HALO, W), F32)

    x = _to_time_major(x_ref, slab_ref, D_MODEL)
    xb = x.astype(BF16)
    att = _to_time_major(att_ref, slab_ref, W).astype(BF16)
    row = pid * tm + lax.broadcasted_iota(jnp.int32, (tm, 1), 0)
    valid = row >= PAD * BATCH

    pc = _dot(xb, win_ref[:, 0:3 * W])
    u = jnp.where(valid, pc[:, 2 * W:3 * W] * pc[:, 0:W], 0.0)
    cbuf_scr[_HALO:_HALO + tm, :] = u
    y = (cb_ref[...] + cw_ref[0:1, :] * cbuf_scr[0:tm, :]
         + cw_ref[1:2, :] * cbuf_scr[BATCH:BATCH + tm, :] + cw_ref[2:3, :] * u)
    cbuf_scr[0:_HALO, :] = u[tm - _HALO:, :]
    y_b = _dot((pc[:, W:2 * W] * y).astype(BF16), cwo_ref[...])

    us = jnp.where(valid, _dot(xb, win_ref[:, 3 * W:4 * W]), 0.0)
    usb = us.astype(BF16)
    nb = S5_BLOCK_STATES
    for blk in range(S5_BLOCKS):
        bu_scr[:, 2 * nb * blk:2 * nb * (blk + 1)] = _dot(
            usb[:, blk * 128:(blk + 1) * 128], bblk_ref[blk])
    for blk in range(S5_BLOCKS):
        c_re = 2 * nb * blk
        c_im = c_re + nb
        a_re = ar_ref[blk]
        a_im = ai_ref[blk]

        def step(t, carry, c_re=c_re, c_im=c_im, a_re=a_re, a_im=a_im):
            s_re, s_im = carry
            r0 = pl.multiple_of(t * BATCH, BATCH)
            n_re = a_re * s_re - a_im * s_im + bu_scr[pl.ds(r0, BATCH), c_re:c_re + nb]
            n_im = a_re * s_im + a_im * s_re + bu_scr[pl.ds(r0, BATCH), c_im:c_im + nb]
            bu_scr[pl.ds(r0, BATCH), c_re:c_re + nb] = n_re
            bu_scr[pl.ds(r0, BATCH), c_im:c_im + nb] = n_im
            return n_re, n_im

        s_re, s_im = lax.fori_loop(
            0, MIX_STEPS, step, (st_scr[:, c_re:c_re + nb], st_scr[:, c_im:c_im + nb]), unroll=True)
        st_scr[:, c_re:c_re + nb] = s_re
        st_scr[:, c_im:c_im + nb] = s_im
    y = jnp.concatenate(
        [_dot(bu_scr[:, 2 * nb * blk:2 * nb * (blk + 1)].astype(BF16), cblk_ref[blk])
         for blk in range(S5_BLOCKS)], axis=1)
    y = _gelu_tanh(y + d_ref[...] * us)
    y = y * _sigmoid(_dot(y.astype(BF16), wglu_ref[...]) + bglu_ref[...])
    y_c = _dot(y.astype(BF16), swo_ref[...])

    y_a = _dot(att, mwo_ref[...])

    g0 = 4 * W
    mixed = _sigmoid(_dot(xb, win_ref[:, g0:g0 + D_MODEL])) * y_a
    mixed += _sigmoid(_dot(xb, win_ref[:, g0 + D_MODEL:g0 + 2 * D_MODEL])) * y_b
    mixed += _sigmoid(_dot(xb, win_ref[:, g0 + 2 * D_MODEL:g0 + 3 * D_MODEL])) * y_c
    z = _layer_norm(ALPHA * x + _dot(mixed.astype(BF16), wo_ref[...]), g_ref[...], b_ref[...])

    for j in range(_D_SLABS):
        slab_ref[j] = z[:, j * HEAD_LANES:(j + 1) * HEAD_LANES]
    for b in range(BATCH):
        for j in range(_D_SLABS):
            out_ref[b, :, j * HEAD_LANES:(j + 1) * HEAD_LANES] = slab_ref[
                j, pl.ds(b, MIX_STEPS, stride=BATCH), :]


def _mix(xn, att, win, cw, cb, cwo, bblk, ar, ai, cblk, d, wglu, bglu, swo, mwo, wo, g, b):
    consts = (win, cw, cb, cwo, bblk, ar, ai, cblk, d, wglu, bglu, swo, mwo, wo, g, b)
    tm = MIX_STEPS * BATCH
    meta_tiles = BLOCK_Q // MIX_STEPS
    tile = lambda w: pl.BlockSpec((BATCH, MIX_STEPS, w),
                                  lambda i: (0, (i + N_MIX_TILES - meta_tiles) % N_MIX_TILES, 0))
    return pl.pallas_call(
        _mix_kernel,
        grid=(N_MIX_TILES,),
        in_specs=[tile(D_MODEL), tile(MIX_WIDTH)] + [_const_spec(c.shape) for c in consts],
        out_specs=tile(D_MODEL),
        out_shape=jax.ShapeDtypeStruct((BATCH, LB, D_MODEL), F32),
        scratch_shapes=[
            pltpu.VMEM((tm, S5_COLS), F32),
            pltpu.VMEM((BATCH, S5_COLS), F32),
            pltpu.VMEM((tm + _HALO, MIX_WIDTH), F32),
            pltpu.VMEM((_D_SLABS, tm, HEAD_LANES), F32),
        ],
        compiler_params=pltpu.CompilerParams(
            dimension_semantics=("arbitrary",), vmem_limit_bytes=VMEM_LIMIT),
        name="mixers_merge",
    )(xn, att, *consts)


def _rope_tables():
    n = _N_TOK_TILES * TOK_TILE
    r = jnp.arange(n)
    pos = jnp.where(r < SEQ, r + N_META, r - SEQ - PAD).astype(F32)
    inv_freq = ROPE_BASE ** (-jnp.arange(0, QK_ROPE_DIM, 2, dtype=F32) / QK_ROPE_DIM)
    ang = pos[:, None] * inv_freq[None, :]
    cos2 = jnp.tile(jnp.cos(ang), (1, 2))
    sin2 = jnp.tile(jnp.sin(ang), (1, 2))
    zn = jnp.zeros((n, QK_NOPE_DIM), F32)
    zt = jnp.zeros((n, HEAD_LANES - QK_HEAD_DIM), F32)
    scale = QK_HEAD_DIM ** -0.5 * LOG2E
    cos_q = scale * jnp.concatenate([jnp.ones_like(zn), cos2, zt], axis=1)
    sin_q = scale * jnp.concatenate([zn, sin2, zt], axis=1)
    cos_k = jnp.concatenate([zn, cos2, zt], axis=1)
    sin_k = jnp.concatenate([zn, sin2, zt], axis=1)
    return jnp.concatenate([cos_q, sin_q, cos_k, sin_k], axis=1)


def _rot_half_cols(w):
    half = QK_ROPE_DIM // 2
    return jnp.concatenate([-w[..., half:], w[..., :half]], axis=-1)


def _qkv_weights(w_in, w_uq, w_ukv):
    zpad = lambda n: jnp.zeros((D_MODEL, n), F32)
    kr = w_in[:, _OFF_KR:_OFF_CONV]
    tail = HEAD_LANES - QK_HEAD_DIM
    wc = jnp.concatenate([
        w_in[:, :_OFF_KR],
        zpad(QK_NOPE_DIM), kr, zpad(tail),
        zpad(QK_NOPE_DIM), _rot_half_cols(kr), zpad(tail)], axis=1)
    uq = w_uq.reshape(Q_LORA_RANK, HEADS, QK_HEAD_DIM)
    zq = lambda n: jnp.zeros((Q_LORA_RANK, HEADS, n), F32)
    wqa = jnp.concatenate([uq, zq(tail)], axis=-1)
    wqb = jnp.concatenate([zq(QK_NOPE_DIM), _rot_half_cols(uq[..., QK_NOPE_DIM:]), zq(tail)], axis=-1)
    ukv = w_ukv.reshape(KV_LORA_RANK, HEADS, QK_NOPE_DIM + V_HEAD_DIM)
    zk = jnp.zeros((KV_LORA_RANK, HEADS, HEAD_LANES - QK_NOPE_DIM), F32)
    wuk = jnp.concatenate([ukv[..., :QK_NOPE_DIM], zk], axis=-1)
    wuv = jnp.concatenate(
        [ukv[..., QK_NOPE_DIM:], jnp.zeros((KV_LORA_RANK, HEADS, V_ROWS - V_HEAD_DIM), F32)], axis=-1)
    vones = jnp.zeros((HEADS, V_ROWS), F32).at[:, V_HEAD_DIM].set(1.0).reshape(1, HEADS * V_ROWS)
    flat = lambda w: w.reshape(w.shape[0], -1).astype(BF16)
    return wc.astype(BF16), flat(wqa), flat(wqb), flat(wuk), flat(wuv), vones


def _s5_weights(a_re, a_im, log_dt, b_re, b_im, c_re, c_im):
    dt = jnp.exp(log_dt)[:, None]
    mag = jnp.exp(dt * a_re)
    ab_re, ab_im = mag * jnp.cos(dt * a_im), mag * jnp.sin(dt * a_im)
    den = a_re * a_re + a_im * a_im
    nr, ni = ab_re - 1.0, ab_im
    coef_re = (nr * a_re + ni * a_im) / den
    coef_im = (ni * a_re - nr * a_im) / den
    bb_re = coef_re[..., None] * b_re - coef_im[..., None] * b_im
    bb_im = coef_re[..., None] * b_im + coef_im[..., None] * b_re
    gpb = S5_GROUPS // S5_BLOCKS
    eye = jnp.eye(gpb, dtype=F32)

    def in_blocks(bb):
        t = bb.transpose(0, 2, 1).reshape(S5_BLOCKS, gpb, S5_GROUP, S5_STATE)
        return jnp.einsum('bghn,gk->bghkn', t, eye).reshape(S5_BLOCKS, gpb * S5_GROUP, gpb * S5_STATE)

    def out_blocks(cc):
        t = cc.reshape(S5_BLOCKS, gpb, S5_GROUP, S5_STATE)
        return jnp.einsum('bghn,gk->bgnkh', t, eye).reshape(S5_BLOCKS, gpb * S5_STATE, gpb * S5_GROUP)

    bblk = jnp.concatenate([in_blocks(bb_re), in_blocks(bb_im)], axis=2).astype(BF16)
    cblk = jnp.concatenate([out_blocks(c_re), out_blocks(-c_im)], axis=1).astype(BF16)
    bcast = lambda a: jnp.broadcast_to(
        a.reshape(S5_BLOCKS, 1, S5_BLOCK_STATES), (S5_BLOCKS, BATCH, S5_BLOCK_STATES))
    return bblk, bcast(ab_re), bcast(ab_im), cblk


def kernel(x, meta, ffn1_w_gate, ffn1_w_up, ffn1_w_down, ln1_g, ln1_b, w_in, mla_q_norm_g, mla_w_uq, mla_kv_norm_g, mla_w_ukv, mla_w_o, conv_w, conv_b, conv_w_out, s5_a_re, s5_a_im, s5_log_dt, s5_b_re, s5_b_im, s5_c_re, s5_c_im, s5_d, s5_w_glu, s5_b_glu, s5_w_out, w_o, ln2_g, ln2_b, ffn2_w_gate, ffn2_w_up, ffn2_w_down, ln3_g, ln3_b):
    row = lambda v: v.reshape(1, -1).astype(F32)
    bf = lambda w: w.astype(BF16)
    h = jnp.concatenate([
        x.astype(F32),
        jnp.zeros((BATCH, PAD, D_MODEL), F32),
        jnp.broadcast_to(meta[None].astype(F32), (BATCH, N_META, D_MODEL))], axis=1)
    tab = _rope_tables()
    for i in range(DEPTH):
        xn = _ffn_ln(h, bf(ffn1_w_gate[i]), bf(ffn1_w_up[i]), bf(ffn1_w_down[i]), row(ln1_g[i]), row(ln1_b[i]))
        wc, wqa, wqb, wuk, wuv, vones = _qkv_weights(w_in[i], mla_w_uq[i], mla_w_ukv[i])
        q, k, v = _qkv(xn, wc, row(mla_q_norm_g[i]), row(mla_kv_norm_g[i]), wqa, wqb, wuk, wuv, vones, tab)
        att = _attention(q, k, v)
        bblk, ar, ai, cblk = _s5_weights(s5_a_re[i], s5_a_im[i], s5_log_dt[i], s5_b_re[i], s5_b_im[i],
                                         s5_c_re[i], s5_c_im[i])
        h = _mix(xn, att, bf(w_in[i][:, _OFF_CONV:]), conv_w[i].astype(F32), row(conv_b[i]),
                 bf(conv_w_out[i]), bblk, ar, ai, cblk, row(s5_d[i]), bf(s5_w_glu[i]), row(s5_b_glu[i]),
                 bf(s5_w_out[i]), bf(mla_w_o[i]), bf(w_o[i]), row(ln2_g[i]), row(ln2_b[i]))
        last = i == DEPTH - 1
        h = _ffn_ln(h, bf(ffn2_w_gate[i]), bf(ffn2_w_up[i]), bf(ffn2_w_down[i]), row(ln3_g[i]), row(ln3_b[i]),
                    out_rows=SEQ if last else LB)
    return h
```

```python
import functools
import math

import jax
import jax.numpy as jnp
from jax import lax
from jax.experimental import pallas as pl
from jax.experimental.pallas import tpu as pltpu

D_MODEL = 1024
BATCH = 8
SEQ = 2048
DEPTH = 2
N_META = 16
BLOCK_Q = 128
MIX_WIDTH = D_MODEL // 2
HEADS = 8
V_HEAD_DIM = 64
QK_NOPE_DIM = 64
QK_ROPE_DIM = 32
QK_HEAD_DIM = QK_NOPE_DIM + QK_ROPE_DIM
Q_LORA_RANK = 384
KV_LORA_RANK = 256
ROPE_BASE = 10000.0
S5_GROUP = 16
S5_GROUPS = 32
S5_STATE = 64
D_FF = 2816
ALPHA = (2.0 * DEPTH) ** 0.25
LN_EPS = 1e-5
RMS_EPS = 1e-6

HEAD_LANES = 128
QKV_WIDTH = HEADS * HEAD_LANES
TOK_TILE = 512
ATT_TILE = 2 * BLOCK_Q
N_SEQ_TILES = SEQ // ATT_TILE
V_ROWS = V_HEAD_DIM + 16
LOG2E = math.log2(math.e)
S5_BLOCKS = 4
S5_BLOCK_STATES = (S5_GROUPS // S5_BLOCKS) * S5_STATE
S5_COLS = 2 * S5_GROUPS * S5_STATE
MIX_STEPS = 64

_OFF_CQ, _OFF_CKV, _OFF_KR, _OFF_CONV = 0, 384, 640, 672
_OFF_GATES = _OFF_CONV + 4 * MIX_WIDTH
D_IN = _OFF_GATES + 3 * D_MODEL

V7X_VMEM_BYTES = 64 * 1024 * 1024
VMEM_LIMIT = 56 * 1024 * 1024

F32 = jnp.float32
BF16 = jnp.bfloat16


def _dot(a, b):
    return jnp.dot(a, b, preferred_element_type=F32)


def _const_spec(shape):
    nd = len(shape)
    return pl.BlockSpec(shape, lambda *_: (0,) * nd, pipeline_mode=pl.Buffered(1))


def _layer_norm(y, g, b):
    mu = jnp.mean(y, axis=-1, keepdims=True)
    yc = y - mu
    var = jnp.mean(yc * yc, axis=-1, keepdims=True)
    return yc * lax.rsqrt(var + LN_EPS) * g + b


def _rms_norm(y, g):
    return y * lax.rsqrt(jnp.mean(y * y, axis=-1, keepdims=True) + RMS_EPS) * g


def _sigmoid(y):
    return 1.0 / (1.0 + jnp.exp(-y))


def _gelu_tanh(y):
    return 0.5 * y * (1.0 + jnp.tanh(math.sqrt(2.0 / math.pi) * (y + 0.044715 * (y * y * y))))


def _ffn_ln_kernel(x_ref, wg_ref, wu_ref, wd_ref, g_ref, b_ref, o_ref, *, ff_chunk, sub_rows):
    for r in range(x_ref.shape[0] // sub_rows):
        rows = slice(r * sub_rows, (r + 1) * sub_rows)
        x = x_ref[rows, :]
        xb = x.astype(BF16)
        acc = None
        for c in range(D_FF // ff_chunk):
            sl = slice(c * ff_chunk, (c + 1) * ff_chunk)
            gate = _dot(xb, wg_ref[:, sl])
            up = _dot(xb, wu_ref[:, sl])
            hmid = (gate * _sigmoid(gate) * up).astype(BF16)
            part = _dot(hmid, wd_ref[sl, :])
            acc = part if acc is None else acc + part
        o_ref[rows, :] = _layer_norm(ALPHA * x + 0.5 * acc, g_ref[...], b_ref[...])


def _ffn_ln(x, wg, wu, wd, g, b, *, ff_chunk=1408):
    rows = x.shape[0]
    tm = min(rows, 2 * TOK_TILE)
    flat = pl.BlockSpec((tm, D_MODEL), lambda i: (i, 0))
    return pl.pallas_call(
        functools.partial(_ffn_ln_kernel, ff_chunk=ff_chunk, sub_rows=min(tm, TOK_TILE)),
        grid=(rows // tm,),
        in_specs=[flat, _const_spec((D_MODEL, D_FF)), _const_spec((D_MODEL, D_FF)),
                  _const_spec((D_FF, D_MODEL)), _const_spec((1, D_MODEL)), _const_spec((1, D_MODEL))],
        out_specs=flat,
        out_shape=jax.ShapeDtypeStruct((rows, D_MODEL), F32),
        compiler_params=pltpu.CompilerParams(
            dimension_semantics=("arbitrary",), vmem_limit_bytes=VMEM_LIMIT),
        name="ffn_ln",
    )(x, wg, wu, wd, g, b)


_WC_WIDTH = Q_LORA_RANK + KV_LORA_RANK + 2 * HEAD_LANES


def _qkv_kernel(x_ref, wc_ref, gq_ref, gkv_ref, wqa_ref, wqb_ref, wuk_ref, wuv_ref, vones_ref, tab_ref,
                q_ref, k_ref, v_ref, *, transpose_v):
    xb = x_ref[...].astype(BF16)
    c = _dot(xb, wc_ref[...])
    c_q = c[:, :Q_LORA_RANK]
    c_kv = c[:, Q_LORA_RANK:Q_LORA_RANK + KV_LORA_RANK]
    k_r = c[:, _WC_WIDTH - 2 * HEAD_LANES:_WC_WIDTH - HEAD_LANES]
    k_r_rot = c[:, _WC_WIDTH - HEAD_LANES:]
    qn = _rms_norm(c_q, gq_ref[...]).astype(BF16)
    kvn = _rms_norm(c_kv, gkv_ref[...]).astype(BF16)
    cos_q = tab_ref[:, 0 * HEAD_LANES:1 * HEAD_LANES]
    sin_q = tab_ref[:, 1 * HEAD_LANES:2 * HEAD_LANES]
    cos_k = tab_ref[:, 2 * HEAD_LANES:3 * HEAD_LANES]
    sin_k = tab_ref[:, 3 * HEAD_LANES:4 * HEAD_LANES]
    k_rope = k_r * cos_k + k_r_rot * sin_k
    q_a = _dot(qn, wqa_ref[...])
    q_b = _dot(qn, wqb_ref[...])
    k_nope = _dot(kvn, wuk_ref[...])
    for h in range(HEADS):
        sl = slice(h * HEAD_LANES, (h + 1) * HEAD_LANES)
        q_ref[:, sl] = (q_a[:, sl] * cos_q + q_b[:, sl] * sin_q).astype(BF16)
        k_ref[:, sl] = (k_nope[:, sl] + k_rope).astype(BF16)
    v = _dot(kvn, wuv_ref[...]) + vones_ref[...]
    if transpose_v:
        for t in range(x_ref.shape[0] // ATT_TILE):
            v_ref[t] = v[t * ATT_TILE:(t + 1) * ATT_TILE, :].T.astype(BF16)
    else:
        v_ref[...] = v.astype(BF16)


def _qkv(xn, wc, gq, gkv, wqa, wqb, wuk, wuv, vones, tab):
    n_seq, rows, _ = xn.shape
    tile = min(rows, TOK_TILE)
    transpose_v = tile % ATT_TILE == 0
    seq = lambda w: pl.BlockSpec((None, tile, w), lambda s, j: (s, j, 0))
    seq_shape = jax.ShapeDtypeStruct((n_seq, rows, QKV_WIDTH), BF16)
    if transpose_v:
        v_spec = pl.BlockSpec((None, tile // ATT_TILE, HEADS * V_ROWS, ATT_TILE), lambda s, j: (s, j, 0, 0))
        v_shape = jax.ShapeDtypeStruct((n_seq, rows // ATT_TILE, HEADS * V_ROWS, ATT_TILE), BF16)
    else:
        v_spec = seq(HEADS * V_ROWS)
        v_shape = jax.ShapeDtypeStruct((n_seq, rows, HEADS * V_ROWS), BF16)
    return pl.pallas_call(
        functools.partial(_qkv_kernel, transpose_v=transpose_v),
        grid=(n_seq, rows // tile),
        in_specs=[
            seq(D_MODEL),
            _const_spec((D_MODEL, _WC_WIDTH)),
            _const_spec((1, Q_LORA_RANK)),
            _const_spec((1, KV_LORA_RANK)),
            _const_spec((Q_LORA_RANK, QKV_WIDTH)),
            _const_spec((Q_LORA_RANK, QKV_WIDTH)),
            _const_spec((KV_LORA_RANK, QKV_WIDTH)),
            _const_spec((KV_LORA_RANK, HEADS * V_ROWS)),
            _const_spec((1, HEADS * V_ROWS)),
            pl.BlockSpec((tile, 4 * HEAD_LANES), lambda s, j: (j, 0)),
        ],
        out_specs=[seq(QKV_WIDTH), seq(QKV_WIDTH), v_spec],
        out_shape=[seq_shape, seq_shape, v_shape],
        compiler_params=pltpu.CompilerParams(
            dimension_semantics=("arbitrary", "arbitrary"), vmem_limit_bytes=VMEM_LIMIT),
        name="qkv_proj",
    )(xn, wc, gq, gkv, wqa, wqb, wuk, wuv, vones, tab)


def _scores(q_ref, k_ref, s_scr, smax_scr, slot, c, mask):
    k0 = c * ATT_TILE if isinstance(c, int) else pl.multiple_of(c * ATT_TILE, ATT_TILE)
    for h in range(HEADS):
        sl = slice(h * HEAD_LANES, (h + 1) * HEAD_LANES)
        s_t = lax.dot_general(
            k_ref[pl.ds(k0, ATT_TILE), sl], q_ref[:, sl], (((1,), (1,)), ((), ())),
            preferred_element_type=F32)
        if mask is not None:
            s_t = jnp.where(mask, s_t, -1e30)
        s_scr[slot, h] = s_t
        smax_scr[slot, h] = jnp.max(s_t, axis=0, keepdims=True)


def _attend(v_ref, s_scr, smax_scr, m_scr, acc_scr, slot, c):
    for h in range(HEADS):
        m_old = m_scr[h]
        m_new = jnp.maximum(m_old, smax_scr[slot, h])
        p_t = jnp.exp2(s_scr[slot, h] - m_new).astype(BF16)
        pv = _dot(v_ref[c, h * V_ROWS:(h + 1) * V_ROWS, :], p_t)
        acc_scr[h] = jnp.exp2(m_old - m_new) * acc_scr[h] + pv
        m_scr[h] = m_new


def _scores_meta(q_ref, km_ref, sm_scr, m_scr, mask):
    for h in range(HEADS):
        sl = slice(h * HEAD_LANES, (h + 1) * HEAD_LANES)
        s_t = lax.dot_general(
            km_ref[:, sl], q_ref[:, sl], (((1,), (1,)), ((), ())), preferred_element_type=F32)
        s_t = jnp.where(mask, s_t, -1e30)
        sm_scr[h] = s_t
        m_scr[h] = jnp.max(s_t, axis=0, keepdims=True)


def _attend_meta(vm_ref, sm_scr, m_scr, acc_scr):
    for h in range(HEADS):
        p_t = jnp.exp2(sm_scr[h] - m_scr[h]).astype(BF16)
        acc_scr[h] = _dot(vm_ref[h * V_ROWS:(h + 1) * V_ROWS, :], p_t)


def _attn_out(acc_scr):
    o_t = jnp.concatenate(
        [acc_scr[h, 0:V_HEAD_DIM, :] / acc_scr[h, V_HEAD_DIM:V_HEAD_DIM + 1, :] for h in range(HEADS)],
        axis=0)
    return o_t.T.astype(BF16)


def _attn_kernel(q_ref, k_ref, v_ref, km_ref, vm_ref, o_ref, s_scr, smax_scr, sm_scr, m_scr, acc_scr):
    i = pl.program_id(1)
    causal = (lax.broadcasted_iota(jnp.int32, (ATT_TILE, ATT_TILE), 0)
              <= lax.broadcasted_iota(jnp.int32, (ATT_TILE, ATT_TILE), 1))
    meta_mask = lax.broadcasted_iota(jnp.int32, (BLOCK_Q, ATT_TILE), 0) < N_META

    scores = functools.partial(_scores, q_ref, k_ref, s_scr, smax_scr)
    attend = functools.partial(_attend, v_ref, s_scr, smax_scr, m_scr, acc_scr)
    _scores_meta(q_ref, km_ref, sm_scr, m_scr, meta_mask)
    scores(0, 0, causal | (i > 0))
    _attend_meta(vm_ref, sm_scr, m_scr, acc_scr)

    def two_tiles(pp, carry):
        p = 1 + 2 * pp
        scores(1, p, None)
        attend(0, p - 1)
        scores(0, p + 1, None)
        attend(1, p)
        return carry

    lax.fori_loop(0, (i - 1) // 2, two_tiles, 0)

    @pl.when(i == 0)
    def _():
        attend(0, 0)

    @pl.when(i % 2 == 1)
    def _():
        scores(1, i, causal)
        attend(0, i - 1)
        attend(1, i)

    @pl.when((i % 2 == 0) & (i > 0))
    def _():
        scores(1, i - 1, None)
        attend(0, i - 2)
        scores(0, i, causal)
        attend(1, i - 1)
        attend(0, i)

    o_ref[...] = _attn_out(acc_scr)


def _attention(q, k, v_t, k_meta, v_meta_t):
    return pl.pallas_call(
        _attn_kernel,
        grid=(BATCH, N_SEQ_TILES),
        in_specs=[
            pl.BlockSpec((None, ATT_TILE, QKV_WIDTH), lambda s, i: (s, i, 0)),
            pl.BlockSpec((None, SEQ, QKV_WIDTH), lambda s, i: (s, 0, 0)),
            pl.BlockSpec((None, N_SEQ_TILES, HEADS * V_ROWS, ATT_TILE), lambda s, i: (s, 0, 0, 0)),
            _const_spec((BLOCK_Q, QKV_WIDTH)),
            _const_spec((HEADS * V_ROWS, BLOCK_Q)),
        ],
        out_specs=pl.BlockSpec((None, ATT_TILE, MIX_WIDTH), lambda s, i: (s, i, 0)),
        out_shape=jax.ShapeDtypeStruct((BATCH, SEQ, MIX_WIDTH), BF16),
        scratch_shapes=[pltpu.VMEM((2, HEADS, ATT_TILE, ATT_TILE), F32),
                        pltpu.VMEM((2, HEADS, 1, ATT_TILE), F32),
                        pltpu.VMEM((HEADS, BLOCK_Q, ATT_TILE), F32),
                        pltpu.VMEM((HEADS, 1, ATT_TILE), F32),
                        pltpu.VMEM((HEADS, V_ROWS, ATT_TILE), F32)],
        compiler_params=pltpu.CompilerParams(
            dimension_semantics=("arbitrary", "arbitrary"), vmem_limit_bytes=VMEM_LIMIT),
        name="mla_attention",
    )(q, k, v_t, k_meta, v_meta_t)


def _attn_meta_kernel(q_ref, km_ref, vm_ref, o_ref, sm_scr, m_scr, acc_scr):
    k_off = lax.broadcasted_iota(jnp.int32, (BLOCK_Q, BLOCK_Q), 0)
    q_off = lax.broadcasted_iota(jnp.int32, (BLOCK_Q, BLOCK_Q), 1)
    _scores_meta(q_ref, km_ref, sm_scr, m_scr, (k_off < N_META) & (k_off <= q_off))
    _attend_meta(vm_ref, sm_scr, m_scr, acc_scr)
    o_ref[...] = _attn_out(acc_scr)


def _attention_meta(q_meta, k_meta, v_meta_t):
    return pl.pallas_call(
        _attn_meta_kernel,
        out_shape=jax.ShapeDtypeStruct((BLOCK_Q, MIX_WIDTH), BF16),
        scratch_shapes=[pltpu.VMEM((HEADS, BLOCK_Q, BLOCK_Q), F32),
                        pltpu.VMEM((HEADS, 1, BLOCK_Q), F32),
                        pltpu.VMEM((HEADS, V_ROWS, BLOCK_Q), F32)],
        name="mla_attention_meta",
    )(q_meta, k_meta, v_meta_t)


_HALO = 2 * BATCH
_D_SLABS = D_MODEL // HEAD_LANES


def _to_time_major(src_ref, slab_ref, width, steps):
    n = width // HEAD_LANES
    for b in range(BATCH):
        for j in range(n):
            slab_ref[j, pl.ds(b, steps, stride=BATCH), :] = (
                src_ref[b, :, j * HEAD_LANES:(j + 1) * HEAD_LANES].astype(F32))
    return jnp.concatenate([slab_ref[j] for j in range(n)], axis=1)


def _mix_kernel(x_ref, att_ref, st0_ref, halo0_ref, win_ref, cw_ref, cb_ref, cwo_ref, bblk_ref, ar_ref,
                ai_ref, cblk_ref, d_ref, wglu_ref, bglu_ref, swo_ref, mwo_ref, wo_ref, g_ref, b_ref,
                out_ref, st_out_ref, halo_out_ref, bu_scr, st_scr, cbuf_scr, slab_ref, *, steps):
    tm = steps * BATCH
    W = MIX_WIDTH

    @pl.when(pl.program_id(0) == 0)
    def _():
        st_scr[...] = st0_ref[...]
        cbuf_scr[0:_HALO, :] = halo0_ref[...]

    x = _to_time_major(x_ref, slab_ref, D_MODEL, steps)
    xb = x.astype(BF16)

    us = _dot(xb, win_ref[:, 3 * W:4 * W])
    usb = us.astype(BF16)
    nb = S5_BLOCK_STATES
    for blk in range(S5_BLOCKS):
        bu_scr[:, 2 * nb * blk:2 * nb * (blk + 1)] = _dot(
            usb[:, blk * 128:(blk + 1) * 128], bblk_ref[blk])
    for blk in range(S5_BLOCKS):
        c_re = 2 * nb * blk
        c_im = c_re + nb
        a_re = ar_ref[blk]
        a_im = ai_ref[blk]

        def step(t, carry, c_re=c_re, c_im=c_im, a_re=a_re, a_im=a_im):
            s_re, s_im = carry
            r0 = pl.multiple_of(t * BATCH, BATCH)
            n_re = a_re * s_re - a_im * s_im + bu_scr[pl.ds(r0, BATCH), c_re:c_re + nb]
            n_im = a_re * s_im + a_im * s_re + bu_scr[pl.ds(r0, BATCH), c_im:c_im + nb]
            bu_scr[pl.ds(r0, BATCH), c_re:c_re + nb] = n_re
            bu_scr[pl.ds(r0, BATCH), c_im:c_im + nb] = n_im
            return n_re, n_im

        s_re, s_im = lax.fori_loop(
            0, steps, step, (st_scr[:, c_re:c_re + nb], st_scr[:, c_im:c_im + nb]), unroll=True)
        st_scr[:, c_re:c_re + nb] = s_re
        st_scr[:, c_im:c_im + nb] = s_im

    pc = _dot(xb, win_ref[:, 0:3 * W])
    u = pc[:, 2 * W:3 * W] * pc[:, 0:W]
    cbuf_scr[_HALO:_HALO + tm, :] = u
    y = (cb_ref[...] + cw_ref[0:1, :] * cbuf_scr[0:tm, :]
         + cw_ref[1:2, :] * cbuf_scr[BATCH:BATCH + tm, :] + cw_ref[2:3, :] * u)
    cbuf_scr[0:_HALO, :] = u[tm - _HALO:, :]
    y_b = _dot((pc[:, W:2 * W] * y).astype(BF16), cwo_ref[...])

    att = _to_time_major(att_ref, slab_ref, W, steps).astype(BF16)
    y_a = _dot(att, mwo_ref[...])
    g0 = 4 * W
    mixed = _sigmoid(_dot(xb, win_ref[:, g0:g0 + D_MODEL])) * y_a
    mixed += _sigmoid(_dot(xb, win_ref[:, g0 + D_MODEL:g0 + 2 * D_MODEL])) * y_b
    gate_c = _sigmoid(_dot(xb, win_ref[:, g0 + 2 * D_MODEL:g0 + 3 * D_MODEL]))

    y = jnp.concatenate(
        [_dot(bu_scr[:, 2 * nb * blk:2 * nb * (blk + 1)].astype(BF16), cblk_ref[blk])
         for blk in range(S5_BLOCKS)], axis=1)
    y = _gelu_tanh(y + d_ref[...] * us)
    y = y * _sigmoid(_dot(y.astype(BF16), wglu_ref[...]) + bglu_ref[...])
    mixed += gate_c * _dot(y.astype(BF16), swo_ref[...])

    half = tm // 2
    for r in range(2):
        rows = slice(r * half, (r + 1) * half)
        z = _layer_norm(ALPHA * x[rows] + _dot(mixed[rows].astype(BF16), wo_ref[...]),
                        g_ref[...], b_ref[...])
        for j in range(_D_SLABS):
            slab_ref[j, rows, :] = z[:, j * HEAD_LANES:(j + 1) * HEAD_LANES]
    for b in range(BATCH):
        for j in range(_D_SLABS):
            out_ref[b, :, j * HEAD_LANES:(j + 1) * HEAD_LANES] = slab_ref[
                j, pl.ds(b, steps, stride=BATCH), :]
    st_out_ref[...] = st_scr[...]
    halo_out_ref[...] = cbuf_scr[0:_HALO, :]


def _mix(xn, att, st0, halo0, win, cw, cb, cwo, bblk, ar, ai, cblk, d, wglu, bglu, swo, mwo, wo, g, b):
    consts = (st0, halo0, win, cw, cb, cwo, bblk, ar, ai, cblk, d, wglu, bglu, swo, mwo, wo, g, b)
    t_len = xn.shape[1]
    steps = min(t_len, MIX_STEPS)
    tm = steps * BATCH
    tile = lambda w: pl.BlockSpec((BATCH, steps, w), lambda i: (0, i, 0))
    return pl.pallas_call(
        functools.partial(_mix_kernel, steps=steps),
        grid=(t_len // steps,),
        in_specs=[tile(D_MODEL), tile(MIX_WIDTH)] + [_const_spec(c.shape) for c in consts],
        out_specs=[tile(D_MODEL), pl.BlockSpec((BATCH, S5_COLS), lambda i: (0, 0)),
                   pl.BlockSpec((_HALO, MIX_WIDTH), lambda i: (0, 0))],
        out_shape=[jax.ShapeDtypeStruct((BATCH, t_len, D_MODEL), F32),
                   jax.ShapeDtypeStruct((BATCH, S5_COLS), F32),
                   jax.ShapeDtypeStruct((_HALO, MIX_WIDTH), F32)],
        scratch_shapes=[
            pltpu.VMEM((tm, S5_COLS), F32),
            pltpu.VMEM((BATCH, S5_COLS), F32),
            pltpu.VMEM((tm + _HALO, MIX_WIDTH), F32),
            pltpu.VMEM((_D_SLABS, tm, HEAD_LANES), F32),
        ],
        compiler_params=pltpu.CompilerParams(
            dimension_semantics=("arbitrary",), vmem_limit_bytes=VMEM_LIMIT),
        name="mixers_merge",
    )(xn, att, *consts)


def _rope_tables(pos):
    n = pos.shape[0]
    inv_freq = ROPE_BASE ** (-jnp.arange(0, QK_ROPE_DIM, 2, dtype=F32) / QK_ROPE_DIM)
    ang = pos.astype(F32)[:, None] * inv_freq[None, :]
    cos2 = jnp.tile(jnp.cos(ang), (1, 2))
    sin2 = jnp.tile(jnp.sin(ang), (1, 2))
    zn = jnp.zeros((n, QK_NOPE_DIM), F32)
    zt = jnp.zeros((n, HEAD_LANES - QK_HEAD_DIM), F32)
    scale = QK_HEAD_DIM ** -0.5 * LOG2E
    cos_q = scale * jnp.concatenate([jnp.ones_like(zn), cos2, zt], axis=1)
    sin_q = scale * jnp.concatenate([zn, sin2, zt], axis=1)
    cos_k = jnp.concatenate([zn, cos2, zt], axis=1)
    sin_k = jnp.concatenate([zn, sin2, zt], axis=1)
    return jnp.concatenate([cos_q, sin_q, cos_k, sin_k], axis=1)


def _rot_half_cols(w):
    half = QK_ROPE_DIM // 2
    return jnp.concatenate([-w[..., half:], w[..., :half]], axis=-1)


def _qkv_weights(w_in, w_uq, w_ukv):
    zpad = lambda n: jnp.zeros((D_MODEL, n), F32)
    kr = w_in[:, _OFF_KR:_OFF_CONV]
    tail = HEAD_LANES - QK_HEAD_DIM
    wc = jnp.concatenate([
        w_in[:, :_OFF_KR],
        zpad(QK_NOPE_DIM), kr, zpad(tail),
        zpad(QK_NOPE_DIM), _rot_half_cols(kr), zpad(tail)], axis=1)
    uq = w_uq.reshape(Q_LORA_RANK, HEADS, QK_HEAD_DIM)
    zq = lambda n: jnp.zeros((Q_LORA_RANK, HEADS, n), F32)
    wqa = jnp.concatenate([uq, zq(tail)], axis=-1)
    wqb = jnp.concatenate([zq(QK_NOPE_DIM), _rot_half_cols(uq[..., QK_NOPE_DIM:]), zq(tail)], axis=-1)
    ukv = w_ukv.reshape(KV_LORA_RANK, HEADS, QK_NOPE_DIM + V_HEAD_DIM)
    zk = jnp.zeros((KV_LORA_RANK, HEADS, HEAD_LANES - QK_NOPE_DIM), F32)
    wuk = jnp.concatenate([ukv[..., :QK_NOPE_DIM], zk], axis=-1)
    wuv = jnp.concatenate(
        [ukv[..., QK_NOPE_DIM:], jnp.zeros((KV_LORA_RANK, HEADS, V_ROWS - V_HEAD_DIM), F32)], axis=-1)
    vones = jnp.zeros((HEADS, V_ROWS), F32).at[:, V_HEAD_DIM].set(1.0).reshape(1, HEADS * V_ROWS)
    flat = lambda w: w.reshape(w.shape[0], -1).astype(BF16)
    return wc.astype(BF16), flat(wqa), flat(wqb), flat(wuk), flat(wuv), vones


def _s5_weights(a_re, a_im, log_dt, b_re, b_im, c_re, c_im):
    dt = jnp.exp(log_dt)[:, None]
    mag = jnp.exp(dt * a_re)
    ab_re, ab_im = mag * jnp.cos(dt * a_im), mag * jnp.sin(dt * a_im)
    den = a_re * a_re + a_im * a_im
    nr, ni = ab_re - 1.0, ab_im
    coef_re = (nr * a_re + ni * a_im) / den
    coef_im = (ni * a_re - nr * a_im) / den
    bb_re = coef_re[..., None] * b_re - coef_im[..., None] * b_im
    bb_im = coef_re[..., None] * b_im + coef_im[..., None] * b_re
    gpb = S5_GROUPS // S5_BLOCKS
    eye = jnp.eye(gpb, dtype=F32)

    def in_blocks(bb):
        t = bb.transpose(0, 2, 1).reshape(S5_BLOCKS, gpb, S5_GROUP, S5_STATE)
        return jnp.einsum('bghn,gk->bghkn', t, eye).reshape(S5_BLOCKS, gpb * S5_GROUP, gpb * S5_STATE)

    def out_blocks(cc):
        t = cc.reshape(S5_BLOCKS, gpb, S5_GROUP, S5_STATE)
        return jnp.einsum('bghn,gk->bgnkh', t, eye).reshape(S5_BLOCKS, gpb * S5_STATE, gpb * S5_GROUP)

    bblk = jnp.concatenate([in_blocks(bb_re), in_blocks(bb_im)], axis=2).astype(BF16)
    cblk = jnp.concatenate([out_blocks(c_re), out_blocks(-c_im)], axis=1).astype(BF16)
    bcast = lambda a: jnp.broadcast_to(
        a.reshape(S5_BLOCKS, 1, S5_BLOCK_STATES), (S5_BLOCKS, BATCH, S5_BLOCK_STATES))
    return bblk, bcast(ab_re), bcast(ab_im), cblk


def kernel(x, meta, ffn1_w_gate, ffn1_w_up, ffn1_w_down, ln1_g, ln1_b, w_in, mla_q_norm_g, mla_w_uq, mla_kv_norm_g, mla_w_ukv, mla_w_o, conv_w, conv_b, conv_w_out, s5_a_re, s5_a_im, s5_log_dt, s5_b_re, s5_b_im, s5_c_re, s5_c_im, s5_d, s5_w_glu, s5_b_glu, s5_w_out, w_o, ln2_g, ln2_b, ffn2_w_gate, ffn2_w_up, ffn2_w_down, ln3_g, ln3_b):
    row = lambda v: v.reshape(1, -1).astype(F32)
    bf = lambda w: w.astype(BF16)
    pad_rows = lambda a: jnp.pad(a, ((0, BLOCK_Q - N_META), (0, 0)))
    per_batch = lambda a: jnp.broadcast_to(a[None], (BATCH,) + a.shape)
    h = x.astype(F32).reshape(BATCH * SEQ, D_MODEL)
    hm = meta.astype(F32)
    tab = _rope_tables(N_META + jnp.arange(SEQ))
    tab_m = _rope_tables(jnp.arange(N_META))
    for i in range(DEPTH):
        ffn1 = (bf(ffn1_w_gate[i]), bf(ffn1_w_up[i]), bf(ffn1_w_down[i]), row(ln1_g[i]), row(ln1_b[i]))
        ffn2 = (bf(ffn2_w_gate[i]), bf(ffn2_w_up[i]), bf(ffn2_w_down[i]), row(ln3_g[i]), row(ln3_b[i]))
        wc, wqa, wqb, wuk, wuv, vones = _qkv_weights(w_in[i], mla_w_uq[i], mla_w_ukv[i])
        qkv_w = (wc, row(mla_q_norm_g[i]), row(mla_kv_norm_g[i]), wqa, wqb, wuk, wuv, vones)
        bblk, ar, ai, cblk = _s5_weights(s5_a_re[i], s5_a_im[i], s5_log_dt[i], s5_b_re[i], s5_b_im[i],
                                         s5_c_re[i], s5_c_im[i])
        mix_w = (bf(w_in[i][:, _OFF_CONV:]), conv_w[i].astype(F32), row(conv_b[i]), bf(conv_w_out[i]),
                 bblk, ar, ai, cblk, row(s5_d[i]), bf(s5_w_glu[i]), row(s5_b_glu[i]), bf(s5_w_out[i]),
                 bf(mla_w_o[i]), bf(w_o[i]), row(ln2_g[i]), row(ln2_b[i]))

        xm = _ffn_ln(hm, *ffn1)
        qm, km, vm = _qkv(xm[None], *qkv_w, tab_m)
        km, vm_t = pad_rows(km[0]), pad_rows(vm[0]).T
        att_m = _attention_meta(pad_rows(qm[0]), km, vm_t)[:N_META]
        hm8, state, halo = _mix(per_batch(xm), per_batch(att_m), jnp.zeros((BATCH, S5_COLS), F32),
                                jnp.zeros((_HALO, MIX_WIDTH), F32), *mix_w)

        xn = _ffn_ln(h, *ffn1)
        xn3 = xn.reshape(BATCH, SEQ, D_MODEL)
        q, k, v = _qkv(xn3, *qkv_w, tab)
        att = _attention(q, k, v, km, vm_t)
        h3, _, _ = _mix(xn3, att, state, halo, *mix_w)
        h = _ffn_ln(h3.reshape(BATCH * SEQ, D_MODEL), *ffn2)
        if i + 1 < DEPTH:
            hm = _ffn_ln(hm8[0], *ffn2)
    return h.reshape(BATCH, SEQ, D_MODEL)
```

```python
import functools
import math

import jax
import jax.numpy as jnp
from jax import lax
from jax.experimental import pallas as pl
from jax.experimental.pallas import tpu as pltpu

D_MODEL = 1024
BATCH = 8
SEQ = 2048
DEPTH = 2
N_META = 16
BLOCK_Q = 128
MIX_WIDTH = D_MODEL // 2
HEADS = 8
V_HEAD_DIM = 64
QK_NOPE_DIM = 64
QK_ROPE_DIM = 32
QK_HEAD_DIM = QK_NOPE_DIM + QK_ROPE_DIM
Q_LORA_RANK = 384
KV_LORA_RANK = 256
ROPE_BASE = 10000.0
S5_GROUP = 16
S5_GROUPS = 32
S5_STATE = 64
D_FF = 2816
ALPHA = (2.0 * DEPTH) ** 0.25
LN_EPS = 1e-5
RMS_EPS = 1e-6

HEAD_LANES = 128
QKV_WIDTH = HEADS * HEAD_LANES
TOK_TILE = 512
ATT_TILE = 2 * BLOCK_Q
N_SEQ_TILES = SEQ // ATT_TILE
V_ROWS = V_HEAD_DIM + 16
LOG2E = math.log2(math.e)
S5_BLOCKS = 4
S5_BLOCK_STATES = (S5_GROUPS // S5_BLOCKS) * S5_STATE
S5_COLS = 2 * S5_GROUPS * S5_STATE
MIX_STEPS = 64

_OFF_CQ, _OFF_CKV, _OFF_KR, _OFF_CONV = 0, 384, 640, 672
_OFF_GATES = _OFF_CONV + 4 * MIX_WIDTH
D_IN = _OFF_GATES + 3 * D_MODEL

V7X_VMEM_BYTES = 64 * 1024 * 1024
VMEM_LIMIT = 56 * 1024 * 1024

F32 = jnp.float32
BF16 = jnp.bfloat16


def _dot(a, b):
    return jnp.dot(a, b, preferred_element_type=F32)


def _const_spec(shape):
    nd = len(shape)
    return pl.BlockSpec(shape, lambda *_: (0,) * nd, pipeline_mode=pl.Buffered(1))


def _layer_spec(stacked, layer):
    nd = stacked.ndim - 1
    return pl.BlockSpec((None,) + stacked.shape[1:], lambda *_: (layer,) + (0,) * nd,
                        pipeline_mode=pl.Buffered(1))


def _layer_norm(y, g, b):
    mu = jnp.mean(y, axis=-1, keepdims=True)
    yc = y - mu
    var = jnp.mean(yc * yc, axis=-1, keepdims=True)
    return yc * lax.rsqrt(var + LN_EPS) * g + b


def _rms_norm(y, g):
    return y * lax.rsqrt(jnp.mean(y * y, axis=-1, keepdims=True) + RMS_EPS) * g


def _sigmoid(y):
    return 1.0 / (1.0 + jnp.exp(-y))


def _gelu_tanh(y):
    return 0.5 * y * (1.0 + jnp.tanh(math.sqrt(2.0 / math.pi) * (y + 0.044715 * (y * y * y))))


def _ffn_ln_kernel(x_ref, wg_ref, wu_ref, wd_ref, g_ref, b_ref, o_ref, *, ff_chunk, sub_rows):
    for r in range(x_ref.shape[0] // sub_rows):
        rows = slice(r * sub_rows, (r + 1) * sub_rows)
        x = x_ref[rows, :]
        xb = x.astype(BF16)
        acc = None
        for c in range(D_FF // ff_chunk):
            sl = slice(c * ff_chunk, (c + 1) * ff_chunk)
            gate = _dot(xb, wg_ref[:, sl])
            up = _dot(xb, wu_ref[:, sl])
            hmid = (gate * _sigmoid(gate) * up).astype(BF16)
            part = _dot(hmid, wd_ref[sl, :])
            acc = part if acc is None else acc + part
        o_ref[rows, :] = _layer_norm(ALPHA * x + 0.5 * acc, g_ref[...], b_ref[...])


def _ffn_ln(x, layer, wg, wu, wd, g, b, *, ff_chunk=1408):
    rows = x.shape[0]
    tm = min(rows, 2 * TOK_TILE)
    flat = pl.BlockSpec((tm, D_MODEL), lambda i: (i, 0))
    return pl.pallas_call(
        functools.partial(_ffn_ln_kernel, ff_chunk=ff_chunk, sub_rows=min(tm, TOK_TILE)),
        grid=(rows // tm,),
        in_specs=[flat] + [_layer_spec(w, layer) for w in (wg, wu, wd, g, b)],
        out_specs=flat,
        out_shape=jax.ShapeDtypeStruct((rows, D_MODEL), F32),
        compiler_params=pltpu.CompilerParams(
            dimension_semantics=("arbitrary",), vmem_limit_bytes=VMEM_LIMIT),
        name="ffn_ln",
    )(x, wg, wu, wd, g, b)


_WC_WIDTH = Q_LORA_RANK + KV_LORA_RANK + 2 * HEAD_LANES


def _qkv_kernel(x_ref, wc_ref, gq_ref, gkv_ref, wqa_ref, wqb_ref, wuk_ref, wuv_ref, vones_ref, tab_ref,
                q_ref, k_ref, v_ref, *, transpose_v):
    xb = x_ref[...].astype(BF16)
    c = _dot(xb, wc_ref[...])
    c_q = c[:, :Q_LORA_RANK]
    c_kv = c[:, Q_LORA_RANK:Q_LORA_RANK + KV_LORA_RANK]
    k_r = c[:, _WC_WIDTH - 2 * HEAD_LANES:_WC_WIDTH - HEAD_LANES]
    k_r_rot = c[:, _WC_WIDTH - HEAD_LANES:]
    qn = _rms_norm(c_q, gq_ref[...]).astype(BF16)
    kvn = _rms_norm(c_kv, gkv_ref[...]).astype(BF16)
    cos_q = tab_ref[:, 0 * HEAD_LANES:1 * HEAD_LANES]
    sin_q = tab_ref[:, 1 * HEAD_LANES:2 * HEAD_LANES]
    cos_k = tab_ref[:, 2 * HEAD_LANES:3 * HEAD_LANES]
    sin_k = tab_ref[:, 3 * HEAD_LANES:4 * HEAD_LANES]
    k_rope = k_r * cos_k + k_r_rot * sin_k
    q_a = _dot(qn, wqa_ref[...])
    q_b = _dot(qn, wqb_ref[...])
    k_nope = _dot(kvn, wuk_ref[...])
    for h in range(HEADS):
        sl = slice(h * HEAD_LANES, (h + 1) * HEAD_LANES)
        q_ref[:, sl] = (q_a[:, sl] * cos_q + q_b[:, sl] * sin_q).astype(BF16)
        k_ref[:, sl] = (k_nope[:, sl] + k_rope).astype(BF16)
    v = _dot(kvn, wuv_ref[...]) + vones_ref[...]
    if transpose_v:
        for t in range(x_ref.shape[0] // ATT_TILE):
            v_ref[t] = v[t * ATT_TILE:(t + 1) * ATT_TILE, :].T.astype(BF16)
    else:
        v_ref[...] = v.astype(BF16)


def _qkv(xn, layer, wc, gq, gkv, wqa, wqb, wuk, wuv, vones, tab):
    n_seq, rows, _ = xn.shape
    tile = min(rows, TOK_TILE)
    transpose_v = tile % ATT_TILE == 0
    seq = lambda w: pl.BlockSpec((None, tile, w), lambda s, j: (s, j, 0))
    seq_shape = jax.ShapeDtypeStruct((n_seq, rows, QKV_WIDTH), BF16)
    if transpose_v:
        v_spec = pl.BlockSpec((None, tile // ATT_TILE, HEADS * V_ROWS, ATT_TILE), lambda s, j: (s, j, 0, 0))
        v_shape = jax.ShapeDtypeStruct((n_seq, rows // ATT_TILE, HEADS * V_ROWS, ATT_TILE), BF16)
    else:
        v_spec = seq(HEADS * V_ROWS)
        v_shape = jax.ShapeDtypeStruct((n_seq, rows, HEADS * V_ROWS), BF16)
    return pl.pallas_call(
        functools.partial(_qkv_kernel, transpose_v=transpose_v),
        grid=(n_seq, rows // tile),
        in_specs=[seq(D_MODEL)]
        + [_layer_spec(w, layer) for w in (wc, gq, gkv, wqa, wqb, wuk, wuv, vones)]
        + [pl.BlockSpec((tile, 4 * HEAD_LANES), lambda s, j: (j, 0))],
        out_specs=[seq(QKV_WIDTH), seq(QKV_WIDTH), v_spec],
        out_shape=[seq_shape, seq_shape, v_shape],
        compiler_params=pltpu.CompilerParams(
            dimension_semantics=("arbitrary", "arbitrary"), vmem_limit_bytes=VMEM_LIMIT),
        name="qkv_proj",
    )(xn, wc, gq, gkv, wqa, wqb, wuk, wuv, vones, tab)


def _scores(q_ref, k_ref, s_scr, smax_scr, slot, c, mask):
    k0 = c * ATT_TILE if isinstance(c, int) else pl.multiple_of(c * ATT_TILE, ATT_TILE)
    for h in range(HEADS):
        sl = slice(h * HEAD_LANES, (h + 1) * HEAD_LANES)
        s_t = lax.dot_general(
            k_ref[pl.ds(k0, ATT_TILE), sl], q_ref[:, sl], (((1,), (1,)), ((), ())),
            preferred_element_type=F32)
        if mask is not None:
            s_t = jnp.where(mask, s_t, -1e30)
        s_scr[slot, h] = s_t
        smax_scr[slot, h] = jnp.max(s_t, axis=0, keepdims=True)


def _attend(v_ref, s_scr, smax_scr, m_scr, acc_scr, slot, c):
    for h in range(HEADS):
        m_old = m_scr[h]
        m_new = jnp.maximum(m_old, smax_scr[slot, h])
        p_t = jnp.exp2(s_scr[slot, h] - m_new).astype(BF16)
        pv = _dot(v_ref[c, h * V_ROWS:(h + 1) * V_ROWS, :], p_t)
        acc_scr[h] = jnp.exp2(m_old - m_new) * acc_scr[h] + pv
        m_scr[h] = m_new


def _scores_meta(q_ref, km_ref, sm_scr, m_scr, mask):
    for h in range(HEADS):
        sl = slice(h * HEAD_LANES, (h + 1) * HEAD_LANES)
        s_t = lax.dot_general(
            km_ref[:, sl], q_ref[:, sl], (((1,), (1,)), ((), ())), preferred_element_type=F32)
        s_t = jnp.where(mask, s_t, -1e30)
        sm_scr[h] = s_t
        m_scr[h] = jnp.max(s_t, axis=0, keepdims=True)


def _attend_meta(vm_ref, sm_scr, m_scr, acc_scr):
    for h in range(HEADS):
        p_t = jnp.exp2(sm_scr[h] - m_scr[h]).astype(BF16)
        acc_scr[h] = _dot(vm_ref[h * V_ROWS:(h + 1) * V_ROWS, :], p_t)


def _attn_out(acc_scr):
    o_t = jnp.concatenate(
        [acc_scr[h, 0:V_HEAD_DIM, :] / acc_scr[h, V_HEAD_DIM:V_HEAD_DIM + 1, :] for h in range(HEADS)],
        axis=0)
    return o_t.T.astype(BF16)


def _attn_kernel(q_ref, k_ref, v_ref, km_ref, vm_ref, o_ref, s_scr, smax_scr, sm_scr, m_scr, acc_scr):
    i = pl.program_id(1)
    causal = (lax.broadcasted_iota(jnp.int32, (ATT_TILE, ATT_TILE), 0)
              <= lax.broadcasted_iota(jnp.int32, (ATT_TILE, ATT_TILE), 1))
    meta_mask = lax.broadcasted_iota(jnp.int32, (BLOCK_Q, ATT_TILE), 0) < N_META

    scores = functools.partial(_scores, q_ref, k_ref, s_scr, smax_scr)
    attend = functools.partial(_attend, v_ref, s_scr, smax_scr, m_scr, acc_scr)
    _scores_meta(q_ref, km_ref, sm_scr, m_scr, meta_mask)
    scores(0, 0, causal | (i > 0))
    _attend_meta(vm_ref, sm_scr, m_scr, acc_scr)

    def two_tiles(pp, carry):
        p = 1 + 2 * pp
        scores(1, p, None)
        attend(0, p - 1)
        scores(0, p + 1, None)
        attend(1, p)
        return carry

    lax.fori_loop(0, (i - 1) // 2, two_tiles, 0)

    @pl.when(i == 0)
    def _():
        attend(0, 0)

    @pl.when(i % 2 == 1)
    def _():
        scores(1, i, causal)
        attend(0, i - 1)
        attend(1, i)

    @pl.when((i % 2 == 0) & (i > 0))
    def _():
        scores(1, i - 1, None)
        attend(0, i - 2)
        scores(0, i, causal)
        attend(1, i - 1)
        attend(0, i)

    o_ref[...] = _attn_out(acc_scr)


def _attention(q, k, v_t, k_meta, v_meta_t):
    return pl.pallas_call(
        _attn_kernel,
        grid=(BATCH, N_SEQ_TILES),
        in_specs=[
            pl.BlockSpec((None, ATT_TILE, QKV_WIDTH), lambda s, i: (s, i, 0)),
            pl.BlockSpec((None, SEQ, QKV_WIDTH), lambda s, i: (s, 0, 0)),
            pl.BlockSpec((None, N_SEQ_TILES, HEADS * V_ROWS, ATT_TILE), lambda s, i: (s, 0, 0, 0)),
            _const_spec((BLOCK_Q, QKV_WIDTH)),
            _const_spec((HEADS * V_ROWS, BLOCK_Q)),
        ],
        out_specs=pl.BlockSpec((None, ATT_TILE, MIX_WIDTH), lambda s, i: (s, i, 0)),
        out_shape=jax.ShapeDtypeStruct((BATCH, SEQ, MIX_WIDTH), BF16),
        scratch_shapes=[pltpu.VMEM((2, HEADS, ATT_TILE, ATT_TILE), F32),
                        pltpu.VMEM((2, HEADS, 1, ATT_TILE), F32),
                        pltpu.VMEM((HEADS, BLOCK_Q, ATT_TILE), F32),
                        pltpu.VMEM((HEADS, 1, ATT_TILE), F32),
                        pltpu.VMEM((HEADS, V_ROWS, ATT_TILE), F32)],
        compiler_params=pltpu.CompilerParams(
            dimension_semantics=("arbitrary", "arbitrary"), vmem_limit_bytes=VMEM_LIMIT),
        name="mla_attention",
    )(q, k, v_t, k_meta, v_meta_t)


def _attn_meta_kernel(q_ref, km_ref, vm_ref, o_ref, sm_scr, m_scr, acc_scr):
    k_off = lax.broadcasted_iota(jnp.int32, (BLOCK_Q, BLOCK_Q), 0)
    q_off = lax.broadcasted_iota(jnp.int32, (BLOCK_Q, BLOCK_Q), 1)
    _scores_meta(q_ref, km_ref, sm_scr, m_scr, (k_off < N_META) & (k_off <= q_off))
    _attend_meta(vm_ref, sm_scr, m_scr, acc_scr)
    o_ref[...] = _attn_out(acc_scr)


def _attention_meta(q_meta, k_meta, v_meta_t):
    return pl.pallas_call(
        _attn_meta_kernel,
        out_shape=jax.ShapeDtypeStruct((BLOCK_Q, MIX_WIDTH), BF16),
        scratch_shapes=[pltpu.VMEM((HEADS, BLOCK_Q, BLOCK_Q), F32),
                        pltpu.VMEM((HEADS, 1, BLOCK_Q), F32),
                        pltpu.VMEM((HEADS, V_ROWS, BLOCK_Q), F32)],
        name="mla_attention_meta",
    )(q_meta, k_meta, v_meta_t)


_HALO = 2 * BATCH
_D_SLABS = D_MODEL // HEAD_LANES


def _to_time_major(src_ref, slab_ref, width, steps):
    n = width // HEAD_LANES
    for b in range(BATCH):
        for j in range(n):
            slab_ref[j, pl.ds(b, steps, stride=BATCH), :] = (
                src_ref[b, :, j * HEAD_LANES:(j + 1) * HEAD_LANES].astype(F32))
    return jnp.concatenate([slab_ref[j] for j in range(n)], axis=1)


def _mix_kernel(x_ref, att_ref, st0_ref, halo0_ref, win_ref, cw_ref, cb_ref, cwo_ref, bblk_ref, ar_ref,
                ai_ref, cblk_ref, d_ref, wglu_ref, bglu_ref, swo_ref, mwo_ref, wo_ref, g_ref, b_ref,
                out_ref, st_out_ref, halo_out_ref, bu_scr, st_scr, cbuf_scr, slab_ref, *, steps):
    tm = steps * BATCH
    W = MIX_WIDTH

    @pl.when(pl.program_id(0) == 0)
    def _():
        st_scr[...] = st0_ref[...]
        cbuf_scr[0:_HALO, :] = halo0_ref[...]

    x = _to_time_major(x_ref, slab_ref, D_MODEL, steps)
    xb = x.astype(BF16)

    us = _dot(xb, win_ref[:, 3 * W:4 * W])
    usb = us.astype(BF16)
    nb = S5_BLOCK_STATES
    for blk in range(S5_BLOCKS):
        bu_scr[:, 2 * nb * blk:2 * nb * (blk + 1)] = _dot(
            usb[:, blk * 128:(blk + 1) * 128], bblk_ref[blk])
    for blk in range(S5_BLOCKS):
        c_re = 2 * nb * blk
        c_im = c_re + nb
        a_re = ar_ref[blk]
        a_im = ai_ref[blk]

        def step(t, carry, c_re=c_re, c_im=c_im, a_re=a_re, a_im=a_im):
            s_re, s_im = carry
            r0 = pl.multiple_of(t * BATCH, BATCH)
            n_re = a_re * s_re - a_im * s_im + bu_scr[pl.ds(r0, BATCH), c_re:c_re + nb]
            n_im = a_re * s_im + a_im * s_re + bu_scr[pl.ds(r0, BATCH), c_im:c_im + nb]
            bu_scr[pl.ds(r0, BATCH), c_re:c_re + nb] = n_re
            bu_scr[pl.ds(r0, BATCH), c_im:c_im + nb] = n_im
            return n_re, n_im

        s_re, s_im = lax.fori_loop(
            0, steps, step, (st_scr[:, c_re:c_re + nb], st_scr[:, c_im:c_im + nb]), unroll=True)
        st_scr[:, c_re:c_re + nb] = s_re
        st_scr[:, c_im:c_im + nb] = s_im

    pc = _dot(xb, win_ref[:, 0:3 * W])
    u = pc[:, 2 * W:3 * W] * pc[:, 0:W]
    cbuf_scr[_HALO:_HALO + tm, :] = u
    y = (cb_ref[...] + cw_ref[0:1, :] * cbuf_scr[0:tm, :]
         + cw_ref[1:2, :] * cbuf_scr[BATCH:BATCH + tm, :] + cw_ref[2:3, :] * u)
    cbuf_scr[0:_HALO, :] = u[tm - _HALO:, :]
    y_b = _dot((pc[:, W:2 * W] * y).astype(BF16), cwo_ref[...])

    att = _to_time_major(att_ref, slab_ref, W, steps).astype(BF16)
    y_a = _dot(att, mwo_ref[...])
    g0 = 4 * W
    mixed = _sigmoid(_dot(xb, win_ref[:, g0:g0 + D_MODEL])) * y_a
    mixed += _sigmoid(_dot(xb, win_ref[:, g0 + D_MODEL:g0 + 2 * D_MODEL])) * y_b
    gate_c = _sigmoid(_dot(xb, win_ref[:, g0 + 2 * D_MODEL:g0 + 3 * D_MODEL]))

    y = jnp.concatenate(
        [_dot(bu_scr[:, 2 * nb * blk:2 * nb * (blk + 1)].astype(BF16), cblk_ref[blk])
         for blk in range(S5_BLOCKS)], axis=1)
    y = _gelu_tanh(y + d_ref[...] * us)
    y = y * _sigmoid(_dot(y.astype(BF16), wglu_ref[...]) + bglu_ref[...])
    mixed += gate_c * _dot(y.astype(BF16), swo_ref[...])

    half = tm // 2
    for r in range(2):
        rows = slice(r * half, (r + 1) * half)
        z = _layer_norm(ALPHA * x[rows] + _dot(mixed[rows].astype(BF16), wo_ref[...]),
                        g_ref[...], b_ref[...])
        for j in range(_D_SLABS):
            slab_ref[j, rows, :] = z[:, j * HEAD_LANES:(j + 1) * HEAD_LANES]
    for b in range(BATCH):
        for j in range(_D_SLABS):
            out_ref[b, :, j * HEAD_LANES:(j + 1) * HEAD_LANES] = slab_ref[
                j, pl.ds(b, steps, stride=BATCH), :]
    st_out_ref[...] = st_scr[...]
    halo_out_ref[...] = cbuf_scr[0:_HALO, :]


def _mix(xn, att, st0, halo0, layer, *weights):
    t_len = xn.shape[1]
    steps = min(t_len, MIX_STEPS)
    tm = steps * BATCH
    tile = lambda w: pl.BlockSpec((BATCH, steps, w), lambda i: (0, i, 0))
    return pl.pallas_call(
        functools.partial(_mix_kernel, steps=steps),
        grid=(t_len // steps,),
        in_specs=[tile(D_MODEL), tile(MIX_WIDTH), _const_spec(st0.shape), _const_spec(halo0.shape)]
        + [_layer_spec(w, layer) for w in weights],
        out_specs=[tile(D_MODEL), pl.BlockSpec((BATCH, S5_COLS), lambda i: (0, 0)),
                   pl.BlockSpec((_HALO, MIX_WIDTH), lambda i: (0, 0))],
        out_shape=[jax.ShapeDtypeStruct((BATCH, t_len, D_MODEL), F32),
                   jax.ShapeDtypeStruct((BATCH, S5_COLS), F32),
                   jax.ShapeDtypeStruct((_HALO, MIX_WIDTH), F32)],
        scratch_shapes=[
            pltpu.VMEM((tm, S5_COLS), F32),
            pltpu.VMEM((BATCH, S5_COLS), F32),
            pltpu.VMEM((tm + _HALO, MIX_WIDTH), F32),
            pltpu.VMEM((_D_SLABS, tm, HEAD_LANES), F32),
        ],
        compiler_params=pltpu.CompilerParams(
            dimension_semantics=("arbitrary",), vmem_limit_bytes=VMEM_LIMIT),
        name="mixers_merge",
    )(xn, att, st0, halo0, *weights)


def _rope_tables(pos):
    n = pos.shape[0]
    inv_freq = ROPE_BASE ** (-jnp.arange(0, QK_ROPE_DIM, 2, dtype=F32) / QK_ROPE_DIM)
    ang = pos.astype(F32)[:, None] * inv_freq[None, :]
    cos2 = jnp.tile(jnp.cos(ang), (1, 2))
    sin2 = jnp.tile(jnp.sin(ang), (1, 2))
    zn = jnp.zeros((n, QK_NOPE_DIM), F32)
    zt = jnp.zeros((n, HEAD_LANES - QK_HEAD_DIM), F32)
    scale = QK_HEAD_DIM ** -0.5 * LOG2E
    cos_q = scale * jnp.concatenate([jnp.ones_like(zn), cos2, zt], axis=1)
    sin_q = scale * jnp.concatenate([zn, sin2, zt], axis=1)
    cos_k = jnp.concatenate([zn, cos2, zt], axis=1)
    sin_k = jnp.concatenate([zn, sin2, zt], axis=1)
    return jnp.concatenate([cos_q, sin_q, cos_k, sin_k], axis=1)


def _rot_half_cols(w):
    half = QK_ROPE_DIM // 2
    return jnp.concatenate([-w[..., half:], w[..., :half]], axis=-1)


def _qkv_weights(w_in, w_uq, w_ukv):
    zpad = lambda n: jnp.zeros((D_MODEL, n), F32)
    kr = w_in[:, _OFF_KR:_OFF_CONV]
    tail = HEAD_LANES - QK_HEAD_DIM
    wc = jnp.concatenate([
        w_in[:, :_OFF_KR],
        zpad(QK_NOPE_DIM), kr, zpad(tail),
        zpad(QK_NOPE_DIM), _rot_half_cols(kr), zpad(tail)], axis=1)
    uq = w_uq.reshape(Q_LORA_RANK, HEADS, QK_HEAD_DIM)
    zq = lambda n: jnp.zeros((Q_LORA_RANK, HEADS, n), F32)
    wqa = jnp.concatenate([uq, zq(tail)], axis=-1)
    wqb = jnp.concatenate([zq(QK_NOPE_DIM), _rot_half_cols(uq[..., QK_NOPE_DIM:]), zq(tail)], axis=-1)
    ukv = w_ukv.reshape(KV_LORA_RANK, HEADS, QK_NOPE_DIM + V_HEAD_DIM)
    zk = jnp.zeros((KV_LORA_RANK, HEADS, HEAD_LANES - QK_NOPE_DIM), F32)
    wuk = jnp.concatenate([ukv[..., :QK_NOPE_DIM], zk], axis=-1)
    wuv = jnp.concatenate(
        [ukv[..., QK_NOPE_DIM:], jnp.zeros((KV_LORA_RANK, HEADS, V_ROWS - V_HEAD_DIM), F32)], axis=-1)
    vones = jnp.zeros((HEADS, V_ROWS), F32).at[:, V_HEAD_DIM].set(1.0).reshape(1, HEADS * V_ROWS)
    flat = lambda w: w.reshape(w.shape[0], -1).astype(BF16)
    return wc.astype(BF16), flat(wqa), flat(wqb), flat(wuk), flat(wuv), vones


def _s5_weights(a_re, a_im, log_dt, b_re, b_im, c_re, c_im):
    dt = jnp.exp(log_dt)[:, None]
    mag = jnp.exp(dt * a_re)
    ab_re, ab_im = mag * jnp.cos(dt * a_im), mag * jnp.sin(dt * a_im)
    den = a_re * a_re + a_im * a_im
    nr, ni = ab_re - 1.0, ab_im
    coef_re = (nr * a_re + ni * a_im) / den
    coef_im = (ni * a_re - nr * a_im) / den
    bb_re = coef_re[..., None] * b_re - coef_im[..., None] * b_im
    bb_im = coef_re[..., None] * b_im + coef_im[..., None] * b_re
    gpb = S5_GROUPS // S5_BLOCKS
    eye = jnp.eye(gpb, dtype=F32)

    def in_blocks(bb):
        t = bb.transpose(0, 2, 1).reshape(S5_BLOCKS, gpb, S5_GROUP, S5_STATE)
        return jnp.einsum('bghn,gk->bghkn', t, eye).reshape(S5_BLOCKS, gpb * S5_GROUP, gpb * S5_STATE)

    def out_blocks(cc):
        t = cc.reshape(S5_BLOCKS, gpb, S5_GROUP, S5_STATE)
        return jnp.einsum('bghn,gk->bgnkh', t, eye).reshape(S5_BLOCKS, gpb * S5_STATE, gpb * S5_GROUP)

    bblk = jnp.concatenate([in_blocks(bb_re), in_blocks(bb_im)], axis=2).astype(BF16)
    cblk = jnp.concatenate([out_blocks(c_re), out_blocks(-c_im)], axis=1).astype(BF16)
    bcast = lambda a: jnp.broadcast_to(
        a.reshape(S5_BLOCKS, 1, S5_BLOCK_STATES), (S5_BLOCKS, BATCH, S5_BLOCK_STATES))
    return bblk, bcast(ab_re), bcast(ab_im), cblk


def kernel(x, meta, ffn1_w_gate, ffn1_w_up, ffn1_w_down, ln1_g, ln1_b, w_in, mla_q_norm_g, mla_w_uq, mla_kv_norm_g, mla_w_ukv, mla_w_o, conv_w, conv_b, conv_w_out, s5_a_re, s5_a_im, s5_log_dt, s5_b_re, s5_b_im, s5_c_re, s5_c_im, s5_d, s5_w_glu, s5_b_glu, s5_w_out, w_o, ln2_g, ln2_b, ffn2_w_gate, ffn2_w_up, ffn2_w_down, ln3_g, ln3_b):
    rows = lambda v: v.reshape(DEPTH, 1, -1).astype(F32)
    bf = lambda w: w.astype(BF16)
    pad_rows = lambda a: jnp.pad(a, ((0, BLOCK_Q - N_META), (0, 0)))
    per_batch = lambda a: jnp.broadcast_to(a[None], (BATCH,) + a.shape)
    ffn1 = (bf(ffn1_w_gate), bf(ffn1_w_up), bf(ffn1_w_down), rows(ln1_g), rows(ln1_b))
    ffn2 = (bf(ffn2_w_gate), bf(ffn2_w_up), bf(ffn2_w_down), rows(ln3_g), rows(ln3_b))
    wc, wqa, wqb, wuk, wuv, vones = jax.vmap(_qkv_weights)(w_in, mla_w_uq, mla_w_ukv)
    qkv_w = (wc, rows(mla_q_norm_g), rows(mla_kv_norm_g), wqa, wqb, wuk, wuv, vones)
    bblk, ar, ai, cblk = jax.vmap(_s5_weights)(s5_a_re, s5_a_im, s5_log_dt, s5_b_re, s5_b_im, s5_c_re, s5_c_im)
    mix_w = (bf(w_in[:, :, _OFF_CONV:]), conv_w.astype(F32), rows(conv_b), bf(conv_w_out),
             bblk, ar, ai, cblk, rows(s5_d), bf(s5_w_glu), rows(s5_b_glu), bf(s5_w_out),
             bf(mla_w_o), bf(w_o), rows(ln2_g), rows(ln2_b))

    h = x.astype(F32).reshape(BATCH * SEQ, D_MODEL)
    hm = meta.astype(F32)
    tab = _rope_tables(N_META + jnp.arange(SEQ))
    tab_m = _rope_tables(jnp.arange(N_META))
    for i in range(DEPTH):
        xm = _ffn_ln(hm, i, *ffn1)
        qm, km, vm = _qkv(xm[None], i, *qkv_w, tab_m)
        km, vm_t = pad_rows(km[0]), pad_rows(vm[0]).T
        att_m = _attention_meta(pad_rows(qm[0]), km, vm_t)[:N_META]
        hm8, state, halo = _mix(per_batch(xm), per_batch(att_m), jnp.zeros((BATCH, S5_COLS), F32),
                                jnp.zeros((_HALO, MIX_WIDTH), F32), i, *mix_w)

        xn = _ffn_ln(h, i, *ffn1)
        xn3 = xn.reshape(BATCH, SEQ, D_MODEL)
        q, k, v = _qkv(xn3, i, *qkv_w, tab)
        att = _attention(q, k, v, km, vm_t)
        h3, _, _ = _mix(xn3, att, state, halo, i, *mix_w)
        h = _ffn_ln(h3.reshape(BATCH * SEQ, D_MODEL), i, *ffn2)
        if i + 1 < DEPTH:
            hm = _ffn_ln(hm8[0], i, *ffn2)
    return h.reshape(BATCH, SEQ, D_MODEL)
```

```python
import functools
import math

import jax
import jax.numpy as jnp
from jax import lax
from jax.experimental import pallas as pl
from jax.experimental.pallas import tpu as pltpu

D_MODEL = 1024
BATCH = 8
SEQ = 2048
DEPTH = 2
N_META = 16
BLOCK_Q = 128
MIX_WIDTH = D_MODEL // 2
HEADS = 8
V_HEAD_DIM = 64
QK_NOPE_DIM = 64
QK_ROPE_DIM = 32
QK_HEAD_DIM = QK_NOPE_DIM + QK_ROPE_DIM
Q_LORA_RANK = 384
KV_LORA_RANK = 256
ROPE_BASE = 10000.0
S5_GROUP = 16
S5_GROUPS = 32
S5_STATE = 64
D_FF = 2816
ALPHA = (2.0 * DEPTH) ** 0.25
LN_EPS = 1e-5
RMS_EPS = 1e-6

HEAD_LANES = 128
QKV_WIDTH = HEADS * HEAD_LANES
TOK_TILE = 512
ATT_TILE = 2 * BLOCK_Q
N_SEQ_TILES = SEQ // ATT_TILE
V_ROWS = V_HEAD_DIM + 16
LOG2E = math.log2(math.e)
S5_BLOCKS = 4
S5_BLOCK_STATES = (S5_GROUPS // S5_BLOCKS) * S5_STATE
S5_COLS = 2 * S5_GROUPS * S5_STATE
MIX_STEPS = 64

_OFF_KR, _OFF_CONV = Q_LORA_RANK + KV_LORA_RANK, Q_LORA_RANK + KV_LORA_RANK + QK_ROPE_DIM
_OFF_GATES = _OFF_CONV + 4 * MIX_WIDTH
D_IN = _OFF_GATES + 3 * D_MODEL

V7X_VMEM_BYTES = 64 * 1024 * 1024
VMEM_LIMIT = V7X_VMEM_BYTES * 7 // 8

F32 = jnp.float32
BF16 = jnp.bfloat16


def _dot(a, b):
    return jnp.dot(a, b, preferred_element_type=F32)


def _const_spec(shape):
    nd = len(shape)
    return pl.BlockSpec(shape, lambda *_: (0,) * nd, pipeline_mode=pl.Buffered(1))


def _layer_spec(stacked, layer):
    nd = stacked.ndim - 1
    return pl.BlockSpec((None,) + stacked.shape[1:], lambda *_: (layer,) + (0,) * nd,
                        pipeline_mode=pl.Buffered(1))


def _layer_norm(y, g, b):
    mu = jnp.mean(y, axis=-1, keepdims=True)
    yc = y - mu
    var = jnp.mean(yc * yc, axis=-1, keepdims=True)
    return yc * lax.rsqrt(var + LN_EPS) * g + b


def _rms_norm(y, g):
    return y * lax.rsqrt(jnp.mean(y * y, axis=-1, keepdims=True) + RMS_EPS) * g


def _sigmoid(y):
    return 1.0 / (1.0 + jnp.exp(-y))


def _gelu_tanh(y):
    return 0.5 * y * (1.0 + jnp.tanh(math.sqrt(2.0 / math.pi) * (y + 0.044715 * (y * y * y))))


def _ffn_rows(x, wg_ref, wu_ref, wd_ref, g_ref, b_ref):
    xb = x.astype(BF16)
    gate = _dot(xb, wg_ref[...])
    hmid = (gate * _sigmoid(gate) * _dot(xb, wu_ref[...])).astype(BF16)
    return _layer_norm(ALPHA * x + 0.5 * _dot(hmid, wd_ref[...]), g_ref[...], b_ref[...])


def _ffn_ln_kernel(x_ref, wg_ref, wu_ref, wd_ref, g_ref, b_ref, o_ref, *, sub_rows):
    for r in range(x_ref.shape[0] // sub_rows):
        rows = slice(r * sub_rows, (r + 1) * sub_rows)
        o_ref[rows, :] = _ffn_rows(x_ref[rows, :], wg_ref, wu_ref, wd_ref, g_ref, b_ref)


def _ffn_ln(x, layer, wg, wu, wd, g, b):
    rows = x.shape[0]
    tm = min(rows, 2 * TOK_TILE)
    flat = pl.BlockSpec((tm, D_MODEL), lambda i: (i, 0))
    return pl.pallas_call(
        functools.partial(_ffn_ln_kernel, sub_rows=min(tm, TOK_TILE)),
        grid=(rows // tm,),
        in_specs=[flat] + [_layer_spec(w, layer) for w in (wg, wu, wd, g, b)],
        out_specs=flat,
        out_shape=jax.ShapeDtypeStruct((rows, D_MODEL), F32),
        compiler_params=pltpu.CompilerParams(
            dimension_semantics=("arbitrary",), vmem_limit_bytes=VMEM_LIMIT),
        name="ffn_ln",
    )(x, wg, wu, wd, g, b)


_WC_WIDTH = Q_LORA_RANK + KV_LORA_RANK + 2 * HEAD_LANES


def _qkv_kernel(x_ref, wc_ref, gq_ref, gkv_ref, wqa_ref, wqb_ref, wuk_ref, wuv_ref, vones_ref, tab_ref,
                q_ref, k_ref, v_ref, *, transpose_v):
    xb = x_ref[...].astype(BF16)
    c = _dot(xb, wc_ref[...])
    c_q = c[:, :Q_LORA_RANK]
    c_kv = c[:, Q_LORA_RANK:Q_LORA_RANK + KV_LORA_RANK]
    k_r = c[:, _WC_WIDTH - 2 * HEAD_LANES:_WC_WIDTH - HEAD_LANES]
    k_r_rot = c[:, _WC_WIDTH - HEAD_LANES:]
    qn = _rms_norm(c_q, gq_ref[...]).astype(BF16)
    kvn = _rms_norm(c_kv, gkv_ref[...]).astype(BF16)
    cos_q = tab_ref[:, 0 * HEAD_LANES:1 * HEAD_LANES]
    sin_q = tab_ref[:, 1 * HEAD_LANES:2 * HEAD_LANES]
    cos_k = tab_ref[:, 2 * HEAD_LANES:3 * HEAD_LANES]
    sin_k = tab_ref[:, 3 * HEAD_LANES:4 * HEAD_LANES]
    k_rope = k_r * cos_k + k_r_rot * sin_k
    q_a = _dot(qn, wqa_ref[...])
    q_b = _dot(qn, wqb_ref[...])
    k_nope = _dot(kvn, wuk_ref[...])
    for h in range(HEADS):
        sl = slice(h * HEAD_LANES, (h + 1) * HEAD_LANES)
        q_ref[:, sl] = (q_a[:, sl] * cos_q + q_b[:, sl] * sin_q).astype(BF16)
        k_ref[:, sl] = (k_nope[:, sl] + k_rope).astype(BF16)
    v = _dot(kvn, wuv_ref[...]) + vones_ref[...]
    if transpose_v:
        for t in range(x_ref.shape[0] // ATT_TILE):
            v_ref[t] = v[t * ATT_TILE:(t + 1) * ATT_TILE, :].T.astype(BF16)
    else:
        v_ref[...] = v.astype(BF16)


def _qkv(xn, layer, wc, gq, gkv, wqa, wqb, wuk, wuv, vones, tab):
    n_seq, rows, _ = xn.shape
    tile = min(rows, 2 * TOK_TILE)
    transpose_v = tile % ATT_TILE == 0
    seq = lambda w: pl.BlockSpec((None, tile, w), lambda s, j: (s, j, 0))
    seq_shape = jax.ShapeDtypeStruct((n_seq, rows, QKV_WIDTH), BF16)
    if transpose_v:
        v_spec = pl.BlockSpec((None, tile // ATT_TILE, HEADS * V_ROWS, ATT_TILE), lambda s, j: (s, j, 0, 0))
        v_shape = jax.ShapeDtypeStruct((n_seq, rows // ATT_TILE, HEADS * V_ROWS, ATT_TILE), BF16)
    else:
        v_spec = seq(HEADS * V_ROWS)
        v_shape = jax.ShapeDtypeStruct((n_seq, rows, HEADS * V_ROWS), BF16)
    return pl.pallas_call(
        functools.partial(_qkv_kernel, transpose_v=transpose_v),
        grid=(n_seq, rows // tile),
        in_specs=[seq(D_MODEL)]
        + [_layer_spec(w, layer) for w in (wc, gq, gkv, wqa, wqb, wuk, wuv, vones)]
        + [pl.BlockSpec((tile, 4 * HEAD_LANES), lambda s, j: (j, 0))],
        out_specs=[seq(QKV_WIDTH), seq(QKV_WIDTH), v_spec],
        out_shape=[seq_shape, seq_shape, v_shape],
        compiler_params=pltpu.CompilerParams(
            dimension_semantics=("arbitrary", "arbitrary"), vmem_limit_bytes=VMEM_LIMIT),
        name="qkv_proj",
    )(xn, wc, gq, gkv, wqa, wqb, wuk, wuv, vones, tab)


def _scores(q_ref, k_ref, s_scr, smax_scr, slot, c, mask):
    k0 = c * ATT_TILE if isinstance(c, int) else pl.multiple_of(c * ATT_TILE, ATT_TILE)
    for h in range(HEADS):
        sl = slice(h * HEAD_LANES, (h + 1) * HEAD_LANES)
        s_t = lax.dot_general(
            k_ref[pl.ds(k0, ATT_TILE), sl], q_ref[:, sl], (((1,), (1,)), ((), ())),
            preferred_element_type=F32)
        if mask is not None:
            s_t = jnp.where(mask, s_t, -1e30)
        s_scr[slot, h] = s_t
        smax_scr[slot, h] = jnp.max(s_t, axis=0, keepdims=True)


def _attend(v_ref, s_scr, smax_scr, m_scr, acc_scr, slot, c):
    for h in range(HEADS):
        m_old = m_scr[h]
        m_new = jnp.maximum(m_old, smax_scr[slot, h])
        p_t = jnp.exp2(s_scr[slot, h] - m_new).astype(BF16)
        pv = _dot(v_ref[c, h * V_ROWS:(h + 1) * V_ROWS, :], p_t)
        acc_scr[h] = jnp.exp2(m_old - m_new) * acc_scr[h] + pv
        m_scr[h] = m_new


def _scores_meta(q_ref, km_ref, sm_scr, m_scr, mask):
    for h in range(HEADS):
        sl = slice(h * HEAD_LANES, (h + 1) * HEAD_LANES)
        s_t = lax.dot_general(
            km_ref[:, sl], q_ref[:, sl], (((1,), (1,)), ((), ())), preferred_element_type=F32)
        s_t = jnp.where(mask, s_t, -1e30)
        sm_scr[h] = s_t
        m_scr[h] = jnp.max(s_t, axis=0, keepdims=True)


def _attend_meta(vm_ref, sm_scr, m_scr, acc_scr):
    for h in range(HEADS):
        p_t = jnp.exp2(sm_scr[h] - m_scr[h]).astype(BF16)
        acc_scr[h] = _dot(vm_ref[h * V_ROWS:(h + 1) * V_ROWS, :], p_t)


def _attn_out(acc_scr):
    o_t = jnp.concatenate(
        [acc_scr[h, 0:V_HEAD_DIM, :] / acc_scr[h, V_HEAD_DIM:V_HEAD_DIM + 1, :] for h in range(HEADS)],
        axis=0)
    return o_t.T.astype(BF16)


def _attn_kernel(q_ref, k_ref, v_ref, km_ref, vm_ref, o_ref, s_scr, smax_scr, sm_scr, m_scr, acc_scr):
    i = pl.program_id(1)
    causal = (lax.broadcasted_iota(jnp.int32, (ATT_TILE, ATT_TILE), 0)
              <= lax.broadcasted_iota(jnp.int32, (ATT_TILE, ATT_TILE), 1))
    meta_mask = lax.broadcasted_iota(jnp.int32, (BLOCK_Q, ATT_TILE), 0) < N_META

    scores = functools.partial(_scores, q_ref, k_ref, s_scr, smax_scr)
    attend = functools.partial(_attend, v_ref, s_scr, smax_scr, m_scr, acc_scr)
    _scores_meta(q_ref, km_ref, sm_scr, m_scr, meta_mask)
    scores(0, 0, causal | (i > 0))
    _attend_meta(vm_ref, sm_scr, m_scr, acc_scr)

    def two_tiles(pp, carry):
        p = 1 + 2 * pp
        scores(1, p, None)
        attend(0, p - 1)
        scores(0, p + 1, None)
        attend(1, p)
        return carry

    lax.fori_loop(0, (i - 1) // 2, two_tiles, 0)

    @pl.when(i == 0)
    def _():
        attend(0, 0)

    @pl.when(i % 2 == 1)
    def _():
        scores(1, i, causal)
        attend(0, i - 1)
        attend(1, i)

    @pl.when((i % 2 == 0) & (i > 0))
    def _():
        scores(1, i - 1, None)
        attend(0, i - 2)
        scores(0, i, causal)
        attend(1, i - 1)
        attend(0, i)

    o_ref[...] = _attn_out(acc_scr)


def _attention(q, k, v_t, k_meta, v_meta_t):
    return pl.pallas_call(
        _attn_kernel,
        grid=(BATCH, N_SEQ_TILES),
        in_specs=[
            pl.BlockSpec((None, ATT_TILE, QKV_WIDTH), lambda s, i: (s, i, 0)),
            pl.BlockSpec((None, SEQ, QKV_WIDTH), lambda s, i: (s, 0, 0)),
            pl.BlockSpec((None, N_SEQ_TILES, HEADS * V_ROWS, ATT_TILE), lambda s, i: (s, 0, 0, 0)),
            _const_spec((BLOCK_Q, QKV_WIDTH)),
            _const_spec((HEADS * V_ROWS, BLOCK_Q)),
        ],
        out_specs=pl.BlockSpec((None, ATT_TILE, MIX_WIDTH), lambda s, i: (s, i, 0)),
        out_shape=jax.ShapeDtypeStruct((BATCH, SEQ, MIX_WIDTH), BF16),
        scratch_shapes=[pltpu.VMEM((2, HEADS, ATT_TILE, ATT_TILE), F32),
                        pltpu.VMEM((2, HEADS, 1, ATT_TILE), F32),
                        pltpu.VMEM((HEADS, BLOCK_Q, ATT_TILE), F32),
                        pltpu.VMEM((HEADS, 1, ATT_TILE), F32),
                        pltpu.VMEM((HEADS, V_ROWS, ATT_TILE), F32)],
        compiler_params=pltpu.CompilerParams(
            dimension_semantics=("arbitrary", "arbitrary"), vmem_limit_bytes=VMEM_LIMIT),
        name="mla_attention",
    )(q, k, v_t, k_meta, v_meta_t)


def _attn_meta_kernel(q_ref, km_ref, vm_ref, o_ref, sm_scr, m_scr, acc_scr):
    k_off = lax.broadcasted_iota(jnp.int32, (BLOCK_Q, BLOCK_Q), 0)
    q_off = lax.broadcasted_iota(jnp.int32, (BLOCK_Q, BLOCK_Q), 1)
    _scores_meta(q_ref, km_ref, sm_scr, m_scr, (k_off < N_META) & (k_off <= q_off))
    _attend_meta(vm_ref, sm_scr, m_scr, acc_scr)
    o_ref[...] = _attn_out(acc_scr)


def _attention_meta(q_meta, k_meta, v_meta_t):
    return pl.pallas_call(
        _attn_meta_kernel,
        out_shape=jax.ShapeDtypeStruct((BLOCK_Q, MIX_WIDTH), BF16),
        scratch_shapes=[pltpu.VMEM((HEADS, BLOCK_Q, BLOCK_Q), F32),
                        pltpu.VMEM((HEADS, 1, BLOCK_Q), F32),
                        pltpu.VMEM((HEADS, V_ROWS, BLOCK_Q), F32)],
        name="mla_attention_meta",
    )(q_meta, k_meta, v_meta_t)


_HALO = 2 * BATCH
_D_SLABS = D_MODEL // HEAD_LANES


def _to_time_major(src_ref, slab_ref, width, steps):
    n = width // HEAD_LANES
    for b in range(BATCH):
        for j in range(n):
            slab_ref[j, pl.ds(b, steps, stride=BATCH), :] = (
                src_ref[b, :, j * HEAD_LANES:(j + 1) * HEAD_LANES].astype(F32))
    return jnp.concatenate([slab_ref[j] for j in range(n)], axis=1)


def _mix_kernel(x_ref, att_ref, st0_ref, halo0_ref, win_ref, cw_ref, cb_ref, cwo_ref, bblk_ref, ar_ref,
                ai_ref, cblk_ref, d_ref, wglu_ref, bglu_ref, swo_ref, mwo_ref, wo_ref, g_ref, b_ref,
                out_ref, st_out_ref, halo_out_ref, bu_scr, st_scr, cbuf_scr, slab_ref, *, steps):
    tm = steps * BATCH
    W = MIX_WIDTH

    @pl.when(pl.program_id(0) == 0)
    def _():
        st_scr[...] = st0_ref[...]
        cbuf_scr[0:_HALO, :] = halo0_ref[...]

    x = _to_time_major(x_ref, slab_ref, D_MODEL, steps)
    xb = x.astype(BF16)

    us = _dot(xb, win_ref[:, 3 * W:4 * W])
    usb = us.astype(BF16)
    nb = S5_BLOCK_STATES
    for blk in range(S5_BLOCKS):
        bu_scr[:, 2 * nb * blk:2 * nb * (blk + 1)] = _dot(
            usb[:, blk * 128:(blk + 1) * 128], bblk_ref[blk])
    for blk in range(S5_BLOCKS):
        c_re = 2 * nb * blk
        c_im = c_re + nb
        a_re = ar_ref[blk]
        a_im = ai_ref[blk]

        def step(t, carry, c_re=c_re, c_im=c_im, a_re=a_re, a_im=a_im):
            s_re, s_im = carry
            r0 = pl.multiple_of(t * BATCH, BATCH)
            n_re = a_re * s_re - a_im * s_im + bu_scr[pl.ds(r0, BATCH), c_re:c_re + nb]
            n_im = a_re * s_im + a_im * s_re + bu_scr[pl.ds(r0, BATCH), c_im:c_im + nb]
            bu_scr[pl.ds(r0, BATCH), c_re:c_re + nb] = n_re
            bu_scr[pl.ds(r0, BATCH), c_im:c_im + nb] = n_im
            return n_re, n_im

        s_re, s_im = lax.fori_loop(
            0, steps, step, (st_scr[:, c_re:c_re + nb], st_scr[:, c_im:c_im + nb]), unroll=True)
        st_scr[:, c_re:c_re + nb] = s_re
        st_scr[:, c_im:c_im + nb] = s_im

    pc = _dot(xb, win_ref[:, 0:3 * W])
    u = pc[:, 2 * W:3 * W] * pc[:, 0:W]
    cbuf_scr[_HALO:_HALO + tm, :] = u
    y = (cb_ref[...] + cw_ref[0:1, :] * cbuf_scr[0:tm, :]
         + cw_ref[1:2, :] * cbuf_scr[BATCH:BATCH + tm, :] + cw_ref[2:3, :] * u)
    cbuf_scr[0:_HALO, :] = u[tm - _HALO:, :]
    y_b = _dot((pc[:, W:2 * W] * y).astype(BF16), cwo_ref[...])

    att = _to_time_major(att_ref, slab_ref, W, steps).astype(BF16)
    y_a = _dot(att, mwo_ref[...])
    g0 = 4 * W
    mixed = _sigmoid(_dot(xb, win_ref[:, g0:g0 + D_MODEL])) * y_a
    mixed += _sigmoid(_dot(xb, win_ref[:, g0 + D_MODEL:g0 + 2 * D_MODEL])) * y_b
    gate_c = _sigmoid(_dot(xb, win_ref[:, g0 + 2 * D_MODEL:g0 + 3 * D_MODEL]))

    y = jnp.concatenate(
        [_dot(bu_scr[:, 2 * nb * blk:2 * nb * (blk + 1)].astype(BF16), cblk_ref[blk])
         for blk in range(S5_BLOCKS)], axis=1)
    y = _gelu_tanh(y + d_ref[...] * us)
    y = y * _sigmoid(_dot(y.astype(BF16), wglu_ref[...]) + bglu_ref[...])
    mixed += gate_c * _dot(y.astype(BF16), swo_ref[...])

    half = tm // 2
    for r in range(2):
        rows = slice(r * half, (r + 1) * half)
        z = _layer_norm(ALPHA * x[rows] + _dot(mixed[rows].astype(BF16), wo_ref[...]),
                        g_ref[...], b_ref[...])
        for j in range(_D_SLABS):
            slab_ref[j, rows, :] = z[:, j * HEAD_LANES:(j + 1) * HEAD_LANES]
    for b in range(BATCH):
        for j in range(_D_SLABS):
            out_ref[b, :, j * HEAD_LANES:(j + 1) * HEAD_LANES] = slab_ref[
                j, pl.ds(b, steps, stride=BATCH), :]
    st_out_ref[...] = st_scr[...]
    halo_out_ref[...] = cbuf_scr[0:_HALO, :]


def _mix(xn, att, st0, halo0, layer, *weights):
    t_len = xn.shape[1]
    steps = min(t_len, MIX_STEPS)
    tm = steps * BATCH
    tile = lambda w: pl.BlockSpec((BATCH, steps, w), lambda i: (0, i, 0))
    return pl.pallas_call(
        functools.partial(_mix_kernel, steps=steps),
        grid=(t_len // steps,),
        in_specs=[tile(D_MODEL), tile(MIX_WIDTH), _const_spec(st0.shape), _const_spec(halo0.shape)]
        + [_layer_spec(w, layer) for w in weights],
        out_specs=[tile(D_MODEL), pl.BlockSpec((BATCH, S5_COLS), lambda i: (0, 0)),
                   pl.BlockSpec((_HALO, MIX_WIDTH), lambda i: (0, 0))],
        out_shape=[jax.ShapeDtypeStruct((BATCH, t_len, D_MODEL), F32),
                   jax.ShapeDtypeStruct((BATCH, S5_COLS), F32),
                   jax.ShapeDtypeStruct((_HALO, MIX_WIDTH), F32)],
        scratch_shapes=[
            pltpu.VMEM((tm, S5_COLS), F32),
            pltpu.VMEM((BATCH, S5_COLS), F32),
            pltpu.VMEM((tm + _HALO, MIX_WIDTH), F32),
            pltpu.VMEM((_D_SLABS, tm, HEAD_LANES), F32),
        ],
        compiler_params=pltpu.CompilerParams(
            dimension_semantics=("arbitrary",), vmem_limit_bytes=VMEM_LIMIT),
        name="mixers_merge",
    )(xn, att, st0, halo0, *weights)


_MIX_IN = D_IN - _OFF_CONV
_CAST_ROWS = 256


def _mixer_w_in_kernel(w_ref, o_ref):
    o_ref[...] = w_ref[:, _OFF_CONV:].astype(BF16)


def _mixer_w_in(w_in):
    return pl.pallas_call(
        _mixer_w_in_kernel,
        grid=(DEPTH, D_MODEL // _CAST_ROWS),
        in_specs=[pl.BlockSpec((None, _CAST_ROWS, D_IN), lambda l, i: (l, i, 0))],
        out_specs=pl.BlockSpec((None, _CAST_ROWS, _MIX_IN), lambda l, i: (l, i, 0)),
        out_shape=jax.ShapeDtypeStruct((DEPTH, D_MODEL, _MIX_IN), BF16),
        compiler_params=pltpu.CompilerParams(dimension_semantics=("arbitrary", "arbitrary")),
        name="mixer_w_in",
    )(w_in)


def _rope_tables(pos):
    n = pos.shape[0]
    inv_freq = ROPE_BASE ** (-jnp.arange(0, QK_ROPE_DIM, 2, dtype=F32) / QK_ROPE_DIM)
    ang = pos.astype(F32)[:, None] * inv_freq[None, :]
    cos2 = jnp.tile(jnp.cos(ang), (1, 2))
    sin2 = jnp.tile(jnp.sin(ang), (1, 2))
    zn = jnp.zeros((n, QK_NOPE_DIM), F32)
    zt = jnp.zeros((n, HEAD_LANES - QK_HEAD_DIM), F32)
    scale = QK_HEAD_DIM ** -0.5 * LOG2E
    cos_q = scale * jnp.concatenate([jnp.ones_like(zn), cos2, zt], axis=1)
    sin_q = scale * jnp.concatenate([zn, sin2, zt], axis=1)
    cos_k = jnp.concatenate([zn, cos2, zt], axis=1)
    sin_k = jnp.concatenate([zn, sin2, zt], axis=1)
    return jnp.concatenate([cos_q, sin_q, cos_k, sin_k], axis=1)


def _rot_half_cols(w):
    half = QK_ROPE_DIM // 2
    return jnp.concatenate([-w[..., half:], w[..., :half]], axis=-1)


def _qkv_weights(w_in, w_uq, w_ukv):
    zeros = lambda *shape: jnp.zeros(shape, BF16)
    kr = w_in[:, _OFF_KR:_OFF_CONV]
    tail = HEAD_LANES - QK_HEAD_DIM
    wc = jnp.concatenate([
        w_in[:, :_OFF_KR],
        zeros(D_MODEL, QK_NOPE_DIM), kr, zeros(D_MODEL, tail),
        zeros(D_MODEL, QK_NOPE_DIM), _rot_half_cols(kr), zeros(D_MODEL, tail)], axis=1)
    uq = w_uq.reshape(Q_LORA_RANK, HEADS, QK_HEAD_DIM)
    wqa = jnp.concatenate([uq, zeros(Q_LORA_RANK, HEADS, tail)], axis=-1)
    wqb = jnp.concatenate([zeros(Q_LORA_RANK, HEADS, QK_NOPE_DIM), _rot_half_cols(uq[..., QK_NOPE_DIM:]),
                           zeros(Q_LORA_RANK, HEADS, tail)], axis=-1)
    ukv = w_ukv.reshape(KV_LORA_RANK, HEADS, QK_NOPE_DIM + V_HEAD_DIM)
    wuk = jnp.concatenate([ukv[..., :QK_NOPE_DIM], zeros(KV_LORA_RANK, HEADS, HEAD_LANES - QK_NOPE_DIM)],
                          axis=-1)
    wuv = jnp.concatenate([ukv[..., QK_NOPE_DIM:], zeros(KV_LORA_RANK, HEADS, V_ROWS - V_HEAD_DIM)], axis=-1)
    vones = jnp.zeros((HEADS, V_ROWS), F32).at[:, V_HEAD_DIM].set(1.0).reshape(1, HEADS * V_ROWS)
    flat = lambda w: w.reshape(w.shape[0], -1)
    return wc, flat(wqa), flat(wqb), flat(wuk), flat(wuv), vones


def _s5_weights(a_re, a_im, log_dt, b_re, b_im, c_re, c_im):
    dt = jnp.exp(log_dt)[:, None]
    mag = jnp.exp(dt * a_re)
    ab_re, ab_im = mag * jnp.cos(dt * a_im), mag * jnp.sin(dt * a_im)
    den = a_re * a_re + a_im * a_im
    nr, ni = ab_re - 1.0, ab_im
    coef_re = (nr * a_re + ni * a_im) / den
    coef_im = (ni * a_re - nr * a_im) / den
    bb_re = coef_re[..., None] * b_re - coef_im[..., None] * b_im
    bb_im = coef_re[..., None] * b_im + coef_im[..., None] * b_re
    gpb = S5_GROUPS // S5_BLOCKS
    eye = jnp.eye(gpb, dtype=F32)

    def in_blocks(bb):
        t = bb.transpose(0, 2, 1).reshape(S5_BLOCKS, gpb, S5_GROUP, S5_STATE)
        return jnp.einsum('bghn,gk->bghkn', t, eye).reshape(S5_BLOCKS, gpb * S5_GROUP, gpb * S5_STATE)

    def out_blocks(cc):
        t = cc.reshape(S5_BLOCKS, gpb, S5_GROUP, S5_STATE)
        return jnp.einsum('bghn,gk->bgnkh', t, eye).reshape(S5_BLOCKS, gpb * S5_STATE, gpb * S5_GROUP)

    bblk = jnp.concatenate([in_blocks(bb_re), in_blocks(bb_im)], axis=2).astype(BF16)
    cblk = jnp.concatenate([out_blocks(c_re), out_blocks(-c_im)], axis=1).astype(BF16)
    bcast = lambda a: jnp.broadcast_to(
        a.reshape(S5_BLOCKS, 1, S5_BLOCK_STATES), (S5_BLOCKS, BATCH, S5_BLOCK_STATES))
    return bblk, bcast(ab_re), bcast(ab_im), cblk


def kernel(x, meta, ffn1_w_gate, ffn1_w_up, ffn1_w_down, ln1_g, ln1_b, w_in, mla_q_norm_g, mla_w_uq, mla_kv_norm_g, mla_w_ukv, mla_w_o, conv_w, conv_b, conv_w_out, s5_a_re, s5_a_im, s5_log_dt, s5_b_re, s5_b_im, s5_c_re, s5_c_im, s5_d, s5_w_glu, s5_b_glu, s5_w_out, w_o, ln2_g, ln2_b, ffn2_w_gate, ffn2_w_up, ffn2_w_down, ln3_g, ln3_b):
    rows = lambda v: v.reshape(DEPTH, 1, -1).astype(F32)
    bf = lambda w: w.astype(BF16)
    pad_rows = lambda a: jnp.pad(a, ((0, BLOCK_Q - N_META), (0, 0)))
    per_batch = lambda a: jnp.broadcast_to(a[None], (BATCH,) + a.shape)
    ffn1 = (bf(ffn1_w_gate), bf(ffn1_w_up), bf(ffn1_w_down), rows(ln1_g), rows(ln1_b))
    ffn2 = (bf(ffn2_w_gate), bf(ffn2_w_up), bf(ffn2_w_down), rows(ln3_g), rows(ln3_b))
    wc, wqa, wqb, wuk, wuv, vones = jax.vmap(_qkv_weights)(
        bf(w_in[:, :, :_OFF_CONV]), bf(mla_w_uq), bf(mla_w_ukv))
    qkv_w = (wc, rows(mla_q_norm_g), rows(mla_kv_norm_g), wqa, wqb, wuk, wuv, vones)
    bblk, ar, ai, cblk = jax.vmap(_s5_weights)(s5_a_re, s5_a_im, s5_log_dt, s5_b_re, s5_b_im, s5_c_re, s5_c_im)
    mix_w = (_mixer_w_in(w_in), conv_w.astype(F32), rows(conv_b), bf(conv_w_out),
             bblk, ar, ai, cblk, rows(s5_d), bf(s5_w_glu), rows(s5_b_glu), bf(s5_w_out),
             bf(mla_w_o), bf(w_o), rows(ln2_g), rows(ln2_b))

    h = x.astype(F32).reshape(BATCH * SEQ, D_MODEL)
    hm = meta.astype(F32)
    tab = _rope_tables(N_META + jnp.arange(SEQ))
    tab_m = _rope_tables(jnp.arange(N_META))
    for i in range(DEPTH):
        xm = _ffn_ln(hm, i, *ffn1)
        qm, km, vm = _qkv(xm[None], i, *qkv_w, tab_m)
        km, vm_t = pad_rows(km[0]), pad_rows(vm[0]).T
        att_m = _attention_meta(pad_rows(qm[0]), km, vm_t)[:N_META]
        hm8, state, halo = _mix(per_batch(xm), per_batch(att_m), jnp.zeros((BATCH, S5_COLS), F32),
                                jnp.zeros((_HALO, MIX_WIDTH), F32), i, *mix_w)

        xn3 = _ffn_ln(h, i, *ffn1).reshape(BATCH, SEQ, D_MODEL)
        q, k, v = _qkv(xn3, i, *qkv_w, tab)
        att = _attention(q, k, v, km, vm_t)
        h3, _, _ = _mix(xn3, att, state, halo, i, *mix_w)
        h = _ffn_ln(h3.reshape(BATCH * SEQ, D_MODEL), i, *ffn2)
        if i + 1 < DEPTH:
            hm = _ffn_ln(hm8[0], i, *ffn2)
    return h.reshape(BATCH, SEQ, D_MODEL)
```

```python
import functools
import math

import jax
import jax.numpy as jnp
from jax import lax
from jax.experimental import pallas as pl
from jax.experimental.pallas import tpu as pltpu

D_MODEL = 1024
BATCH = 8
SEQ = 2048
DEPTH = 2
N_META = 16
BLOCK_Q = 128
MIX_WIDTH = D_MODEL // 2
HEADS = 8
V_HEAD_DIM = 64
QK_NOPE_DIM = 64
QK_ROPE_DIM = 32
QK_HEAD_DIM = QK_NOPE_DIM + QK_ROPE_DIM
Q_LORA_RANK = 384
KV_LORA_RANK = 256
ROPE_BASE = 10000.0
S5_GROUP = 16
S5_GROUPS = 32
S5_STATE = 64
D_FF = 2816
ALPHA = (2.0 * DEPTH) ** 0.25
LN_EPS = 1e-5
RMS_EPS = 1e-6

HEAD_LANES = 128
QKV_WIDTH = HEADS * HEAD_LANES
TOK_TILE = 512
ATT_TILE = 2 * BLOCK_Q
N_SEQ_TILES = SEQ // ATT_TILE
V_ROWS = V_HEAD_DIM + 16
LOG2E = math.log2(math.e)
S5_BLOCKS = 4
S5_BLOCK_STATES = (S5_GROUPS // S5_BLOCKS) * S5_STATE
S5_COLS = 2 * S5_GROUPS * S5_STATE
MIX_STEPS = 64

_OFF_KR, _OFF_CONV = Q_LORA_RANK + KV_LORA_RANK, Q_LORA_RANK + KV_LORA_RANK + QK_ROPE_DIM
_OFF_GATES = _OFF_CONV + 4 * MIX_WIDTH
D_IN = _OFF_GATES + 3 * D_MODEL

V7X_VMEM_BYTES = 64 * 1024 * 1024
VMEM_LIMIT = V7X_VMEM_BYTES * 7 // 8

F32 = jnp.float32
BF16 = jnp.bfloat16


def _dot(a, b):
    return jnp.dot(a, b, preferred_element_type=F32)


def _const_spec(shape):
    nd = len(shape)
    return pl.BlockSpec(shape, lambda *_: (0,) * nd, pipeline_mode=pl.Buffered(1))


def _layer_spec(stacked, layer):
    nd = stacked.ndim - 1
    return pl.BlockSpec((None,) + stacked.shape[1:], lambda *_: (layer,) + (0,) * nd,
                        pipeline_mode=pl.Buffered(1))


def _layer_norm(y, g, b):
    mu = jnp.mean(y, axis=-1, keepdims=True)
    yc = y - mu
    var = jnp.mean(yc * yc, axis=-1, keepdims=True)
    return yc * lax.rsqrt(var + LN_EPS) * g + b


def _rms_norm(y, g):
    return y * lax.rsqrt(jnp.mean(y * y, axis=-1, keepdims=True) + RMS_EPS) * g


def _sigmoid(y):
    return 1.0 / (1.0 + jnp.exp(-y))


def _gelu_tanh(y):
    return 0.5 * y * (1.0 + jnp.tanh(math.sqrt(2.0 / math.pi) * (y + 0.044715 * (y * y * y))))


def _ffn_rows(x, wg_ref, wu_ref, wd_ref, g_ref, b_ref):
    xb = x.astype(BF16)
    gate = _dot(xb, wg_ref[...])
    hmid = (gate * _sigmoid(gate) * _dot(xb, wu_ref[...])).astype(BF16)
    return _layer_norm(ALPHA * x + 0.5 * _dot(hmid, wd_ref[...]), g_ref[...], b_ref[...])


def _ffn_ln_kernel(x_ref, wg_ref, wu_ref, wd_ref, g_ref, b_ref, o_ref, *, sub_rows):
    for r in range(x_ref.shape[0] // sub_rows):
        rows = slice(r * sub_rows, (r + 1) * sub_rows)
        o_ref[rows, :] = _ffn_rows(x_ref[rows, :], wg_ref, wu_ref, wd_ref, g_ref, b_ref)


def _ffn_ln(x, layer, wg, wu, wd, g, b):
    rows = x.shape[0]
    tm = min(rows, 2 * TOK_TILE)
    flat = pl.BlockSpec((tm, D_MODEL), lambda i: (i, 0))
    return pl.pallas_call(
        functools.partial(_ffn_ln_kernel, sub_rows=min(tm, TOK_TILE)),
        grid=(rows // tm,),
        in_specs=[flat] + [_layer_spec(w, layer) for w in (wg, wu, wd, g, b)],
        out_specs=flat,
        out_shape=jax.ShapeDtypeStruct((rows, D_MODEL), F32),
        compiler_params=pltpu.CompilerParams(
            dimension_semantics=("arbitrary",), vmem_limit_bytes=VMEM_LIMIT),
        name="ffn_ln",
    )(x, wg, wu, wd, g, b)


_WC_WIDTH = Q_LORA_RANK + KV_LORA_RANK + 2 * HEAD_LANES


def _qkv_kernel(x_ref, wc_ref, gq_ref, gkv_ref, wqa_ref, wqb_ref, wuk_ref, wuv_ref, vones_ref, tab_ref,
                q_ref, k_ref, v_ref, *, transpose_v):
    xb = x_ref[...].astype(BF16)
    c = _dot(xb, wc_ref[...])
    c_q = c[:, :Q_LORA_RANK]
    c_kv = c[:, Q_LORA_RANK:Q_LORA_RANK + KV_LORA_RANK]
    k_r = c[:, _WC_WIDTH - 2 * HEAD_LANES:_WC_WIDTH - HEAD_LANES]
    k_r_rot = c[:, _WC_WIDTH - HEAD_LANES:]
    qn = _rms_norm(c_q, gq_ref[...]).astype(BF16)
    kvn = _rms_norm(c_kv, gkv_ref[...]).astype(BF16)
    cos_q = tab_ref[:, 0 * HEAD_LANES:1 * HEAD_LANES]
    sin_q = tab_ref[:, 1 * HEAD_LANES:2 * HEAD_LANES]
    cos_k = tab_ref[:, 2 * HEAD_LANES:3 * HEAD_LANES]
    sin_k = tab_ref[:, 3 * HEAD_LANES:4 * HEAD_LANES]
    k_rope = k_r * cos_k + k_r_rot * sin_k
    q_a = _dot(qn, wqa_ref[...])
    q_b = _dot(qn, wqb_ref[...])
    k_nope = _dot(kvn, wuk_ref[...])
    for h in range(HEADS):
        sl = slice(h * HEAD_LANES, (h + 1) * HEAD_LANES)
        q_ref[:, sl] = (q_a[:, sl] * cos_q + q_b[:, sl] * sin_q).astype(BF16)
        k_ref[:, sl] = (k_nope[:, sl] + k_rope).astype(BF16)
    v = _dot(kvn, wuv_ref[...]) + vones_ref[...]
    if transpose_v:
        for t in range(x_ref.shape[0] // ATT_TILE):
            v_ref[t] = v[t * ATT_TILE:(t + 1) * ATT_TILE, :].T.astype(BF16)
    else:
        v_ref[...] = v.astype(BF16)


def _qkv(xn, layer, wc, gq, gkv, wqa, wqb, wuk, wuv, vones, tab):
    n_seq, rows, _ = xn.shape
    tile = min(rows, 2 * TOK_TILE)
    transpose_v = tile % ATT_TILE == 0
    seq = lambda w: pl.BlockSpec((None, tile, w), lambda s, j: (s, j, 0))
    seq_shape = jax.ShapeDtypeStruct((n_seq, rows, QKV_WIDTH), BF16)
    if transpose_v:
        v_spec = pl.BlockSpec((None, tile // ATT_TILE, HEADS * V_ROWS, ATT_TILE), lambda s, j: (s, j, 0, 0))
        v_shape = jax.ShapeDtypeStruct((n_seq, rows // ATT_TILE, HEADS * V_ROWS, ATT_TILE), BF16)
    else:
        v_spec = seq(HEADS * V_ROWS)
        v_shape = jax.ShapeDtypeStruct((n_seq, rows, HEADS * V_ROWS), BF16)
    return pl.pallas_call(
        functools.partial(_qkv_kernel, transpose_v=transpose_v),
        grid=(n_seq, rows // tile),
        in_specs=[seq(D_MODEL)]
        + [_layer_spec(w, layer) for w in (wc, gq, gkv, wqa, wqb, wuk, wuv, vones)]
        + [pl.BlockSpec((tile, 4 * HEAD_LANES), lambda s, j: (j, 0))],
        out_specs=[seq(QKV_WIDTH), seq(QKV_WIDTH), v_spec],
        out_shape=[seq_shape, seq_shape, v_shape],
        compiler_params=pltpu.CompilerParams(
            dimension_semantics=("arbitrary", "arbitrary"), vmem_limit_bytes=VMEM_LIMIT),
        name="qkv_proj",
    )(xn, wc, gq, gkv, wqa, wqb, wuk, wuv, vones, tab)


def _scores(q_ref, k_ref, s_scr, smax_scr, slot, c, mask):
    k0 = c * ATT_TILE if isinstance(c, int) else pl.multiple_of(c * ATT_TILE, ATT_TILE)
    for h in range(HEADS):
        sl = slice(h * HEAD_LANES, (h + 1) * HEAD_LANES)
        s_t = lax.dot_general(
            k_ref[pl.ds(k0, ATT_TILE), sl], q_ref[:, sl], (((1,), (1,)), ((), ())),
            preferred_element_type=F32)
        if mask is not None:
            s_t = jnp.where(mask, s_t, -1e30)
        s_scr[slot, h] = s_t
        smax_scr[slot, h] = jnp.max(s_t, axis=0, keepdims=True)


def _attend(v_ref, s_scr, smax_scr, m_scr, acc_scr, slot, c):
    for h in range(HEADS):
        m_old = m_scr[h]
        m_new = jnp.maximum(m_old, smax_scr[slot, h])
        p_t = jnp.exp2(s_scr[slot, h] - m_new).astype(BF16)
        pv = _dot(v_ref[c, h * V_ROWS:(h + 1) * V_ROWS, :], p_t)
        acc_scr[h] = jnp.exp2(m_old - m_new) * acc_scr[h] + pv
        m_scr[h] = m_new


def _scores_meta(q_ref, km_ref, sm_scr, m_scr, mask):
    for h in range(HEADS):
        sl = slice(h * HEAD_LANES, (h + 1) * HEAD_LANES)
        s_t = lax.dot_general(
            km_ref[:, sl], q_ref[:, sl], (((1,), (1,)), ((), ())), preferred_element_type=F32)
        s_t = jnp.where(mask, s_t, -1e30)
        sm_scr[h] = s_t
        m_scr[h] = jnp.max(s_t, axis=0, keepdims=True)


def _attend_meta(vm_ref, sm_scr, m_scr, acc_scr):
    for h in range(HEADS):
        p_t = jnp.exp2(sm_scr[h] - m_scr[h]).astype(BF16)
        acc_scr[h] = _dot(vm_ref[h * V_ROWS:(h + 1) * V_ROWS, :], p_t)


def _attn_out(acc_scr):
    o_t = jnp.concatenate(
        [acc_scr[h, 0:V_HEAD_DIM, :] / acc_scr[h, V_HEAD_DIM:V_HEAD_DIM + 1, :] for h in range(HEADS)],
        axis=0)
    return o_t.T.astype(BF16)


def _attn_kernel(q_ref, k_ref, v_ref, km_ref, vm_ref, o_ref, s_scr, smax_scr, sm_scr, m_scr, acc_scr):
    i = pl.program_id(1)
    causal = (lax.broadcasted_iota(jnp.int32, (ATT_TILE, ATT_TILE), 0)
              <= lax.broadcasted_iota(jnp.int32, (ATT_TILE, ATT_TILE), 1))
    meta_mask = lax.broadcasted_iota(jnp.int32, (BLOCK_Q, ATT_TILE), 0) < N_META

    scores = functools.partial(_scores, q_ref, k_ref, s_scr, smax_scr)
    attend = functools.partial(_attend, v_ref, s_scr, smax_scr, m_scr, acc_scr)
    _scores_meta(q_ref, km_ref, sm_scr, m_scr, meta_mask)
    scores(0, 0, causal | (i > 0))
    _attend_meta(vm_ref, sm_scr, m_scr, acc_scr)

    def two_tiles(pp, carry):
        p = 1 + 2 * pp
        scores(1, p, None)
        attend(0, p - 1)
        scores(0, p + 1, None)
        attend(1, p)
        return carry

    lax.fori_loop(0, (i - 1) // 2, two_tiles, 0)

    @pl.when(i == 0)
    def _():
        attend(0, 0)

    @pl.when(i % 2 == 1)
    def _():
        scores(1, i, causal)
        attend(0, i - 1)
        attend(1, i)

    @pl.when((i % 2 == 0) & (i > 0))
    def _():
        scores(1, i - 1, None)
        attend(0, i - 2)
        scores(0, i, causal)
        attend(1, i - 1)
        attend(0, i)

    o_ref[...] = _attn_out(acc_scr)


def _attention(q, k, v_t, k_meta, v_meta_t):
    return pl.pallas_call(
        _attn_kernel,
        grid=(BATCH, N_SEQ_TILES),
        in_specs=[
            pl.BlockSpec((None, ATT_TILE, QKV_WIDTH), lambda s, i: (s, i, 0)),
            pl.BlockSpec((None, SEQ, QKV_WIDTH), lambda s, i: (s, 0, 0)),
            pl.BlockSpec((None, N_SEQ_TILES, HEADS * V_ROWS, ATT_TILE), lambda s, i: (s, 0, 0, 0)),
            _const_spec((BLOCK_Q, QKV_WIDTH)),
            _const_spec((HEADS * V_ROWS, BLOCK_Q)),
        ],
        out_specs=pl.BlockSpec((None, ATT_TILE, MIX_WIDTH), lambda s, i: (s, i, 0)),
        out_shape=jax.ShapeDtypeStruct((BATCH, SEQ, MIX_WIDTH), BF16),
        scratch_shapes=[pltpu.VMEM((2, HEADS, ATT_TILE, ATT_TILE), F32),
                        pltpu.VMEM((2, HEADS, 1, ATT_TILE), F32),
                        pltpu.VMEM((HEADS, BLOCK_Q, ATT_TILE), F32),
                        pltpu.VMEM((HEADS, 1, ATT_TILE), F32),
                        pltpu.VMEM((HEADS, V_ROWS, ATT_TILE), F32)],
        compiler_params=pltpu.CompilerParams(
            dimension_semantics=("arbitrary", "arbitrary"), vmem_limit_bytes=VMEM_LIMIT),
        name="mla_attention",
    )(q, k, v_t, k_meta, v_meta_t)


def _attn_meta_kernel(q_ref, km_ref, vm_ref, o_ref, sm_scr, m_scr, acc_scr):
    k_off = lax.broadcasted_iota(jnp.int32, (BLOCK_Q, BLOCK_Q), 0)
    q_off = lax.broadcasted_iota(jnp.int32, (BLOCK_Q, BLOCK_Q), 1)
    _scores_meta(q_ref, km_ref, sm_scr, m_scr, (k_off < N_META) & (k_off <= q_off))
    _attend_meta(vm_ref, sm_scr, m_scr, acc_scr)
    o_ref[...] = _attn_out(acc_scr)


def _attention_meta(q_meta, k_meta, v_meta_t):
    return pl.pallas_call(
        _attn_meta_kernel,
        out_shape=jax.ShapeDtypeStruct((BLOCK_Q, MIX_WIDTH), BF16),
        scratch_shapes=[pltpu.VMEM((HEADS, BLOCK_Q, BLOCK_Q), F32),
                        pltpu.VMEM((HEADS, 1, BLOCK_Q), F32),
                        pltpu.VMEM((HEADS, V_ROWS, BLOCK_Q), F32)],
        name="mla_attention_meta",
    )(q_meta, k_meta, v_meta_t)


_HALO = 2 * BATCH
_D_SLABS = D_MODEL // HEAD_LANES


def _to_time_major(src_ref, slab_ref, width, steps):
    n = width // HEAD_LANES
    for b in range(BATCH):
        for j in range(n):
            slab_ref[j, pl.ds(b, steps, stride=BATCH), :] = (
                src_ref[b, :, j * HEAD_LANES:(j + 1) * HEAD_LANES].astype(F32))
    return jnp.concatenate([slab_ref[j] for j in range(n)], axis=1)


def _mix_kernel(x_ref, att_ref, st0_ref, halo0_ref, win_ref, cw_ref, cb_ref, cwo_ref, bblk_ref, ar_ref,
                ai_ref, cblk_ref, d_ref, wglu_ref, bglu_ref, swo_ref, mwo_ref, wo_ref, g_ref, b_ref,
                out_ref, st_out_ref, halo_out_ref, bu_scr, st_scr, cbuf_scr, slab_ref, *, steps):
    tm = steps * BATCH
    W = MIX_WIDTH

    @pl.when(pl.program_id(0) == 0)
    def _():
        st_scr[...] = st0_ref[...]
        cbuf_scr[0:_HALO, :] = halo0_ref[...]

    x = _to_time_major(x_ref, slab_ref, D_MODEL, steps)
    xb = x.astype(BF16)

    us = _dot(xb, win_ref[:, 3 * W:4 * W])
    usb = us.astype(BF16)
    nb = S5_BLOCK_STATES
    for blk in range(S5_BLOCKS):
        bu_scr[:, 2 * nb * blk:2 * nb * (blk + 1)] = _dot(
            usb[:, blk * 128:(blk + 1) * 128], bblk_ref[blk])
    for blk in range(S5_BLOCKS):
        c_re = 2 * nb * blk
        c_im = c_re + nb
        a_re = ar_ref[blk]
        a_im = ai_ref[blk]

        def step(t, carry, c_re=c_re, c_im=c_im, a_re=a_re, a_im=a_im):
            s_re, s_im = carry
            r0 = pl.multiple_of(t * BATCH, BATCH)
            n_re = a_re * s_re - a_im * s_im + bu_scr[pl.ds(r0, BATCH), c_re:c_re + nb]
            n_im = a_re * s_im + a_im * s_re + bu_scr[pl.ds(r0, BATCH), c_im:c_im + nb]
            bu_scr[pl.ds(r0, BATCH), c_re:c_re + nb] = n_re
            bu_scr[pl.ds(r0, BATCH), c_im:c_im + nb] = n_im
            return n_re, n_im

        s_re, s_im = lax.fori_loop(
            0, steps, step, (st_scr[:, c_re:c_re + nb], st_scr[:, c_im:c_im + nb]), unroll=True)
        st_scr[:, c_re:c_re + nb] = s_re
        st_scr[:, c_im:c_im + nb] = s_im

    pc = _dot(xb, win_ref[:, 0:3 * W])
    u = pc[:, 2 * W:3 * W] * pc[:, 0:W]
    cbuf_scr[_HALO:_HALO + tm, :] = u
    y = (cb_ref[...] + cw_ref[0:1, :] * cbuf_scr[0:tm, :]
         + cw_ref[1:2, :] * cbuf_scr[BATCH:BATCH + tm, :] + cw_ref[2:3, :] * u)
    cbuf_scr[0:_HALO, :] = u[tm - _HALO:, :]
    y_b = _dot((pc[:, W:2 * W] * y).astype(BF16), cwo_ref[...])

    att = _to_time_major(att_ref, slab_ref, W, steps).astype(BF16)
    y_a = _dot(att, mwo_ref[...])
    g0 = 4 * W
    mixed = _sigmoid(_dot(xb, win_ref[:, g0:g0 + D_MODEL])) * y_a
    mixed += _sigmoid(_dot(xb, win_ref[:, g0 + D_MODEL:g0 + 2 * D_MODEL])) * y_b
    gate_c = _sigmoid(_dot(xb, win_ref[:, g0 + 2 * D_MODEL:g0 + 3 * D_MODEL]))

    y = jnp.concatenate(
        [_dot(bu_scr[:, 2 * nb * blk:2 * nb * (blk + 1)].astype(BF16), cblk_ref[blk])
         for blk in range(S5_BLOCKS)], axis=1)
    y = _gelu_tanh(y + d_ref[...] * us)
    y = y * _sigmoid(_dot(y.astype(BF16), wglu_ref[...]) + bglu_ref[...])
    mixed += gate_c * _dot(y.astype(BF16), swo_ref[...])

    half = tm // 2
    for r in range(2):
        rows = slice(r * half, (r + 1) * half)
        z = _layer_norm(ALPHA * x[rows] + _dot(mixed[rows].astype(BF16), wo_ref[...]),
                        g_ref[...], b_ref[...])
        for j in range(_D_SLABS):
            slab_ref[j, rows, :] = z[:, j * HEAD_LANES:(j + 1) * HEAD_LANES]
    for b in range(BATCH):
        for j in range(_D_SLABS):
            out_ref[b, :, j * HEAD_LANES:(j + 1) * HEAD_LANES] = slab_ref[
                j, pl.ds(b, steps, stride=BATCH), :]
    st_out_ref[...] = st_scr[...]
    halo_out_ref[...] = cbuf_scr[0:_HALO, :]


def _mix(xn, att, st0, halo0, layer, *weights):
    t_len = xn.shape[1]
    steps = min(t_len, MIX_STEPS)
    tm = steps * BATCH
    tile = lambda w: pl.BlockSpec((BATCH, steps, w), lambda i: (0, i, 0))
    return pl.pallas_call(
        functools.partial(_mix_kernel, steps=steps),
        grid=(t_len // steps,),
        in_specs=[tile(D_MODEL), tile(MIX_WIDTH), _const_spec(st0.shape), _const_spec(halo0.shape)]
        + [_layer_spec(w, layer) for w in weights],
        out_specs=[tile(D_MODEL), pl.BlockSpec((BATCH, S5_COLS), lambda i: (0, 0)),
                   pl.BlockSpec((_HALO, MIX_WIDTH), lambda i: (0, 0))],
        out_shape=[jax.ShapeDtypeStruct((BATCH, t_len, D_MODEL), F32),
                   jax.ShapeDtypeStruct((BATCH, S5_COLS), F32),
                   jax.ShapeDtypeStruct((_HALO, MIX_WIDTH), F32)],
        scratch_shapes=[
            pltpu.VMEM((tm, S5_COLS), F32),
            pltpu.VMEM((BATCH, S5_COLS), F32),
            pltpu.VMEM((tm + _HALO, MIX_WIDTH), F32),
            pltpu.VMEM((_D_SLABS, tm, HEAD_LANES), F32),
        ],
        compiler_params=pltpu.CompilerParams(
            dimension_semantics=("arbitrary",), vmem_limit_bytes=VMEM_LIMIT),
        name="mixers_merge",
    )(xn, att, st0, halo0, *weights)


_MIX_IN = D_IN - _OFF_CONV
_CAST_ROWS = 256


def _split_w_in_kernel(w_ref, mla_ref, mix_ref):
    mla_ref[...] = w_ref[:, :_OFF_CONV].astype(BF16)
    mix_ref[...] = w_ref[:, _OFF_CONV:].astype(BF16)


def _split_w_in(w_in):
    spec = lambda w: pl.BlockSpec((None, _CAST_ROWS, w), lambda l, i: (l, i, 0))
    return pl.pallas_call(
        _split_w_in_kernel,
        grid=(DEPTH, D_MODEL // _CAST_ROWS),
        in_specs=[spec(D_IN)],
        out_specs=[spec(_OFF_CONV), spec(_MIX_IN)],
        out_shape=[jax.ShapeDtypeStruct((DEPTH, D_MODEL, _OFF_CONV), BF16),
                   jax.ShapeDtypeStruct((DEPTH, D_MODEL, _MIX_IN), BF16)],
        compiler_params=pltpu.CompilerParams(dimension_semantics=("arbitrary", "arbitrary")),
        name="split_w_in",
    )(w_in)


def _rope_tables(pos):
    n = pos.shape[0]
    inv_freq = ROPE_BASE ** (-jnp.arange(0, QK_ROPE_DIM, 2, dtype=F32) / QK_ROPE_DIM)
    ang = pos.astype(F32)[:, None] * inv_freq[None, :]
    cos2 = jnp.tile(jnp.cos(ang), (1, 2))
    sin2 = jnp.tile(jnp.sin(ang), (1, 2))
    zn = jnp.zeros((n, QK_NOPE_DIM), F32)
    zt = jnp.zeros((n, HEAD_LANES - QK_HEAD_DIM), F32)
    scale = QK_HEAD_DIM ** -0.5 * LOG2E
    cos_q = scale * jnp.concatenate([jnp.ones_like(zn), cos2, zt], axis=1)
    sin_q = scale * jnp.concatenate([zn, sin2, zt], axis=1)
    cos_k = jnp.concatenate([zn, cos2, zt], axis=1)
    sin_k = jnp.concatenate([zn, sin2, zt], axis=1)
    return jnp.concatenate([cos_q, sin_q, cos_k, sin_k], axis=1)


def _rot_half_cols(w):
    half = QK_ROPE_DIM // 2
    return jnp.concatenate([-w[..., half:], w[..., :half]], axis=-1)


def _qkv_weights(w_in, w_uq, w_ukv):
    zeros = lambda *shape: jnp.zeros(shape, BF16)
    kr = w_in[:, _OFF_KR:_OFF_CONV]
    tail = HEAD_LANES - QK_HEAD_DIM
    wc = jnp.concatenate([
        w_in[:, :_OFF_KR],
        zeros(D_MODEL, QK_NOPE_DIM), kr, zeros(D_MODEL, tail),
        zeros(D_MODEL, QK_NOPE_DIM), _rot_half_cols(kr), zeros(D_MODEL, tail)], axis=1)
    uq = w_uq.reshape(Q_LORA_RANK, HEADS, QK_HEAD_DIM)
    wqa = jnp.concatenate([uq, zeros(Q_LORA_RANK, HEADS, tail)], axis=-1)
    wqb = jnp.concatenate([zeros(Q_LORA_RANK, HEADS, QK_NOPE_DIM), _rot_half_cols(uq[..., QK_NOPE_DIM:]),
                           zeros(Q_LORA_RANK, HEADS, tail)], axis=-1)
    ukv = w_ukv.reshape(KV_LORA_RANK, HEADS, QK_NOPE_DIM + V_HEAD_DIM)
    wuk = jnp.concatenate([ukv[..., :QK_NOPE_DIM], zeros(KV_LORA_RANK, HEADS, HEAD_LANES - QK_NOPE_DIM)],
                          axis=-1)
    wuv = jnp.concatenate([ukv[..., QK_NOPE_DIM:], zeros(KV_LORA_RANK, HEADS, V_ROWS - V_HEAD_DIM)], axis=-1)
    vones = jnp.zeros((HEADS, V_ROWS), F32).at[:, V_HEAD_DIM].set(1.0).reshape(1, HEADS * V_ROWS)
    flat = lambda w: w.reshape(w.shape[0], -1)
    return wc, flat(wqa), flat(wqb), flat(wuk), flat(wuv), vones


def _s5_weights(a_re, a_im, log_dt, b_re, b_im, c_re, c_im):
    dt = jnp.exp(log_dt)[:, None]
    mag = jnp.exp(dt * a_re)
    ab_re, ab_im = mag * jnp.cos(dt * a_im), mag * jnp.sin(dt * a_im)
    den = a_re * a_re + a_im * a_im
    nr, ni = ab_re - 1.0, ab_im
    coef_re = (nr * a_re + ni * a_im) / den
    coef_im = (ni * a_re - nr * a_im) / den
    bb_re = coef_re[..., None] * b_re - coef_im[..., None] * b_im
    bb_im = coef_re[..., None] * b_im + coef_im[..., None] * b_re
    gpb = S5_GROUPS // S5_BLOCKS
    eye = jnp.eye(gpb, dtype=F32)

    def in_blocks(bb):
        t = bb.transpose(0, 2, 1).reshape(S5_BLOCKS, gpb, S5_GROUP, S5_STATE)
        return jnp.einsum('bghn,gk->bghkn', t, eye).reshape(S5_BLOCKS, gpb * S5_GROUP, gpb * S5_STATE)

    def out_blocks(cc):
        t = cc.reshape(S5_BLOCKS, gpb, S5_GROUP, S5_STATE)
        return jnp.einsum('bghn,gk->bgnkh', t, eye).reshape(S5_BLOCKS, gpb * S5_STATE, gpb * S5_GROUP)

    bblk = jnp.concatenate([in_blocks(bb_re), in_blocks(bb_im)], axis=2).astype(BF16)
    cblk = jnp.concatenate([out_blocks(c_re), out_blocks(-c_im)], axis=1).astype(BF16)
    bcast = lambda a: jnp.broadcast_to(
        a.reshape(S5_BLOCKS, 1, S5_BLOCK_STATES), (S5_BLOCKS, BATCH, S5_BLOCK_STATES))
    return bblk, bcast(ab_re), bcast(ab_im), cblk


def kernel(x, meta, ffn1_w_gate, ffn1_w_up, ffn1_w_down, ln1_g, ln1_b, w_in, mla_q_norm_g, mla_w_uq, mla_kv_norm_g, mla_w_ukv, mla_w_o, conv_w, conv_b, conv_w_out, s5_a_re, s5_a_im, s5_log_dt, s5_b_re, s5_b_im, s5_c_re, s5_c_im, s5_d, s5_w_glu, s5_b_glu, s5_w_out, w_o, ln2_g, ln2_b, ffn2_w_gate, ffn2_w_up, ffn2_w_down, ln3_g, ln3_b):
    rows = lambda v: v.reshape(DEPTH, 1, -1).astype(F32)
    bf = lambda w: w.astype(BF16)
    pad_rows = lambda a: jnp.pad(a, ((0, BLOCK_Q - N_META), (0, 0)))
    per_batch = lambda a: jnp.broadcast_to(a[None], (BATCH,) + a.shape)
    ffn1 = (bf(ffn1_w_gate), bf(ffn1_w_up), bf(ffn1_w_down), rows(ln1_g), rows(ln1_b))
    ffn2 = (bf(ffn2_w_gate), bf(ffn2_w_up), bf(ffn2_w_down), rows(ln3_g), rows(ln3_b))
    w_in_mla, w_in_mix = _split_w_in(w_in)
    wc, wqa, wqb, wuk, wuv, vones = jax.vmap(_qkv_weights)(w_in_mla, bf(mla_w_uq), bf(mla_w_ukv))
    qkv_w = (wc, rows(mla_q_norm_g), rows(mla_kv_norm_g), wqa, wqb, wuk, wuv, vones)
    bblk, ar, ai, cblk = jax.vmap(_s5_weights)(s5_a_re, s5_a_im, s5_log_dt, s5_b_re, s5_b_im, s5_c_re, s5_c_im)
    mix_w = (w_in_mix, conv_w.astype(F32), rows(conv_b), bf(conv_w_out),
             bblk, ar, ai, cblk, rows(s5_d), bf(s5_w_glu), rows(s5_b_glu), bf(s5_w_out),
             bf(mla_w_o), bf(w_o), rows(ln2_g), rows(ln2_b))

    h = x.astype(F32).reshape(BATCH * SEQ, D_MODEL)
    hm = meta.astype(F32)
    tab = _rope_tables(N_META + jnp.arange(SEQ))
    tab_m = _rope_tables(jnp.arange(N_META))
    for i in range(DEPTH):
        xm = _ffn_ln(hm, i, *ffn1)
        qm, km, vm = _qkv(xm[None], i, *qkv_w, tab_m)
        km, vm_t = pad_rows(km[0]), pad_rows(vm[0]).T
        att_m = _attention_meta(pad_rows(qm[0]), km, vm_t)[:N_META]
        hm8, state, halo = _mix(per_batch(xm), per_batch(att_m), jnp.zeros((BATCH, S5_COLS), F32),
                                jnp.zeros((_HALO, MIX_WIDTH), F32), i, *mix_w)

        xn3 = _ffn_ln(h, i, *ffn1).reshape(BATCH, SEQ, D_MODEL)
        q, k, v = _qkv(xn3, i, *qkv_w, tab)
        att = _attention(q, k, v, km, vm_t)
        h3, _, _ = _mix(xn3, att, state, halo, i, *mix_w)
        h = _ffn_ln(h3.reshape(BATCH * SEQ, D_MODEL), i, *ffn2)
        if i + 1 < DEPTH:
            hm = _ffn_ln(hm8[0], i, *ffn2)
    return h.reshape(BATCH, SEQ, D_MODEL)
```

```python
import functools
import math

import jax
import jax.numpy as jnp
from jax import lax
from jax.experimental import pallas as pl
from jax.experimental.pallas import tpu as pltpu

D_MODEL = 1024
BATCH = 8
SEQ = 2048
DEPTH = 2
N_META = 16
BLOCK_Q = 128
MIX_WIDTH = D_MODEL // 2
HEADS = 8
V_HEAD_DIM = 64
QK_NOPE_DIM = 64
QK_ROPE_DIM = 32
QK_HEAD_DIM = QK_NOPE_DIM + QK_ROPE_DIM
Q_LORA_RANK = 384
KV_LORA_RANK = 256
ROPE_BASE = 10000.0
S5_GROUP = 16
S5_GROUPS = 32
S5_STATE = 64
D_FF = 2816
ALPHA = (2.0 * DEPTH) ** 0.25
LN_EPS = 1e-5
RMS_EPS = 1e-6

HEAD_LANES = 128
QKV_WIDTH = HEADS * HEAD_LANES
TOK_TILE = 512
ATT_TILE = 2 * BLOCK_Q
N_SEQ_TILES = SEQ // ATT_TILE
V_ROWS = V_HEAD_DIM + 16
LOG2E = math.log2(math.e)
S5_BLOCKS = 4
S5_BLOCK_STATES = (S5_GROUPS // S5_BLOCKS) * S5_STATE
S5_COLS = 2 * S5_GROUPS * S5_STATE
MIX_STEPS = 64

_OFF_KR, _OFF_CONV = Q_LORA_RANK + KV_LORA_RANK, Q_LORA_RANK + KV_LORA_RANK + QK_ROPE_DIM

V7X_VMEM_BYTES = 64 * 1024 * 1024
VMEM_LIMIT = V7X_VMEM_BYTES * 7 // 8

F32 = jnp.float32
BF16 = jnp.bfloat16


def _dot(a, b):
    return jnp.dot(a, b, preferred_element_type=F32)


def _const_spec(shape):
    nd = len(shape)
    return pl.BlockSpec(shape, lambda *_: (0,) * nd, pipeline_mode=pl.Buffered(1))


def _layer_spec(stacked, layer):
    nd = stacked.ndim - 1
    return pl.BlockSpec((None,) + stacked.shape[1:], lambda *_: (layer,) + (0,) * nd,
                        pipeline_mode=pl.Buffered(1))


def _layer_norm(y, g, b):
    mu = jnp.mean(y, axis=-1, keepdims=True)
    yc = y - mu
    var = jnp.mean(yc * yc, axis=-1, keepdims=True)
    return yc * lax.rsqrt(var + LN_EPS) * g + b


def _rms_norm(y, g):
    return y * lax.rsqrt(jnp.mean(y * y, axis=-1, keepdims=True) + RMS_EPS) * g


def _sigmoid(y):
    return 1.0 / (1.0 + jnp.exp(-y))


def _gelu_tanh(y):
    return 0.5 * y * (1.0 + jnp.tanh(math.sqrt(2.0 / math.pi) * (y + 0.044715 * (y * y * y))))


def _ffn_rows(x, wg_ref, wu_ref, wd_ref, g_ref, b_ref):
    xb = x.astype(BF16)
    gate = _dot(xb, wg_ref[...])
    hmid = (gate * _sigmoid(gate) * _dot(xb, wu_ref[...])).astype(BF16)
    return _layer_norm(ALPHA * x + 0.5 * _dot(hmid, wd_ref[...]), g_ref[...], b_ref[...])


def _ffn_ln_kernel(x_ref, wg_ref, wu_ref, wd_ref, g_ref, b_ref, o_ref, *, sub_rows):
    for r in range(x_ref.shape[0] // sub_rows):
        rows = slice(r * sub_rows, (r + 1) * sub_rows)
        o_ref[rows, :] = _ffn_rows(x_ref[rows, :], wg_ref, wu_ref, wd_ref, g_ref, b_ref)


def _ffn_ln(x, layer, wg, wu, wd, g, b):
    rows = x.shape[0]
    tm = min(rows, 2 * TOK_TILE)
    flat = pl.BlockSpec((tm, D_MODEL), lambda i: (i, 0))
    return pl.pallas_call(
        functools.partial(_ffn_ln_kernel, sub_rows=min(tm, TOK_TILE)),
        grid=(rows // tm,),
        in_specs=[flat] + [_layer_spec(w, layer) for w in (wg, wu, wd, g, b)],
        out_specs=flat,
        out_shape=jax.ShapeDtypeStruct((rows, D_MODEL), F32),
        compiler_params=pltpu.CompilerParams(
            dimension_semantics=("arbitrary",), vmem_limit_bytes=VMEM_LIMIT),
        name="ffn_ln",
    )(x, wg, wu, wd, g, b)


_WC_WIDTH = Q_LORA_RANK + KV_LORA_RANK + 2 * HEAD_LANES


def _qkv_kernel(x_ref, wc_ref, gq_ref, gkv_ref, wqa_ref, wqb_ref, wuk_ref, wuv_ref, vones_ref, tab_ref,
                q_ref, k_ref, v_ref, *, transpose_v):
    xb = x_ref[...].astype(BF16)
    c = _dot(xb, wc_ref[...])
    c_q = c[:, :Q_LORA_RANK]
    c_kv = c[:, Q_LORA_RANK:Q_LORA_RANK + KV_LORA_RANK]
    k_r = c[:, _WC_WIDTH - 2 * HEAD_LANES:_WC_WIDTH - HEAD_LANES]
    k_r_rot = c[:, _WC_WIDTH - HEAD_LANES:]
    qn = _rms_norm(c_q, gq_ref[...]).astype(BF16)
    kvn = _rms_norm(c_kv, gkv_ref[...]).astype(BF16)
    cos_q = tab_ref[:, 0 * HEAD_LANES:1 * HEAD_LANES]
    sin_q = tab_ref[:, 1 * HEAD_LANES:2 * HEAD_LANES]
    cos_k = tab_ref[:, 2 * HEAD_LANES:3 * HEAD_LANES]
    sin_k = tab_ref[:, 3 * HEAD_LANES:4 * HEAD_LANES]
    k_rope = k_r * cos_k + k_r_rot * sin_k
    q_a = _dot(qn, wqa_ref[...])
    q_b = _dot(qn, wqb_ref[...])
    k_nope = _dot(kvn, wuk_ref[...])
    for h in range(HEADS):
        sl = slice(h * HEAD_LANES, (h + 1) * HEAD_LANES)
        q_ref[:, sl] = (q_a[:, sl] * cos_q + q_b[:, sl] * sin_q).astype(BF16)
        k_ref[:, sl] = (k_nope[:, sl] + k_rope).astype(BF16)
    v = _dot(kvn, wuv_ref[...]) + vones_ref[...]
    if transpose_v:
        for t in range(x_ref.shape[0] // ATT_TILE):
            v_ref[t] = v[t * ATT_TILE:(t + 1) * ATT_TILE, :].T.astype(BF16)
    else:
        v_ref[...] = v.astype(BF16)


def _qkv(xn, layer, wc, gq, gkv, wqa, wqb, wuk, wuv, vones, tab):
    n_seq, rows, _ = xn.shape
    tile = min(rows, 2 * TOK_TILE)
    transpose_v = tile % ATT_TILE == 0
    seq = lambda w: pl.BlockSpec((None, tile, w), lambda s, j: (s, j, 0))
    seq_shape = jax.ShapeDtypeStruct((n_seq, rows, QKV_WIDTH), BF16)
    if transpose_v:
        v_spec = pl.BlockSpec((None, tile // ATT_TILE, HEADS * V_ROWS, ATT_TILE), lambda s, j: (s, j, 0, 0))
        v_shape = jax.ShapeDtypeStruct((n_seq, rows // ATT_TILE, HEADS * V_ROWS, ATT_TILE), BF16)
    else:
        v_spec = seq(HEADS * V_ROWS)
        v_shape = jax.ShapeDtypeStruct((n_seq, rows, HEADS * V_ROWS), BF16)
    return pl.pallas_call(
        functools.partial(_qkv_kernel, transpose_v=transpose_v),
        grid=(n_seq, rows // tile),
        in_specs=[seq(D_MODEL)]
        + [_layer_spec(w, layer) for w in (wc, gq, gkv, wqa, wqb, wuk, wuv, vones)]
        + [pl.BlockSpec((tile, 4 * HEAD_LANES), lambda s, j: (j, 0))],
        out_specs=[seq(QKV_WIDTH), seq(QKV_WIDTH), v_spec],
        out_shape=[seq_shape, seq_shape, v_shape],
        compiler_params=pltpu.CompilerParams(
            dimension_semantics=("arbitrary", "arbitrary"), vmem_limit_bytes=VMEM_LIMIT),
        name="qkv_proj",
    )(xn, wc, gq, gkv, wqa, wqb, wuk, wuv, vones, tab)


def _scores(q_ref, k_ref, s_scr, smax_scr, slot, c, mask):
    k0 = c * ATT_TILE if isinstance(c, int) else pl.multiple_of(c * ATT_TILE, ATT_TILE)
    for h in range(HEADS):
        sl = slice(h * HEAD_LANES, (h + 1) * HEAD_LANES)
        s_t = lax.dot_general(
            k_ref[pl.ds(k0, ATT_TILE), sl], q_ref[:, sl], (((1,), (1,)), ((), ())),
            preferred_element_type=F32)
        if mask is not None:
            s_t = jnp.where(mask, s_t, -1e30)
        s_scr[slot, h] = s_t
        smax_scr[slot, h] = jnp.max(s_t, axis=0, keepdims=True)


def _attend(v_ref, s_scr, smax_scr, m_scr, acc_scr, slot, c):
    for h in range(HEADS):
        m_old = m_scr[h]
        m_new = jnp.maximum(m_old, smax_scr[slot, h])
        p_t = jnp.exp2(s_scr[slot, h] - m_new).astype(BF16)
        pv = _dot(v_ref[c, h * V_ROWS:(h + 1) * V_ROWS, :], p_t)
        acc_scr[h] = jnp.exp2(m_old - m_new) * acc_scr[h] + pv
        m_scr[h] = m_new


def _scores_meta(q_ref, km_ref, sm_scr, m_scr, mask):
    for h in range(HEADS):
        sl = slice(h * HEAD_LANES, (h + 1) * HEAD_LANES)
        s_t = lax.dot_general(
            km_ref[:, sl], q_ref[:, sl], (((1,), (1,)), ((), ())), preferred_element_type=F32)
        s_t = jnp.where(mask, s_t, -1e30)
        sm_scr[h] = s_t
        m_scr[h] = jnp.max(s_t, axis=0, keepdims=True)


def _attend_meta(vm_ref, sm_scr, m_scr, acc_scr):
    for h in range(HEADS):
        p_t = jnp.exp2(sm_scr[h] - m_scr[h]).astype(BF16)
        acc_scr[h] = _dot(vm_ref[h * V_ROWS:(h + 1) * V_ROWS, :], p_t)


def _attn_out(acc_scr):
    o_t = jnp.concatenate(
        [acc_scr[h, 0:V_HEAD_DIM, :] / acc_scr[h, V_HEAD_DIM:V_HEAD_DIM + 1, :] for h in range(HEADS)],
        axis=0)
    return o_t.T.astype(BF16)


def _attn_kernel(q_ref, k_ref, v_ref, km_ref, vm_ref, o_ref, s_scr, smax_scr, sm_scr, m_scr, acc_scr):
    i = pl.program_id(1)
    causal = (lax.broadcasted_iota(jnp.int32, (ATT_TILE, ATT_TILE), 0)
              <= lax.broadcasted_iota(jnp.int32, (ATT_TILE, ATT_TILE), 1))
    meta_mask = lax.broadcasted_iota(jnp.int32, (BLOCK_Q, ATT_TILE), 0) < N_META

    scores = functools.partial(_scores, q_ref, k_ref, s_scr, smax_scr)
    attend = functools.partial(_attend, v_ref, s_scr, smax_scr, m_scr, acc_scr)
    _scores_meta(q_ref, km_ref, sm_scr, m_scr, meta_mask)
    scores(0, 0, causal | (i > 0))
    _attend_meta(vm_ref, sm_scr, m_scr, acc_scr)

    def two_tiles(pp, carry):
        p = 1 + 2 * pp
        scores(1, p, None)
        attend(0, p - 1)
        scores(0, p + 1, None)
        attend(1, p)
        return carry

    lax.fori_loop(0, (i - 1) // 2, two_tiles, 0)

    @pl.when(i == 0)
    def _():
        attend(0, 0)

    @pl.when(i % 2 == 1)
    def _():
        scores(1, i, causal)
        attend(0, i - 1)
        attend(1, i)

    @pl.when((i % 2 == 0) & (i > 0))
    def _():
        scores(1, i - 1, None)
        attend(0, i - 2)
        scores(0, i, causal)
        attend(1, i - 1)
        attend(0, i)

    o_ref[...] = _attn_out(acc_scr)


def _attention(q, k, v_t, k_meta, v_meta_t):
    return pl.pallas_call(
        _attn_kernel,
        grid=(BATCH, N_SEQ_TILES),
        in_specs=[
            pl.BlockSpec((None, ATT_TILE, QKV_WIDTH), lambda s, i: (s, i, 0)),
            pl.BlockSpec((None, SEQ, QKV_WIDTH), lambda s, i: (s, 0, 0)),
            pl.BlockSpec((None, N_SEQ_TILES, HEADS * V_ROWS, ATT_TILE), lambda s, i: (s, 0, 0, 0)),
            _const_spec((BLOCK_Q, QKV_WIDTH)),
            _const_spec((HEADS * V_ROWS, BLOCK_Q)),
        ],
        out_specs=pl.BlockSpec((None, ATT_TILE, MIX_WIDTH), lambda s, i: (s, i, 0)),
        out_shape=jax.ShapeDtypeStruct((BATCH, SEQ, MIX_WIDTH), BF16),
        scratch_shapes=[pltpu.VMEM((2, HEADS, ATT_TILE, ATT_TILE), F32),
                        pltpu.VMEM((2, HEADS, 1, ATT_TILE), F32),
                        pltpu.VMEM((HEADS, BLOCK_Q, ATT_TILE), F32),
                        pltpu.VMEM((HEADS, 1, ATT_TILE), F32),
                        pltpu.VMEM((HEADS, V_ROWS, ATT_TILE), F32)],
        compiler_params=pltpu.CompilerParams(
            dimension_semantics=("arbitrary", "arbitrary"), vmem_limit_bytes=VMEM_LIMIT),
        name="mla_attention",
    )(q, k, v_t, k_meta, v_meta_t)


def _attn_meta_kernel(q_ref, km_ref, vm_ref, o_ref, sm_scr, m_scr, acc_scr):
    k_off = lax.broadcasted_iota(jnp.int32, (BLOCK_Q, BLOCK_Q), 0)
    q_off = lax.broadcasted_iota(jnp.int32, (BLOCK_Q, BLOCK_Q), 1)
    _scores_meta(q_ref, km_ref, sm_scr, m_scr, (k_off < N_META) & (k_off <= q_off))
    _attend_meta(vm_ref, sm_scr, m_scr, acc_scr)
    o_ref[...] = _attn_out(acc_scr)


def _attention_meta(q_meta, k_meta, v_meta_t):
    return pl.pallas_call(
        _attn_meta_kernel,
        out_shape=jax.ShapeDtypeStruct((BLOCK_Q, MIX_WIDTH), BF16),
        scratch_shapes=[pltpu.VMEM((HEADS, BLOCK_Q, BLOCK_Q), F32),
                        pltpu.VMEM((HEADS, 1, BLOCK_Q), F32),
                        pltpu.VMEM((HEADS, V_ROWS, BLOCK_Q), F32)],
        name="mla_attention_meta",
    )(q_meta, k_meta, v_meta_t)


_HALO = 2 * BATCH
_D_SLABS = D_MODEL // HEAD_LANES


def _to_time_major(src_ref, slab_ref, width, steps):
    n = width // HEAD_LANES
    for b in range(BATCH):
        for j in range(n):
            slab_ref[j, pl.ds(b, steps, stride=BATCH), :] = (
                src_ref[b, :, j * HEAD_LANES:(j + 1) * HEAD_LANES].astype(F32))
    return jnp.concatenate([slab_ref[j] for j in range(n)], axis=1)


def _mix_kernel(x_ref, att_ref, st0_ref, halo0_ref, win_ref, cw_ref, cb_ref, cwo_ref, bblk_ref, ar_ref,
                ai_ref, cblk_ref, d_ref, wglu_ref, bglu_ref, swo_ref, mwo_ref, wo_ref, g_ref, b_ref,
                out_ref, st_out_ref, halo_out_ref, bu_scr, st_scr, cbuf_scr, slab_ref, *, steps):
    tm = steps * BATCH
    W = MIX_WIDTH

    @pl.when(pl.program_id(0) == 0)
    def _():
        st_scr[...] = st0_ref[...]
        cbuf_scr[0:_HALO, :] = halo0_ref[...]

    x = _to_time_major(x_ref, slab_ref, D_MODEL, steps)
    xb = x.astype(BF16)

    us = _dot(xb, win_ref[:, 3 * W:4 * W])
    usb = us.astype(BF16)
    nb = S5_BLOCK_STATES
    for blk in range(S5_BLOCKS):
        bu_scr[:, 2 * nb * blk:2 * nb * (blk + 1)] = _dot(
            usb[:, blk * 128:(blk + 1) * 128], bblk_ref[blk])
    for blk in range(S5_BLOCKS):
        c_re = 2 * nb * blk
        c_im = c_re + nb
        a_re = ar_ref[blk]
        a_im = ai_ref[blk]

        def step(t, carry, c_re=c_re, c_im=c_im, a_re=a_re, a_im=a_im):
            s_re, s_im = carry
            r0 = pl.multiple_of(t * BATCH, BATCH)
            n_re = a_re * s_re - a_im * s_im + bu_scr[pl.ds(r0, BATCH), c_re:c_re + nb]
            n_im = a_re * s_im + a_im * s_re + bu_scr[pl.ds(r0, BATCH), c_im:c_im + nb]
            bu_scr[pl.ds(r0, BATCH), c_re:c_re + nb] = n_re
            bu_scr[pl.ds(r0, BATCH), c_im:c_im + nb] = n_im
            return n_re, n_im

        s_re, s_im = lax.fori_loop(
            0, steps, step, (st_scr[:, c_re:c_re + nb], st_scr[:, c_im:c_im + nb]), unroll=True)
        st_scr[:, c_re:c_re + nb] = s_re
        st_scr[:, c_im:c_im + nb] = s_im

    pc = _dot(xb, win_ref[:, 0:3 * W])
    u = pc[:, 2 * W:3 * W] * pc[:, 0:W]
    cbuf_scr[_HALO:_HALO + tm, :] = u
    y = (cb_ref[...] + cw_ref[0:1, :] * cbuf_scr[0:tm, :]
         + cw_ref[1:2, :] * cbuf_scr[BATCH:BATCH + tm, :] + cw_ref[2:3, :] * u)
    cbuf_scr[0:_HALO, :] = u[tm - _HALO:, :]
    y_b = _dot((pc[:, W:2 * W] * y).astype(BF16), cwo_ref[...])

    att = _to_time_major(att_ref, slab_ref, W, steps).astype(BF16)
    y_a = _dot(att, mwo_ref[...])
    g0 = 4 * W
    mixed = _sigmoid(_dot(xb, win_ref[:, g0:g0 + D_MODEL])) * y_a
    mixed += _sigmoid(_dot(xb, win_ref[:, g0 + D_MODEL:g0 + 2 * D_MODEL])) * y_b
    gate_c = _sigmoid(_dot(xb, win_ref[:, g0 + 2 * D_MODEL:g0 + 3 * D_MODEL]))

    y = jnp.concatenate(
        [_dot(bu_scr[:, 2 * nb * blk:2 * nb * (blk + 1)].astype(BF16), cblk_ref[blk])
         for blk in range(S5_BLOCKS)], axis=1)
    y = _gelu_tanh(y + d_ref[...] * us)
    y = y * _sigmoid(_dot(y.astype(BF16), wglu_ref[...]) + bglu_ref[...])
    mixed += gate_c * _dot(y.astype(BF16), swo_ref[...])

    half = tm // 2
    for r in range(2):
        rows = slice(r * half, (r + 1) * half)
        z = _layer_norm(ALPHA * x[rows] + _dot(mixed[rows].astype(BF16), wo_ref[...]),
                        g_ref[...], b_ref[...])
        for j in range(_D_SLABS):
            slab_ref[j, rows, :] = z[:, j * HEAD_LANES:(j + 1) * HEAD_LANES]
    for b in range(BATCH):
        for j in range(_D_SLABS):
            out_ref[b, :, j * HEAD_LANES:(j + 1) * HEAD_LANES] = slab_ref[
                j, pl.ds(b, steps, stride=BATCH), :]
    st_out_ref[...] = st_scr[...]
    halo_out_ref[...] = cbuf_scr[0:_HALO, :]


def _mix(xn, att, st0, halo0, layer, *weights):
    t_len = xn.shape[1]
    steps = min(t_len, MIX_STEPS)
    tm = steps * BATCH
    tile = lambda w: pl.BlockSpec((BATCH, steps, w), lambda i: (0, i, 0))
    return pl.pallas_call(
        functools.partial(_mix_kernel, steps=steps),
        grid=(t_len // steps,),
        in_specs=[tile(D_MODEL), tile(MIX_WIDTH), _const_spec(st0.shape), _const_spec(halo0.shape)]
        + [_layer_spec(w, layer) for w in weights],
        out_specs=[tile(D_MODEL), pl.BlockSpec((BATCH, S5_COLS), lambda i: (0, 0)),
                   pl.BlockSpec((_HALO, MIX_WIDTH), lambda i: (0, 0))],
        out_shape=[jax.ShapeDtypeStruct((BATCH, t_len, D_MODEL), F32),
                   jax.ShapeDtypeStruct((BATCH, S5_COLS), F32),
                   jax.ShapeDtypeStruct((_HALO, MIX_WIDTH), F32)],
        scratch_shapes=[
            pltpu.VMEM((tm, S5_COLS), F32),
            pltpu.VMEM((BATCH, S5_COLS), F32),
            pltpu.VMEM((tm + _HALO, MIX_WIDTH), F32),
            pltpu.VMEM((_D_SLABS, tm, HEAD_LANES), F32),
        ],
        compiler_params=pltpu.CompilerParams(
            dimension_semantics=("arbitrary",), vmem_limit_bytes=VMEM_LIMIT),
        name="mixers_merge",
    )(xn, att, st0, halo0, *weights)


def _rope_tables(pos):
    n = pos.shape[0]
    inv_freq = ROPE_BASE ** (-jnp.arange(0, QK_ROPE_DIM, 2, dtype=F32) / QK_ROPE_DIM)
    ang = pos.astype(F32)[:, None] * inv_freq[None, :]
    cos2 = jnp.tile(jnp.cos(ang), (1, 2))
    sin2 = jnp.tile(jnp.sin(ang), (1, 2))
    zn = jnp.zeros((n, QK_NOPE_DIM), F32)
    zt = jnp.zeros((n, HEAD_LANES - QK_HEAD_DIM), F32)
    scale = QK_HEAD_DIM ** -0.5 * LOG2E
    cos_q = scale * jnp.concatenate([jnp.ones_like(zn), cos2, zt], axis=1)
    sin_q = scale * jnp.concatenate([zn, sin2, zt], axis=1)
    cos_k = jnp.concatenate([zn, cos2, zt], axis=1)
    sin_k = jnp.concatenate([zn, sin2, zt], axis=1)
    return jnp.concatenate([cos_q, sin_q, cos_k, sin_k], axis=1)


def _rot_half_cols(w):
    half = QK_ROPE_DIM // 2
    return jnp.concatenate([-w[..., half:], w[..., :half]], axis=-1)


def _qkv_weights(w_in, w_uq, w_ukv):
    zeros = lambda *shape: jnp.zeros(shape, BF16)
    kr = w_in[:, _OFF_KR:_OFF_CONV]
    tail = HEAD_LANES - QK_HEAD_DIM
    wc = jnp.concatenate([
        w_in[:, :_OFF_KR],
        zeros(D_MODEL, QK_NOPE_DIM), kr, zeros(D_MODEL, tail),
        zeros(D_MODEL, QK_NOPE_DIM), _rot_half_cols(kr), zeros(D_MODEL, tail)], axis=1)
    uq = w_uq.reshape(Q_LORA_RANK, HEADS, QK_HEAD_DIM)
    wqa = jnp.concatenate([uq, zeros(Q_LORA_RANK, HEADS, tail)], axis=-1)
    wqb = jnp.concatenate([zeros(Q_LORA_RANK, HEADS, QK_NOPE_DIM), _rot_half_cols(uq[..., QK_NOPE_DIM:]),
                           zeros(Q_LORA_RANK, HEADS, tail)], axis=-1)
    ukv = w_ukv.reshape(KV_LORA_RANK, HEADS, QK_NOPE_DIM + V_HEAD_DIM)
    wuk = jnp.concatenate([ukv[..., :QK_NOPE_DIM], zeros(KV_LORA_RANK, HEADS, HEAD_LANES - QK_NOPE_DIM)],
                          axis=-1)
    wuv = jnp.concatenate([ukv[..., QK_NOPE_DIM:], zeros(KV_LORA_RANK, HEADS, V_ROWS - V_HEAD_DIM)], axis=-1)
    vones = jnp.zeros((HEADS, V_ROWS), F32).at[:, V_HEAD_DIM].set(1.0).reshape(1, HEADS * V_ROWS)
    flat = lambda w: w.reshape(w.shape[0], -1)
    return wc, flat(wqa), flat(wqb), flat(wuk), flat(wuv), vones


def _s5_weights(a_re, a_im, log_dt, b_re, b_im, c_re, c_im):
    dt = jnp.exp(log_dt)[:, None]
    mag = jnp.exp(dt * a_re)
    ab_re, ab_im = mag * jnp.cos(dt * a_im), mag * jnp.sin(dt * a_im)
    den = a_re * a_re + a_im * a_im
    nr, ni = ab_re - 1.0, ab_im
    coef_re = (nr * a_re + ni * a_im) / den
    coef_im = (ni * a_re - nr * a_im) / den
    bb_re = coef_re[..., None] * b_re - coef_im[..., None] * b_im
    bb_im = coef_re[..., None] * b_im + coef_im[..., None] * b_re
    gpb = S5_GROUPS // S5_BLOCKS
    eye = jnp.eye(gpb, dtype=F32)

    def in_blocks(bb):
        t = bb.transpose(0, 2, 1).reshape(S5_BLOCKS, gpb, S5_GROUP, S5_STATE)
        return jnp.einsum('bghn,gk->bghkn', t, eye).reshape(S5_BLOCKS, gpb * S5_GROUP, gpb * S5_STATE)

    def out_blocks(cc):
        t = cc.reshape(S5_BLOCKS, gpb, S5_GROUP, S5_STATE)
        return jnp.einsum('bghn,gk->bgnkh', t, eye).reshape(S5_BLOCKS, gpb * S5_STATE, gpb * S5_GROUP)

    bblk = jnp.concatenate([in_blocks(bb_re), in_blocks(bb_im)], axis=2).astype(BF16)
    cblk = jnp.concatenate([out_blocks(c_re), out_blocks(-c_im)], axis=1).astype(BF16)
    bcast = lambda a: jnp.broadcast_to(
        a.reshape(S5_BLOCKS, 1, S5_BLOCK_STATES), (S5_BLOCKS, BATCH, S5_BLOCK_STATES))
    return bblk, bcast(ab_re), bcast(ab_im), cblk


def kernel(x, meta, ffn1_w_gate, ffn1_w_up, ffn1_w_down, ln1_g, ln1_b, w_in, mla_q_norm_g, mla_w_uq, mla_kv_norm_g, mla_w_ukv, mla_w_o, conv_w, conv_b, conv_w_out, s5_a_re, s5_a_im, s5_log_dt, s5_b_re, s5_b_im, s5_c_re, s5_c_im, s5_d, s5_w_glu, s5_b_glu, s5_w_out, w_o, ln2_g, ln2_b, ffn2_w_gate, ffn2_w_up, ffn2_w_down, ln3_g, ln3_b):
    rows = lambda v: v.reshape(DEPTH, 1, -1).astype(F32)
    bf = lambda w: w.astype(BF16)
    pad_rows = lambda a: jnp.pad(a, ((0, BLOCK_Q - N_META), (0, 0)))
    per_batch = lambda a: jnp.broadcast_to(a[None], (BATCH,) + a.shape)
    ffn1 = (bf(ffn1_w_gate), bf(ffn1_w_up), bf(ffn1_w_down), rows(ln1_g), rows(ln1_b))
    ffn2 = (bf(ffn2_w_gate), bf(ffn2_w_up), bf(ffn2_w_down), rows(ln3_g), rows(ln3_b))
    w_in_b = bf(w_in)
    wc, wqa, wqb, wuk, wuv, vones = jax.vmap(_qkv_weights)(w_in_b, bf(mla_w_uq), bf(mla_w_ukv))
    qkv_w = (wc, rows(mla_q_norm_g), rows(mla_kv_norm_g), wqa, wqb, wuk, wuv, vones)
    bblk, ar, ai, cblk = jax.vmap(_s5_weights)(s5_a_re, s5_a_im, s5_log_dt, s5_b_re, s5_b_im, s5_c_re, s5_c_im)
    mix_w = (w_in_b[:, :, _OFF_CONV:], conv_w.astype(F32), rows(conv_b), bf(conv_w_out),
             bblk, ar, ai, cblk, rows(s5_d), bf(s5_w_glu), rows(s5_b_glu), bf(s5_w_out),
             bf(mla_w_o), bf(w_o), rows(ln2_g), rows(ln2_b))

    h = x.astype(F32).reshape(BATCH * SEQ, D_MODEL)
    hm = meta.astype(F32)
    tab = _rope_tables(N_META + jnp.arange(SEQ))
    tab_m = _rope_tables(jnp.arange(N_META))
    for i in range(DEPTH):
        xm = _ffn_ln(hm, i, *ffn1)
        qm, km, vm = _qkv(xm[None], i, *qkv_w, tab_m)
        km, vm_t = pad_rows(km[0]), pad_rows(vm[0]).T
        att_m = _attention_meta(pad_rows(qm[0]), km, vm_t)[:N_META]
        hm8, state, halo = _mix(per_batch(xm), per_batch(att_m), jnp.zeros((BATCH, S5_COLS), F32),
                                jnp.zeros((_HALO, MIX_WIDTH), F32), i, *mix_w)

        xn3 = _ffn_ln(h, i, *ffn1).reshape(BATCH, SEQ, D_MODEL)
        q, k, v = _qkv(xn3, i, *qkv_w, tab)
        att = _attention(q, k, v, km, vm_t)
        h3, _, _ = _mix(xn3, att, state, halo, i, *mix_w)
        h = _ffn_ln(h3.reshape(BATCH * SEQ, D_MODEL), i, *ffn2)
        if i + 1 < DEPTH:
            hm = _ffn_ln(hm8[0], i, *ffn2)
    return h.reshape(BATCH, SEQ, D_MODEL)
```

```python
import functools
import math

import jax
import jax.numpy as jnp
from jax import lax
from jax.experimental import pallas as pl
from jax.experimental.pallas import tpu as pltpu

D_MODEL = 1024
BATCH = 8
SEQ = 2048
DEPTH = 2
N_META = 16
BLOCK_Q = 128
MIX_WIDTH = D_MODEL // 2
HEADS = 8
V_HEAD_DIM = 64
QK_NOPE_DIM = 64
QK_ROPE_DIM = 32
QK_HEAD_DIM = QK_NOPE_DIM + QK_ROPE_DIM
Q_LORA_RANK = 384
KV_LORA_RANK = 256
ROPE_BASE = 10000.0
S5_GROUP = 16
S5_GROUPS = 32
S5_STATE = 64
D_FF = 2816
ALPHA = (2.0 * DEPTH) ** 0.25
LN_EPS = 1e-5
RMS_EPS = 1e-6

HEAD_LANES = 128
QKV_WIDTH = HEADS * HEAD_LANES
TOK_TILE = 512
ATT_TILE = 2 * BLOCK_Q
N_SEQ_TILES = SEQ // ATT_TILE
V_ROWS = V_HEAD_DIM + 16
LOG2E = math.log2(math.e)
S5_BLOCKS = 4
S5_BLOCK_STATES = (S5_GROUPS // S5_BLOCKS) * S5_STATE
S5_COLS = 2 * S5_GROUPS * S5_STATE
MIX_STEPS = 64

_OFF_KR, _OFF_CONV = Q_LORA_RANK + KV_LORA_RANK, Q_LORA_RANK + KV_LORA_RANK + QK_ROPE_DIM

V7X_VMEM_BYTES = 64 * 1024 * 1024
VMEM_LIMIT = V7X_VMEM_BYTES * 7 // 8

F32 = jnp.float32
BF16 = jnp.bfloat16


def _dot(a, b):
    return jnp.dot(a, b, preferred_element_type=F32)


def _const_spec(shape):
    nd = len(shape)
    return pl.BlockSpec(shape, lambda *_: (0,) * nd, pipeline_mode=pl.Buffered(1))


def _layer_spec(stacked, layer):
    nd = stacked.ndim - 1
    return pl.BlockSpec((None,) + stacked.shape[1:], lambda *_: (layer,) + (0,) * nd,
                        pipeline_mode=pl.Buffered(1))


def _layer_norm(y, g, b):
    mu = jnp.mean(y, axis=-1, keepdims=True)
    yc = y - mu
    var = jnp.mean(yc * yc, axis=-1, keepdims=True)
    return yc * lax.rsqrt(var + LN_EPS) * g + b


def _rms_norm(y, g):
    return y * lax.rsqrt(jnp.mean(y * y, axis=-1, keepdims=True) + RMS_EPS) * g


def _sigmoid(y):
    return 1.0 / (1.0 + jnp.exp(-y))


def _gelu_tanh(y):
    return 0.5 * y * (1.0 + jnp.tanh(math.sqrt(2.0 / math.pi) * (y + 0.044715 * (y * y * y))))


def _ffn_rows(x, wg_ref, wu_ref, wd_ref, g_ref, b_ref):
    xb = x.astype(BF16)
    gate = _dot(xb, wg_ref[...])
    hmid = (gate * _sigmoid(gate) * _dot(xb, wu_ref[...])).astype(BF16)
    return _layer_norm(ALPHA * x + 0.5 * _dot(hmid, wd_ref[...]), g_ref[...], b_ref[...])


def _ffn_ln_kernel(x_ref, wg_ref, wu_ref, wd_ref, g_ref, b_ref, o_ref, *, sub_rows):
    for r in range(x_ref.shape[0] // sub_rows):
        rows = slice(r * sub_rows, (r + 1) * sub_rows)
        o_ref[rows, :] = _ffn_rows(x_ref[rows, :], wg_ref, wu_ref, wd_ref, g_ref, b_ref)


def _ffn_ln(x, layer, wg, wu, wd, g, b):
    rows = x.shape[0]
    tm = min(rows, 2 * TOK_TILE)
    flat = pl.BlockSpec((tm, D_MODEL), lambda i: (i, 0))
    return pl.pallas_call(
        functools.partial(_ffn_ln_kernel, sub_rows=min(tm, TOK_TILE)),
        grid=(rows // tm,),
        in_specs=[flat] + [_layer_spec(w, layer) for w in (wg, wu, wd, g, b)],
        out_specs=flat,
        out_shape=jax.ShapeDtypeStruct((rows, D_MODEL), F32),
        compiler_params=pltpu.CompilerParams(
            dimension_semantics=("arbitrary",), vmem_limit_bytes=VMEM_LIMIT),
        name="ffn_ln",
    )(x, wg, wu, wd, g, b)


_WC_WIDTH = Q_LORA_RANK + KV_LORA_RANK + 2 * HEAD_LANES


def _qkv_kernel(x_ref, wc_ref, gq_ref, gkv_ref, wqa_ref, wqb_ref, wuk_ref, wuv_ref, vones_ref, tab_ref,
                q_ref, k_ref, v_ref, *, transpose_v):
    xb = x_ref[...].astype(BF16)
    c = _dot(xb, wc_ref[...])
    c_q = c[:, :Q_LORA_RANK]
    c_kv = c[:, Q_LORA_RANK:Q_LORA_RANK + KV_LORA_RANK]
    k_r = c[:, _WC_WIDTH - 2 * HEAD_LANES:_WC_WIDTH - HEAD_LANES]
    k_r_rot = c[:, _WC_WIDTH - HEAD_LANES:]
    qn = _rms_norm(c_q, gq_ref[...]).astype(BF16)
    kvn = _rms_norm(c_kv, gkv_ref[...]).astype(BF16)
    cos_q = tab_ref[:, 0 * HEAD_LANES:1 * HEAD_LANES]
    sin_q = tab_ref[:, 1 * HEAD_LANES:2 * HEAD_LANES]
    cos_k = tab_ref[:, 2 * HEAD_LANES:3 * HEAD_LANES]
    sin_k = tab_ref[:, 3 * HEAD_LANES:4 * HEAD_LANES]
    k_rope = k_r * cos_k + k_r_rot * sin_k
    q_a = _dot(qn, wqa_ref[...])
    q_b = _dot(qn, wqb_ref[...])
    k_nope = _dot(kvn, wuk_ref[...])
    for h in range(HEADS):
        sl = slice(h * HEAD_LANES, (h + 1) * HEAD_LANES)
        q_ref[:, sl] = (q_a[:, sl] * cos_q + q_b[:, sl] * sin_q).astype(BF16)
        k_ref[:, sl] = (k_nope[:, sl] + k_rope).astype(BF16)
    v = _dot(kvn, wuv_ref[...]) + vones_ref[...]
    if transpose_v:
        for t in range(x_ref.shape[0] // ATT_TILE):
            v_ref[t] = v[t * ATT_TILE:(t + 1) * ATT_TILE, :].T.astype(BF16)
    else:
        v_ref[...] = v.astype(BF16)


def _qkv(xn, layer, wc, gq, gkv, wqa, wqb, wuk, wuv, vones, tab):
    n_seq, rows, _ = xn.shape
    tile = min(rows, 2 * TOK_TILE)
    transpose_v = tile % ATT_TILE == 0
    seq = lambda w: pl.BlockSpec((None, tile, w), lambda s, j: (s, j, 0))
    seq_shape = jax.ShapeDtypeStruct((n_seq, rows, QKV_WIDTH), BF16)
    if transpose_v:
        v_spec = pl.BlockSpec((None, tile // ATT_TILE, HEADS * V_ROWS, ATT_TILE), lambda s, j: (s, j, 0, 0))
        v_shape = jax.ShapeDtypeStruct((n_seq, rows // ATT_TILE, HEADS * V_ROWS, ATT_TILE), BF16)
    else:
        v_spec = seq(HEADS * V_ROWS)
        v_shape = jax.ShapeDtypeStruct((n_seq, rows, HEADS * V_ROWS), BF16)
    return pl.pallas_call(
        functools.partial(_qkv_kernel, transpose_v=transpose_v),
        grid=(n_seq, rows // tile),
        in_specs=[seq(D_MODEL)]
        + [_layer_spec(w, layer) for w in (wc, gq, gkv, wqa, wqb, wuk, wuv, vones)]
        + [pl.BlockSpec((tile, 4 * HEAD_LANES), lambda s, j: (j, 0))],
        out_specs=[seq(QKV_WIDTH), seq(QKV_WIDTH), v_spec],
        out_shape=[seq_shape, seq_shape, v_shape],
        compiler_params=pltpu.CompilerParams(
            dimension_semantics=("arbitrary", "arbitrary"), vmem_limit_bytes=VMEM_LIMIT),
        name="qkv_proj",
    )(xn, wc, gq, gkv, wqa, wqb, wuk, wuv, vones, tab)


_HEAD_GROUPS = tuple((h,) for h in range(HEADS))


def _scores(q_ref, k_ref, s_scr, smax_scr, slot, c, mask, heads=range(HEADS)):
    k0 = c * ATT_TILE if isinstance(c, int) else pl.multiple_of(c * ATT_TILE, ATT_TILE)
    for h in heads:
        sl = slice(h * HEAD_LANES, (h + 1) * HEAD_LANES)
        s_t = lax.dot_general(
            k_ref[pl.ds(k0, ATT_TILE), sl], q_ref[:, sl], (((1,), (1,)), ((), ())),
            preferred_element_type=F32)
        if mask is not None:
            s_t = jnp.where(mask, s_t, -1e30)
        s_scr[slot, h] = s_t
        smax_scr[slot, h] = jnp.max(s_t, axis=0, keepdims=True)


def _attend(v_ref, s_scr, smax_scr, m_scr, acc_scr, slot, c, heads=range(HEADS)):
    for h in heads:
        m_old = m_scr[h]
        m_new = jnp.maximum(m_old, smax_scr[slot, h])
        p_t = jnp.exp2(s_scr[slot, h] - m_new).astype(BF16)
        pv = _dot(v_ref[c, h * V_ROWS:(h + 1) * V_ROWS, :], p_t)
        acc_scr[h] = jnp.exp2(m_old - m_new) * acc_scr[h] + pv
        m_scr[h] = m_new


def _scores_meta(q_ref, km_ref, sm_scr, m_scr, mask):
    for h in range(HEADS):
        sl = slice(h * HEAD_LANES, (h + 1) * HEAD_LANES)
        s_t = lax.dot_general(
            km_ref[:, sl], q_ref[:, sl], (((1,), (1,)), ((), ())), preferred_element_type=F32)
        s_t = jnp.where(mask, s_t, -1e30)
        sm_scr[h] = s_t
        m_scr[h] = jnp.max(s_t, axis=0, keepdims=True)


def _attend_meta(vm_ref, sm_scr, m_scr, acc_scr):
    for h in range(HEADS):
        p_t = jnp.exp2(sm_scr[h] - m_scr[h]).astype(BF16)
        acc_scr[h] = _dot(vm_ref[h * V_ROWS:(h + 1) * V_ROWS, :], p_t)


def _attn_out(acc_scr):
    o_t = jnp.concatenate(
        [acc_scr[h, 0:V_HEAD_DIM, :] / acc_scr[h, V_HEAD_DIM:V_HEAD_DIM + 1, :] for h in range(HEADS)],
        axis=0)
    return o_t.T.astype(BF16)


def _attn_kernel(q_ref, k_ref, v_ref, km_ref, vm_ref, o_ref, s_scr, smax_scr, sm_scr, m_scr, acc_scr):
    i = pl.program_id(1)
    causal = (lax.broadcasted_iota(jnp.int32, (ATT_TILE, ATT_TILE), 0)
              <= lax.broadcasted_iota(jnp.int32, (ATT_TILE, ATT_TILE), 1))
    meta_mask = lax.broadcasted_iota(jnp.int32, (BLOCK_Q, ATT_TILE), 0) < N_META

    scores = functools.partial(_scores, q_ref, k_ref, s_scr, smax_scr)
    attend = functools.partial(_attend, v_ref, s_scr, smax_scr, m_scr, acc_scr)
    def overlapped(next_slot, next_tile, mask, slot, tile):
        for hs in _HEAD_GROUPS:
            scores(next_slot, next_tile, mask, hs)
            attend(slot, tile, hs)

    _scores_meta(q_ref, km_ref, sm_scr, m_scr, meta_mask)
    scores(0, 0, causal | (i > 0))
    _attend_meta(vm_ref, sm_scr, m_scr, acc_scr)

    def two_tiles(pp, carry):
        p = 1 + 2 * pp
        overlapped(1, p, None, 0, p - 1)
        overlapped(0, p + 1, None, 1, p)
        return carry

    lax.fori_loop(0, (i - 1) // 2, two_tiles, 0)

    @pl.when(i == 0)
    def _():
        attend(0, 0)

    @pl.when(i % 2 == 1)
    def _():
        overlapped(1, i, causal, 0, i - 1)
        attend(1, i)

    @pl.when((i % 2 == 0) & (i > 0))
    def _():
        overlapped(1, i - 1, None, 0, i - 2)
        overlapped(0, i, causal, 1, i - 1)
        attend(0, i)

    o_ref[...] = _attn_out(acc_scr)


def _attention(q, k, v_t, k_meta, v_meta_t):
    return pl.pallas_call(
        _attn_kernel,
        grid=(BATCH, N_SEQ_TILES),
        in_specs=[
            pl.BlockSpec((None, ATT_TILE, QKV_WIDTH), lambda s, i: (s, i, 0)),
            pl.BlockSpec((None, SEQ, QKV_WIDTH), lambda s, i: (s, 0, 0)),
            pl.BlockSpec((None, N_SEQ_TILES, HEADS * V_ROWS, ATT_TILE), lambda s, i: (s, 0, 0, 0)),
            _const_spec((BLOCK_Q, QKV_WIDTH)),
            _const_spec((HEADS * V_ROWS, BLOCK_Q)),
        ],
        out_specs=pl.BlockSpec((None, ATT_TILE, MIX_WIDTH), lambda s, i: (s, i, 0)),
        out_shape=jax.ShapeDtypeStruct((BATCH, SEQ, MIX_WIDTH), BF16),
        scratch_shapes=[pltpu.VMEM((2, HEADS, ATT_TILE, ATT_TILE), F32),
                        pltpu.VMEM((2, HEADS, 1, ATT_TILE), F32),
                        pltpu.VMEM((HEADS, BLOCK_Q, ATT_TILE), F32),
                        pltpu.VMEM((HEADS, 1, ATT_TILE), F32),
                        pltpu.VMEM((HEADS, V_ROWS, ATT_TILE), F32)],
        compiler_params=pltpu.CompilerParams(
            dimension_semantics=("arbitrary", "arbitrary"), vmem_limit_bytes=VMEM_LIMIT),
        name="mla_attention",
    )(q, k, v_t, k_meta, v_meta_t)


def _attn_meta_kernel(q_ref, km_ref, vm_ref, o_ref, sm_scr, m_scr, acc_scr):
    k_off = lax.broadcasted_iota(jnp.int32, (BLOCK_Q, BLOCK_Q), 0)
    q_off = lax.broadcasted_iota(jnp.int32, (BLOCK_Q, BLOCK_Q), 1)
    _scores_meta(q_ref, km_ref, sm_scr, m_scr, (k_off < N_META) & (k_off <= q_off))
    _attend_meta(vm_ref, sm_scr, m_scr, acc_scr)
    o_ref[...] = _attn_out(acc_scr)


def _attention_meta(q_meta, k_meta, v_meta_t):
    return pl.pallas_call(
        _attn_meta_kernel,
        out_shape=jax.ShapeDtypeStruct((BLOCK_Q, MIX_WIDTH), BF16),
        scratch_shapes=[pltpu.VMEM((HEADS, BLOCK_Q, BLOCK_Q), F32),
                        pltpu.VMEM((HEADS, 1, BLOCK_Q), F32),
                        pltpu.VMEM((HEADS, V_ROWS, BLOCK_Q), F32)],
        name="mla_attention_meta",
    )(q_meta, k_meta, v_meta_t)


_HALO = 2 * BATCH
_D_SLABS = D_MODEL // HEAD_LANES


def _to_time_major(src_ref, slab_ref, width, steps):
    n = width // HEAD_LANES
    for b in range(BATCH):
        for j in range(n):
            slab_ref[j, pl.ds(b, steps, stride=BATCH), :] = (
                src_ref[b, :, j * HEAD_LANES:(j + 1) * HEAD_LANES].astype(F32))
    return jnp.concatenate([slab_ref[j] for j in range(n)], axis=1)


def _mix_kernel(x_ref, att_ref, st0_ref, halo0_ref, win_ref, cw_ref, cb_ref, cwo_ref, bblk_ref, ar_ref,
                ai_ref, cblk_ref, d_ref, wglu_ref, bglu_ref, swo_ref, mwo_ref, wo_ref, g_ref, b_ref,
                out_ref, st_out_ref, halo_out_ref, bu_scr, st_scr, cbuf_scr, slab_ref, *, steps):
    tm = steps * BATCH
    W = MIX_WIDTH

    @pl.when(pl.program_id(0) == 0)
    def _():
        st_scr[...] = st0_ref[...]
        cbuf_scr[0:_HALO, :] = halo0_ref[...]

    x = _to_time_major(x_ref, slab_ref, D_MODEL, steps)
    xb = x.astype(BF16)

    us = _dot(xb, win_ref[:, 3 * W:4 * W])
    usb = us.astype(BF16)
    nb = S5_BLOCK_STATES
    for blk in range(S5_BLOCKS):
        bu_scr[:, 2 * nb * blk:2 * nb * (blk + 1)] = _dot(
            usb[:, blk * 128:(blk + 1) * 128], bblk_ref[blk])
    for blk in range(S5_BLOCKS):
        c_re = 2 * nb * blk
        c_im = c_re + nb
        a_re = ar_ref[blk]
        a_im = ai_ref[blk]

        def step(t, carry, c_re=c_re, c_im=c_im, a_re=a_re, a_im=a_im):
            s_re, s_im = carry
            r0 = pl.multiple_of(t * BATCH, BATCH)
            n_re = a_re * s_re - a_im * s_im + bu_scr[pl.ds(r0, BATCH), c_re:c_re + nb]
            n_im = a_re * s_im + a_im * s_re + bu_scr[pl.ds(r0, BATCH), c_im:c_im + nb]
            bu_scr[pl.ds(r0, BATCH), c_re:c_re + nb] = n_re
            bu_scr[pl.ds(r0, BATCH), c_im:c_im + nb] = n_im
            return n_re, n_im

        s_re, s_im = lax.fori_loop(
            0, steps, step, (st_scr[:, c_re:c_re + nb], st_scr[:, c_im:c_im + nb]), unroll=True)
        st_scr[:, c_re:c_re + nb] = s_re
        st_scr[:, c_im:c_im + nb] = s_im

    pc = _dot(xb, win_ref[:, 0:3 * W])
    u = pc[:, 2 * W:3 * W] * pc[:, 0:W]
    cbuf_scr[_HALO:_HALO + tm, :] = u
    y = (cb_ref[...] + cw_ref[0:1, :] * cbuf_scr[0:tm, :]
         + cw_ref[1:2, :] * cbuf_scr[BATCH:BATCH + tm, :] + cw_ref[2:3, :] * u)
    cbuf_scr[0:_HALO, :] = u[tm - _HALO:, :]
    y_b = _dot((pc[:, W:2 * W] * y).astype(BF16), cwo_ref[...])

    att = _to_time_major(att_ref, slab_ref, W, steps).astype(BF16)
    y_a = _dot(att, mwo_ref[...])
    g0 = 4 * W
    mixed = _sigmoid(_dot(xb, win_ref[:, g0:g0 + D_MODEL])) * y_a
    mixed += _sigmoid(_dot(xb, win_ref[:, g0 + D_MODEL:g0 + 2 * D_MODEL])) * y_b
    gate_c = _sigmoid(_dot(xb, win_ref[:, g0 + 2 * D_MODEL:g0 + 3 * D_MODEL]))

    y = jnp.concatenate(
        [_dot(bu_scr[:, 2 * nb * blk:2 * nb * (blk + 1)].astype(BF16), cblk_ref[blk])
         for blk in range(S5_BLOCKS)], axis=1)
    y = _gelu_tanh(y + d_ref[...] * us)
    y = y * _sigmoid(_dot(y.astype(BF16), wglu_ref[...]) + bglu_ref[...])
    mixed += gate_c * _dot(y.astype(BF16), swo_ref[...])

    half = tm // 2
    for r in range(2):
        rows = slice(r * half, (r + 1) * half)
        z = _layer_norm(ALPHA * x[rows] + _dot(mixed[rows].astype(BF16), wo_ref[...]),
                        g_ref[...], b_ref[...])
        for j in range(_D_SLABS):
            slab_ref[j, rows, :] = z[:, j * HEAD_LANES:(j + 1) * HEAD_LANES]
    for b in range(BATCH):
        for j in range(_D_SLABS):
            out_ref[b, :, j * HEAD_LANES:(j + 1) * HEAD_LANES] = slab_ref[
                j, pl.ds(b, steps, stride=BATCH), :]
    st_out_ref[...] = st_scr[...]
    halo_out_ref[...] = cbuf_scr[0:_HALO, :]


def _mix(xn, att, st0, halo0, layer, *weights):
    t_len = xn.shape[1]
    steps = min(t_len, MIX_STEPS)
    tm = steps * BATCH
    tile = lambda w: pl.BlockSpec((BATCH, steps, w), lambda i: (0, i, 0))
    return pl.pallas_call(
        functools.partial(_mix_kernel, steps=steps),
        grid=(t_len // steps,),
        in_specs=[tile(D_MODEL), tile(MIX_WIDTH), _const_spec(st0.shape), _const_spec(halo0.shape)]
        + [_layer_spec(w, layer) for w in weights],
        out_specs=[tile(D_MODEL), pl.BlockSpec((BATCH, S5_COLS), lambda i: (0, 0)),
                   pl.BlockSpec((_HALO, MIX_WIDTH), lambda i: (0, 0))],
        out_shape=[jax.ShapeDtypeStruct((BATCH, t_len, D_MODEL), F32),
                   jax.ShapeDtypeStruct((BATCH, S5_COLS), F32),
                   jax.ShapeDtypeStruct((_HALO, MIX_WIDTH), F32)],
        scratch_shapes=[
            pltpu.VMEM((tm, S5_COLS), F32),
            pltpu.VMEM((BATCH, S5_COLS), F32),
            pltpu.VMEM((tm + _HALO, MIX_WIDTH), F32),
            pltpu.VMEM((_D_SLABS, tm, HEAD_LANES), F32),
        ],
        compiler_params=pltpu.CompilerParams(
            dimension_semantics=("arbitrary",), vmem_limit_bytes=VMEM_LIMIT),
        name="mixers_merge",
    )(xn, att, st0, halo0, *weights)


def _rope_tables(pos):
    n = pos.shape[0]
    inv_freq = ROPE_BASE ** (-jnp.arange(0, QK_ROPE_DIM, 2, dtype=F32) / QK_ROPE_DIM)
    ang = pos.astype(F32)[:, None] * inv_freq[None, :]
    cos2 = jnp.tile(jnp.cos(ang), (1, 2))
    sin2 = jnp.tile(jnp.sin(ang), (1, 2))
    zn = jnp.zeros((n, QK_NOPE_DIM), F32)
    zt = jnp.zeros((n, HEAD_LANES - QK_HEAD_DIM), F32)
    scale = QK_HEAD_DIM ** -0.5 * LOG2E
    cos_q = scale * jnp.concatenate([jnp.ones_like(zn), cos2, zt], axis=1)
    sin_q = scale * jnp.concatenate([zn, sin2, zt], axis=1)
    cos_k = jnp.concatenate([zn, cos2, zt], axis=1)
    sin_k = jnp.concatenate([zn, sin2, zt], axis=1)
    return jnp.concatenate([cos_q, sin_q, cos_k, sin_k], axis=1)


def _rot_half_cols(w):
    half = QK_ROPE_DIM // 2
    return jnp.concatenate([-w[..., half:], w[..., :half]], axis=-1)


def _qkv_weights(w_in, w_uq, w_ukv):
    zeros = lambda *shape: jnp.zeros(shape, BF16)
    kr = w_in[:, _OFF_KR:_OFF_CONV]
    tail = HEAD_LANES - QK_HEAD_DIM
    wc = jnp.concatenate([
        w_in[:, :_OFF_KR],
        zeros(D_MODEL, QK_NOPE_DIM), kr, zeros(D_MODEL, tail),
        zeros(D_MODEL, QK_NOPE_DIM), _rot_half_cols(kr), zeros(D_MODEL, tail)], axis=1)
    uq = w_uq.reshape(Q_LORA_RANK, HEADS, QK_HEAD_DIM)
    wqa = jnp.concatenate([uq, zeros(Q_LORA_RANK, HEADS, tail)], axis=-1)
    wqb = jnp.concatenate([zeros(Q_LORA_RANK, HEADS, QK_NOPE_DIM), _rot_half_cols(uq[..., QK_NOPE_DIM:]),
                           zeros(Q_LORA_RANK, HEADS, tail)], axis=-1)
    ukv = w_ukv.reshape(KV_LORA_RANK, HEADS, QK_NOPE_DIM + V_HEAD_DIM)
    wuk = jnp.concatenate([ukv[..., :QK_NOPE_DIM], zeros(KV_LORA_RANK, HEADS, HEAD_LANES - QK_NOPE_DIM)],
                          axis=-1)
    wuv = jnp.concatenate([ukv[..., QK_NOPE_DIM:], zeros(KV_LORA_RANK, HEADS, V_ROWS - V_HEAD_DIM)], axis=-1)
    vones = jnp.zeros((HEADS, V_ROWS), F32).at[:, V_HEAD_DIM].set(1.0).reshape(1, HEADS * V_ROWS)
    flat = lambda w: w.reshape(w.shape[0], -1)
    return wc, flat(wqa), flat(wqb), flat(wuk), flat(wuv), vones


def _s5_weights(a_re, a_im, log_dt, b_re, b_im, c_re, c_im):
    dt = jnp.exp(log_dt)[:, None]
    mag = jnp.exp(dt * a_re)
    ab_re, ab_im = mag * jnp.cos(dt * a_im), mag * jnp.sin(dt * a_im)
    den = a_re * a_re + a_im * a_im
    nr, ni = ab_re - 1.0, ab_im
    coef_re = (nr * a_re + ni * a_im) / den
    coef_im = (ni * a_re - nr * a_im) / den
    bb_re = coef_re[..., None] * b_re - coef_im[..., None] * b_im
    bb_im = coef_re[..., None] * b_im + coef_im[..., None] * b_re
    gpb = S5_GROUPS // S5_BLOCKS
    eye = jnp.eye(gpb, dtype=F32)

    def in_blocks(bb):
        t = bb.transpose(0, 2, 1).reshape(S5_BLOCKS, gpb, S5_GROUP, S5_STATE)
        return jnp.einsum('bghn,gk->bghkn', t, eye).reshape(S5_BLOCKS, gpb * S5_GROUP, gpb * S5_STATE)

    def out_blocks(cc):
        t = cc.reshape(S5_BLOCKS, gpb, S5_GROUP, S5_STATE)
        return jnp.einsum('bghn,gk->bgnkh', t, eye).reshape(S5_BLOCKS, gpb * S5_STATE, gpb * S5_GROUP)

    bblk = jnp.concatenate([in_blocks(bb_re), in_blocks(bb_im)], axis=2).astype(BF16)
    cblk = jnp.concatenate([out_blocks(c_re), out_blocks(-c_im)], axis=1).astype(BF16)
    bcast = lambda a: jnp.broadcast_to(
        a.reshape(S5_BLOCKS, 1, S5_BLOCK_STATES), (S5_BLOCKS, BATCH, S5_BLOCK_STATES))
    return bblk, bcast(ab_re), bcast(ab_im), cblk


def kernel(x, meta, ffn1_w_gate, ffn1_w_up, ffn1_w_down, ln1_g, ln1_b, w_in, mla_q_norm_g, mla_w_uq, mla_kv_norm_g, mla_w_ukv, mla_w_o, conv_w, conv_b, conv_w_out, s5_a_re, s5_a_im, s5_log_dt, s5_b_re, s5_b_im, s5_c_re, s5_c_im, s5_d, s5_w_glu, s5_b_glu, s5_w_out, w_o, ln2_g, ln2_b, ffn2_w_gate, ffn2_w_up, ffn2_w_down, ln3_g, ln3_b):
    rows = lambda v: v.reshape(DEPTH, 1, -1).astype(F32)
    bf = lambda w: w.astype(BF16)
    pad_rows = lambda a: jnp.pad(a, ((0, BLOCK_Q - N_META), (0, 0)))
    per_batch = lambda a: jnp.broadcast_to(a[None], (BATCH,) + a.shape)
    ffn1 = (bf(ffn1_w_gate), bf(ffn1_w_up), bf(ffn1_w_down), rows(ln1_g), rows(ln1_b))
    ffn2 = (bf(ffn2_w_gate), bf(ffn2_w_up), bf(ffn2_w_down), rows(ln3_g), rows(ln3_b))
    w_in_b = bf(w_in)
    wc, wqa, wqb, wuk, wuv, vones = jax.vmap(_qkv_weights)(w_in_b, bf(mla_w_uq), bf(mla_w_ukv))
    qkv_w = (wc, rows(mla_q_norm_g), rows(mla_kv_norm_g), wqa, wqb, wuk, wuv, vones)
    bblk, ar, ai, cblk = jax.vmap(_s5_weights)(s5_a_re, s5_a_im, s5_log_dt, s5_b_re, s5_b_im, s5_c_re, s5_c_im)
    mix_w = (w_in_b[:, :, _OFF_CONV:], conv_w.astype(F32), rows(conv_b), bf(conv_w_out),
             bblk, ar, ai, cblk, rows(s5_d), bf(s5_w_glu), rows(s5_b_glu), bf(s5_w_out),
             bf(mla_w_o), bf(w_o), rows(ln2_g), rows(ln2_b))

    h = x.astype(F32).reshape(BATCH * SEQ, D_MODEL)
    hm = meta.astype(F32)
    tab = _rope_tables(N_META + jnp.arange(SEQ))
    tab_m = _rope_tables(jnp.arange(N_META))
    for i in range(DEPTH):
        xm = _ffn_ln(hm, i, *ffn1)
        qm, km, vm = _qkv(xm[None], i, *qkv_w, tab_m)
        km, vm_t = pad_rows(km[0]), pad_rows(vm[0]).T
        att_m = _attention_meta(pad_rows(qm[0]), km, vm_t)[:N_META]
        hm8, state, halo = _mix(per_batch(xm), per_batch(att_m), jnp.zeros((BATCH, S5_COLS), F32),
                                jnp.zeros((_HALO, MIX_WIDTH), F32), i, *mix_w)

        xn3 = _ffn_ln(h, i, *ffn1).reshape(BATCH, SEQ, D_MODEL)
        q, k, v = _qkv(xn3, i, *qkv_w, tab)
        att = _attention(q, k, v, km, vm_t)
        h3, _, _ = _mix(xn3, att, state, halo, i, *mix_w)
        h = _ffn_ln(h3.reshape(BATCH * SEQ, D_MODEL), i, *ffn2)
        if i + 1 < DEPTH:
            hm = _ffn_ln(hm8[0], i, *ffn2)
    return h.reshape(BATCH, SEQ, D_MODEL)
```

```python
import functools
import math

import jax
import jax.numpy as jnp
from jax import lax
from jax.experimental import pallas as pl
from jax.experimental.pallas import tpu as pltpu

D_MODEL = 1024
BATCH = 8
SEQ = 2048
DEPTH = 2
N_META = 16
BLOCK_Q = 128
MIX_WIDTH = D_MODEL // 2
HEADS = 8
V_HEAD_DIM = 64
QK_NOPE_DIM = 64
QK_ROPE_DIM = 32
QK_HEAD_DIM = QK_NOPE_DIM + QK_ROPE_DIM
Q_LORA_RANK = 384
KV_LORA_RANK = 256
ROPE_BASE = 10000.0
S5_GROUP = 16
S5_GROUPS = 32
S5_STATE = 64
D_FF = 2816
ALPHA = (2.0 * DEPTH) ** 0.25
LN_EPS = 1e-5
RMS_EPS = 1e-6

HEAD_LANES = 128
QKV_WIDTH = HEADS * HEAD_LANES
TOK_TILE = 512
ATT_TILE = 2 * BLOCK_Q
N_SEQ_TILES = SEQ // ATT_TILE
V_ROWS = V_HEAD_DIM + 16
LOG2E = math.log2(math.e)
S5_BLOCKS = 4
S5_BLOCK_STATES = (S5_GROUPS // S5_BLOCKS) * S5_STATE
S5_COLS = 2 * S5_GROUPS * S5_STATE
MIX_STEPS = 64

_OFF_KR, _OFF_CONV = Q_LORA_RANK + KV_LORA_RANK, Q_LORA_RANK + KV_LORA_RANK + QK_ROPE_DIM

V7X_VMEM_BYTES = 64 * 1024 * 1024
VMEM_LIMIT = V7X_VMEM_BYTES * 7 // 8

F32 = jnp.float32
BF16 = jnp.bfloat16


def _dot(a, b):
    return jnp.dot(a, b, preferred_element_type=F32)


def _const_spec(shape):
    nd = len(shape)
    return pl.BlockSpec(shape, lambda *_: (0,) * nd, pipeline_mode=pl.Buffered(1))


def _layer_spec(stacked, layer):
    nd = stacked.ndim - 1
    return pl.BlockSpec((None,) + stacked.shape[1:], lambda *_: (layer,) + (0,) * nd,
                        pipeline_mode=pl.Buffered(1))


def _layer_norm(y, g, b):
    mu = jnp.mean(y, axis=-1, keepdims=True)
    yc = y - mu
    var = jnp.mean(yc * yc, axis=-1, keepdims=True)
    return yc * lax.rsqrt(var + LN_EPS) * g + b


def _rms_norm(y, g):
    return y * lax.rsqrt(jnp.mean(y * y, axis=-1, keepdims=True) + RMS_EPS) * g


def _sigmoid(y):
    return 1.0 / (1.0 + jnp.exp(-y))


def _gelu_tanh(y):
    return 0.5 * y * (1.0 + jnp.tanh(math.sqrt(2.0 / math.pi) * (y + 0.044715 * (y * y * y))))


def _ffn_rows(x, wg_ref, wu_ref, wd_ref, g_ref, b_ref):
    xb = x.astype(BF16)
    gate = _dot(xb, wg_ref[...])
    hmid = (gate * _sigmoid(gate) * _dot(xb, wu_ref[...])).astype(BF16)
    return _layer_norm(ALPHA * x + 0.5 * _dot(hmid, wd_ref[...]), g_ref[...], b_ref[...])


def _ffn_ln_kernel(x_ref, wg_ref, wu_ref, wd_ref, g_ref, b_ref, o_ref, *, sub_rows):
    for r in range(x_ref.shape[0] // sub_rows):
        rows = slice(r * sub_rows, (r + 1) * sub_rows)
        o_ref[rows, :] = _ffn_rows(x_ref[rows, :], wg_ref, wu_ref, wd_ref, g_ref, b_ref)


def _ffn_ln(x, layer, wg, wu, wd, g, b):
    rows = x.shape[0]
    tm = min(rows, 2 * TOK_TILE)
    flat = pl.BlockSpec((tm, D_MODEL), lambda i: (i, 0))
    return pl.pallas_call(
        functools.partial(_ffn_ln_kernel, sub_rows=min(tm, TOK_TILE)),
        grid=(rows // tm,),
        in_specs=[flat] + [_layer_spec(w, layer) for w in (wg, wu, wd, g, b)],
        out_specs=flat,
        out_shape=jax.ShapeDtypeStruct((rows, D_MODEL), F32),
        compiler_params=pltpu.CompilerParams(
            dimension_semantics=("arbitrary",), vmem_limit_bytes=VMEM_LIMIT),
        name="ffn_ln",
    )(x, wg, wu, wd, g, b)


_WC_WIDTH = Q_LORA_RANK + KV_LORA_RANK + 2 * HEAD_LANES


def _qkv_kernel(x_ref, wc_ref, gq_ref, gkv_ref, wqa_ref, wqb_ref, wuk_ref, wuv_ref, vones_ref, tab_ref,
                q_ref, k_ref, v_ref, *, transpose_v):
    xb = x_ref[...].astype(BF16)
    c = _dot(xb, wc_ref[...])
    c_q = c[:, :Q_LORA_RANK]
    c_kv = c[:, Q_LORA_RANK:Q_LORA_RANK + KV_LORA_RANK]
    k_r = c[:, _WC_WIDTH - 2 * HEAD_LANES:_WC_WIDTH - HEAD_LANES]
    k_r_rot = c[:, _WC_WIDTH - HEAD_LANES:]
    qn = _rms_norm(c_q, gq_ref[...]).astype(BF16)
    kvn = _rms_norm(c_kv, gkv_ref[...]).astype(BF16)
    cos_q = tab_ref[:, 0 * HEAD_LANES:1 * HEAD_LANES]
    sin_q = tab_ref[:, 1 * HEAD_LANES:2 * HEAD_LANES]
    cos_k = tab_ref[:, 2 * HEAD_LANES:3 * HEAD_LANES]
    sin_k = tab_ref[:, 3 * HEAD_LANES:4 * HEAD_LANES]
    k_rope = k_r * cos_k + k_r_rot * sin_k
    q_a = _dot(qn, wqa_ref[...])
    q_b = _dot(qn, wqb_ref[...])
    k_nope = _dot(kvn, wuk_ref[...])
    for h in range(HEADS):
        sl = slice(h * HEAD_LANES, (h + 1) * HEAD_LANES)
        q_ref[:, sl] = (q_a[:, sl] * cos_q + q_b[:, sl] * sin_q).astype(BF16)
        k_ref[:, sl] = (k_nope[:, sl] + k_rope).astype(BF16)
    v = _dot(kvn, wuv_ref[...]) + vones_ref[...]
    if transpose_v:
        for t in range(x_ref.shape[0] // ATT_TILE):
            v_ref[t] = v[t * ATT_TILE:(t + 1) * ATT_TILE, :].T.astype(BF16)
    else:
        v_ref[...] = v.astype(BF16)


def _qkv(xn, layer, wc, gq, gkv, wqa, wqb, wuk, wuv, vones, tab):
    n_seq, rows, _ = xn.shape
    tile = min(rows, 2 * TOK_TILE)
    transpose_v = tile % ATT_TILE == 0
    seq = lambda w: pl.BlockSpec((None, tile, w), lambda s, j: (s, j, 0))
    seq_shape = jax.ShapeDtypeStruct((n_seq, rows, QKV_WIDTH), BF16)
    if transpose_v:
        v_spec = pl.BlockSpec((None, tile // ATT_TILE, HEADS * V_ROWS, ATT_TILE), lambda s, j: (s, j, 0, 0))
        v_shape = jax.ShapeDtypeStruct((n_seq, rows // ATT_TILE, HEADS * V_ROWS, ATT_TILE), BF16)
    else:
        v_spec = seq(HEADS * V_ROWS)
        v_shape = jax.ShapeDtypeStruct((n_seq, rows, HEADS * V_ROWS), BF16)
    return pl.pallas_call(
        functools.partial(_qkv_kernel, transpose_v=transpose_v),
        grid=(n_seq, rows // tile),
        in_specs=[seq(D_MODEL)]
        + [_layer_spec(w, layer) for w in (wc, gq, gkv, wqa, wqb, wuk, wuv, vones)]
        + [pl.BlockSpec((tile, 4 * HEAD_LANES), lambda s, j: (j, 0))],
        out_specs=[seq(QKV_WIDTH), seq(QKV_WIDTH), v_spec],
        out_shape=[seq_shape, seq_shape, v_shape],
        compiler_params=pltpu.CompilerParams(
            dimension_semantics=("arbitrary", "arbitrary"), vmem_limit_bytes=VMEM_LIMIT),
        name="qkv_proj",
    )(xn, wc, gq, gkv, wqa, wqb, wuk, wuv, vones, tab)


_HEAD_GROUPS = tuple((h,) for h in range(HEADS))


def _scores(q_ref, k_ref, s_scr, smax_scr, slot, c, mask, heads=range(HEADS)):
    k0 = c * ATT_TILE if isinstance(c, int) else pl.multiple_of(c * ATT_TILE, ATT_TILE)
    for h in heads:
        sl = slice(h * HEAD_LANES, (h + 1) * HEAD_LANES)
        s_t = lax.dot_general(
            k_ref[pl.ds(k0, ATT_TILE), sl], q_ref[:, sl], (((1,), (1,)), ((), ())),
            preferred_element_type=F32)
        if mask is not None:
            s_t = jnp.where(mask, s_t, -1e30)
        s_scr[slot, h] = s_t
        smax_scr[slot, h] = jnp.max(s_t, axis=0, keepdims=True)


def _attend(v_ref, s_scr, smax_scr, m_scr, acc_scr, slot, c, heads=range(HEADS)):
    for h in heads:
        m_old = m_scr[h]
        m_new = jnp.maximum(m_old, smax_scr[slot, h])
        p_t = jnp.exp2(s_scr[slot, h] - m_new).astype(BF16)
        pv = _dot(v_ref[c, h * V_ROWS:(h + 1) * V_ROWS, :], p_t)
        acc_scr[h] = jnp.exp2(m_old - m_new) * acc_scr[h] + pv
        m_scr[h] = m_new


def _scores_meta(q_ref, km_ref, sm_scr, m_scr, mask):
    for h in range(HEADS):
        sl = slice(h * HEAD_LANES, (h + 1) * HEAD_LANES)
        s_t = lax.dot_general(
            km_ref[:, sl], q_ref[:, sl], (((1,), (1,)), ((), ())), preferred_element_type=F32)
        s_t = jnp.where(mask, s_t, -1e30)
        sm_scr[h] = s_t
        m_scr[h] = jnp.max(s_t, axis=0, keepdims=True)


def _attend_meta(vm_ref, sm_scr, m_scr, acc_scr):
    for h in range(HEADS):
        p_t = jnp.exp2(sm_scr[h] - m_scr[h]).astype(BF16)
        acc_scr[h] = _dot(vm_ref[h * V_ROWS:(h + 1) * V_ROWS, :], p_t)


def _attn_out(acc_scr):
    o_t = jnp.concatenate(
        [acc_scr[h, 0:V_HEAD_DIM, :] / acc_scr[h, V_HEAD_DIM:V_HEAD_DIM + 1, :] for h in range(HEADS)],
        axis=0)
    return o_t.T.astype(BF16)


def _attn_kernel(q_ref, k_ref, v_ref, km_ref, vm_ref, o_ref, s_scr, smax_scr, sm_scr, m_scr, acc_scr):
    i = pl.program_id(1)
    causal = (lax.broadcasted_iota(jnp.int32, (ATT_TILE, ATT_TILE), 0)
              <= lax.broadcasted_iota(jnp.int32, (ATT_TILE, ATT_TILE), 1))
    meta_mask = lax.broadcasted_iota(jnp.int32, (BLOCK_Q, ATT_TILE), 0) < N_META

    scores = functools.partial(_scores, q_ref, k_ref, s_scr, smax_scr)
    attend = functools.partial(_attend, v_ref, s_scr, smax_scr, m_scr, acc_scr)
    def overlapped(next_slot, next_tile, mask, slot, tile):
        for hs in _HEAD_GROUPS:
            scores(next_slot, next_tile, mask, hs)
            attend(slot, tile, hs)

    _scores_meta(q_ref, km_ref, sm_scr, m_scr, meta_mask)
    scores(0, 0, causal | (i > 0))
    _attend_meta(vm_ref, sm_scr, m_scr, acc_scr)

    def two_tiles(pp, carry):
        p = 1 + 2 * pp
        overlapped(1, p, None, 0, p - 1)
        overlapped(0, p + 1, None, 1, p)
        return carry

    lax.fori_loop(0, (i - 1) // 2, two_tiles, 0)

    @pl.when(i == 0)
    def _():
        attend(0, 0)

    @pl.when(i % 2 == 1)
    def _():
        overlapped(1, i, causal, 0, i - 1)
        attend(1, i)

    @pl.when((i % 2 == 0) & (i > 0))
    def _():
        overlapped(1, i - 1, None, 0, i - 2)
        overlapped(0, i, causal, 1, i - 1)
        attend(0, i)

    o_ref[...] = _attn_out(acc_scr)


def _attention(q, k, v_t, k_meta, v_meta_t):
    return pl.pallas_call(
        _attn_kernel,
        grid=(BATCH, N_SEQ_TILES),
        in_specs=[
            pl.BlockSpec((None, ATT_TILE, QKV_WIDTH), lambda s, i: (s, i, 0)),
            pl.BlockSpec((None, SEQ, QKV_WIDTH), lambda s, i: (s, 0, 0)),
            pl.BlockSpec((None, N_SEQ_TILES, HEADS * V_ROWS, ATT_TILE), lambda s, i: (s, 0, 0, 0)),
            _const_spec((BLOCK_Q, QKV_WIDTH)),
            _const_spec((HEADS * V_ROWS, BLOCK_Q)),
        ],
        out_specs=pl.BlockSpec((None, ATT_TILE, MIX_WIDTH), lambda s, i: (s, i, 0)),
        out_shape=jax.ShapeDtypeStruct((BATCH, SEQ, MIX_WIDTH), BF16),
        scratch_shapes=[pltpu.VMEM((2, HEADS, ATT_TILE, ATT_TILE), F32),
                        pltpu.VMEM((2, HEADS, 1, ATT_TILE), F32),
                        pltpu.VMEM((HEADS, BLOCK_Q, ATT_TILE), F32),
                        pltpu.VMEM((HEADS, 1, ATT_TILE), F32),
                        pltpu.VMEM((HEADS, V_ROWS, ATT_TILE), F32)],
        compiler_params=pltpu.CompilerParams(
            dimension_semantics=("arbitrary", "arbitrary"), vmem_limit_bytes=VMEM_LIMIT),
        name="mla_attention",
    )(q, k, v_t, k_meta, v_meta_t)


def _attn_meta_kernel(q_ref, km_ref, vm_ref, o_ref, sm_scr, m_scr, acc_scr):
    k_off = lax.broadcasted_iota(jnp.int32, (BLOCK_Q, BLOCK_Q), 0)
    q_off = lax.broadcasted_iota(jnp.int32, (BLOCK_Q, BLOCK_Q), 1)
    _scores_meta(q_ref, km_ref, sm_scr, m_scr, (k_off < N_META) & (k_off <= q_off))
    _attend_meta(vm_ref, sm_scr, m_scr, acc_scr)
    o_ref[...] = _attn_out(acc_scr)


def _attention_meta(q_meta, k_meta, v_meta_t):
    return pl.pallas_call(
        _attn_meta_kernel,
        out_shape=jax.ShapeDtypeStruct((BLOCK_Q, MIX_WIDTH), BF16),
        scratch_shapes=[pltpu.VMEM((HEADS, BLOCK_Q, BLOCK_Q), F32),
                        pltpu.VMEM((HEADS, 1, BLOCK_Q), F32),
                        pltpu.VMEM((HEADS, V_ROWS, BLOCK_Q), F32)],
        name="mla_attention_meta",
    )(q_meta, k_meta, v_meta_t)


_HALO = 2 * BATCH
_D_SLABS = D_MODEL // HEAD_LANES
_MERGE_COLS = 256
_LN_GROUPS = 2


def _to_time_major(src_ref, slab_ref, width, steps):
    n = width // HEAD_LANES
    for b in range(BATCH):
        for j in range(n):
            slab_ref[j, pl.ds(b, steps, stride=BATCH), :] = (
                src_ref[b, :, j * HEAD_LANES:(j + 1) * HEAD_LANES].astype(F32))
    return jnp.concatenate([slab_ref[j] for j in range(n)], axis=1)


def _mix_kernel(x_ref, att_ref, st0_ref, halo0_ref, win_ref, cw_ref, cb_ref, cwo_ref, bblk_ref, ar_ref,
                ai_ref, cblk_ref, d_ref, wglu_ref, bglu_ref, swo_ref, mwo_ref, wo_ref, g_ref, b_ref,
                out_ref, st_out_ref, halo_out_ref, bu_scr, st_scr, cbuf_scr, slab_ref, *, steps):
    tm = steps * BATCH
    W = MIX_WIDTH

    @pl.when(pl.program_id(0) == 0)
    def _():
        st_scr[...] = st0_ref[...]
        cbuf_scr[0:_HALO, :] = halo0_ref[...]

    x = _to_time_major(x_ref, slab_ref, D_MODEL, steps)
    xb = x.astype(BF16)

    us = _dot(xb, win_ref[:, 3 * W:4 * W])
    usb = us.astype(BF16)
    nb = S5_BLOCK_STATES
    for blk in range(S5_BLOCKS):
        bu_scr[:, 2 * nb * blk:2 * nb * (blk + 1)] = _dot(
            usb[:, blk * 128:(blk + 1) * 128], bblk_ref[blk])
    for blk in range(S5_BLOCKS):
        c_re = 2 * nb * blk
        c_im = c_re + nb
        a_re = ar_ref[blk]
        a_im = ai_ref[blk]

        def step(t, carry, c_re=c_re, c_im=c_im, a_re=a_re, a_im=a_im):
            s_re, s_im = carry
            r0 = pl.multiple_of(t * BATCH, BATCH)
            n_re = a_re * s_re - a_im * s_im + bu_scr[pl.ds(r0, BATCH), c_re:c_re + nb]
            n_im = a_re * s_im + a_im * s_re + bu_scr[pl.ds(r0, BATCH), c_im:c_im + nb]
            bu_scr[pl.ds(r0, BATCH), c_re:c_re + nb] = n_re
            bu_scr[pl.ds(r0, BATCH), c_im:c_im + nb] = n_im
            return n_re, n_im

        s_re, s_im = lax.fori_loop(
            0, steps, step, (st_scr[:, c_re:c_re + nb], st_scr[:, c_im:c_im + nb]), unroll=True)
        st_scr[:, c_re:c_re + nb] = s_re
        st_scr[:, c_im:c_im + nb] = s_im

    pc = _dot(xb, win_ref[:, 0:3 * W])
    u = pc[:, 2 * W:3 * W] * pc[:, 0:W]
    cbuf_scr[_HALO:_HALO + tm, :] = u
    y = (cb_ref[...] + cw_ref[0:1, :] * cbuf_scr[0:tm, :]
         + cw_ref[1:2, :] * cbuf_scr[BATCH:BATCH + tm, :] + cw_ref[2:3, :] * u)
    cbuf_scr[0:_HALO, :] = u[tm - _HALO:, :]
    z_b = (pc[:, W:2 * W] * y).astype(BF16)

    y = jnp.concatenate(
        [_dot(bu_scr[:, 2 * nb * blk:2 * nb * (blk + 1)].astype(BF16), cblk_ref[blk])
         for blk in range(S5_BLOCKS)], axis=1)
    y = _gelu_tanh(y + d_ref[...] * us)

    att = _to_time_major(att_ref, slab_ref, W, steps).astype(BF16)
    g0 = 4 * W
    mixed_cols, gate_c_cols = [], []
    for c0 in range(0, D_MODEL, _MERGE_COLS):
        cols = slice(c0, c0 + _MERGE_COLS)
        gate = lambda k: _sigmoid(_dot(xb, win_ref[:, g0 + k * D_MODEL + c0:g0 + k * D_MODEL + c0 + _MERGE_COLS]))
        mixed_cols.append(gate(0) * _dot(att, mwo_ref[:, cols]) + gate(1) * _dot(z_b, cwo_ref[:, cols]))
        gate_c_cols.append(gate(2))
        if c0 == D_MODEL // 2 - _MERGE_COLS:
            y = y * _sigmoid(_dot(y.astype(BF16), wglu_ref[...]) + bglu_ref[...])
    mixed = jnp.concatenate(mixed_cols, axis=1) + jnp.concatenate(gate_c_cols, axis=1) * _dot(
        y.astype(BF16), swo_ref[...])

    group = tm // _LN_GROUPS
    for r in range(_LN_GROUPS):
        rows = slice(r * group, (r + 1) * group)
        z = _layer_norm(ALPHA * x[rows] + _dot(mixed[rows].astype(BF16), wo_ref[...]),
                        g_ref[...], b_ref[...])
        for j in range(_D_SLABS):
            slab_ref[j, rows, :] = z[:, j * HEAD_LANES:(j + 1) * HEAD_LANES]
    for b in range(BATCH):
        for j in range(_D_SLABS):
            out_ref[b, :, j * HEAD_LANES:(j + 1) * HEAD_LANES] = slab_ref[
                j, pl.ds(b, steps, stride=BATCH), :]
    st_out_ref[...] = st_scr[...]
    halo_out_ref[...] = cbuf_scr[0:_HALO, :]


def _mix(xn, att, st0, halo0, layer, *weights):
    t_len = xn.shape[1]
    steps = min(t_len, MIX_STEPS)
    tm = steps * BATCH
    tile = lambda w: pl.BlockSpec((BATCH, steps, w), lambda i: (0, i, 0))
    return pl.pallas_call(
        functools.partial(_mix_kernel, steps=steps),
        grid=(t_len // steps,),
        in_specs=[tile(D_MODEL), tile(MIX_WIDTH), _const_spec(st0.shape), _const_spec(halo0.shape)]
        + [_layer_spec(w, layer) for w in weights],
        out_specs=[tile(D_MODEL), pl.BlockSpec((BATCH, S5_COLS), lambda i: (0, 0)),
                   pl.BlockSpec((_HALO, MIX_WIDTH), lambda i: (0, 0))],
        out_shape=[jax.ShapeDtypeStruct((BATCH, t_len, D_MODEL), F32),
                   jax.ShapeDtypeStruct((BATCH, S5_COLS), F32),
                   jax.ShapeDtypeStruct((_HALO, MIX_WIDTH), F32)],
        scratch_shapes=[
            pltpu.VMEM((tm, S5_COLS), F32),
            pltpu.VMEM((BATCH, S5_COLS), F32),
            pltpu.VMEM((tm + _HALO, MIX_WIDTH), F32),
            pltpu.VMEM((_D_SLABS, tm, HEAD_LANES), F32),
        ],
        compiler_params=pltpu.CompilerParams(
            dimension_semantics=("arbitrary",), vmem_limit_bytes=VMEM_LIMIT),
        name="mixers_merge",
    )(xn, att, st0, halo0, *weights)


def _rope_tables(pos):
    n = pos.shape[0]
    inv_freq = ROPE_BASE ** (-jnp.arange(0, QK_ROPE_DIM, 2, dtype=F32) / QK_ROPE_DIM)
    ang = pos.astype(F32)[:, None] * inv_freq[None, :]
    cos2 = jnp.tile(jnp.cos(ang), (1, 2))
    sin2 = jnp.tile(jnp.sin(ang), (1, 2))
    zn = jnp.zeros((n, QK_NOPE_DIM), F32)
    zt = jnp.zeros((n, HEAD_LANES - QK_HEAD_DIM), F32)
    scale = QK_HEAD_DIM ** -0.5 * LOG2E
    cos_q = scale * jnp.concatenate([jnp.ones_like(zn), cos2, zt], axis=1)
    sin_q = scale * jnp.concatenate([zn, sin2, zt], axis=1)
    cos_k = jnp.concatenate([zn, cos2, zt], axis=1)
    sin_k = jnp.concatenate([zn, sin2, zt], axis=1)
    return jnp.concatenate([cos_q, sin_q, cos_k, sin_k], axis=1)


def _rot_half_cols(w):
    half = QK_ROPE_DIM // 2
    return jnp.concatenate([-w[..., half:], w[..., :half]], axis=-1)


def _qkv_weights(w_in, w_uq, w_ukv):
    zeros = lambda *shape: jnp.zeros(shape, BF16)
    kr = w_in[:, _OFF_KR:_OFF_CONV]
    tail = HEAD_LANES - QK_HEAD_DIM
    wc = jnp.concatenate([
        w_in[:, :_OFF_KR],
        zeros(D_MODEL, QK_NOPE_DIM), kr, zeros(D_MODEL, tail),
        zeros(D_MODEL, QK_NOPE_DIM), _rot_half_cols(kr), zeros(D_MODEL, tail)], axis=1)
    uq = w_uq.reshape(Q_LORA_RANK, HEADS, QK_HEAD_DIM)
    wqa = jnp.concatenate([uq, zeros(Q_LORA_RANK, HEADS, tail)], axis=-1)
    wqb = jnp.concatenate([zeros(Q_LORA_RANK, HEADS, QK_NOPE_DIM), _rot_half_cols(uq[..., QK_NOPE_DIM:]),
                           zeros(Q_LORA_RANK, HEADS, tail)], axis=-1)
    ukv = w_ukv.reshape(KV_LORA_RANK, HEADS, QK_NOPE_DIM + V_HEAD_DIM)
    wuk = jnp.concatenate([ukv[..., :QK_NOPE_DIM], zeros(KV_LORA_RANK, HEADS, HEAD_LANES - QK_NOPE_DIM)],
                          axis=-1)
    wuv = jnp.concatenate([ukv[..., QK_NOPE_DIM:], zeros(KV_LORA_RANK, HEADS, V_ROWS - V_HEAD_DIM)], axis=-1)
    vones = jnp.zeros((HEADS, V_ROWS), F32).at[:, V_HEAD_DIM].set(1.0).reshape(1, HEADS * V_ROWS)
    flat = lambda w: w.reshape(w.shape[0], -1)
    return wc, flat(wqa), flat(wqb), flat(wuk), flat(wuv), vones


def _s5_weights(a_re, a_im, log_dt, b_re, b_im, c_re, c_im):
    dt = jnp.exp(log_dt)[:, None]
    mag = jnp.exp(dt * a_re)
    ab_re, ab_im = mag * jnp.cos(dt * a_im), mag * jnp.sin(dt * a_im)
    den = a_re * a_re + a_im * a_im
    nr, ni = ab_re - 1.0, ab_im
    coef_re = (nr * a_re + ni * a_im) / den
    coef_im = (ni * a_re - nr * a_im) / den
    bb_re = coef_re[..., None] * b_re - coef_im[..., None] * b_im
    bb_im = coef_re[..., None] * b_im + coef_im[..., None] * b_re
    gpb = S5_GROUPS // S5_BLOCKS
    eye = jnp.eye(gpb, dtype=F32)

    def in_blocks(bb):
        t = bb.transpose(0, 2, 1).reshape(S5_BLOCKS, gpb, S5_GROUP, S5_STATE)
        return jnp.einsum('bghn,gk->bghkn', t, eye).reshape(S5_BLOCKS, gpb * S5_GROUP, gpb * S5_STATE)

    def out_blocks(cc):
        t = cc.reshape(S5_BLOCKS, gpb, S5_GROUP, S5_STATE)
        return jnp.einsum('bghn,gk->bgnkh', t, eye).reshape(S5_BLOCKS, gpb * S5_STATE, gpb * S5_GROUP)

    bblk = jnp.concatenate([in_blocks(bb_re), in_blocks(bb_im)], axis=2).astype(BF16)
    cblk = jnp.concatenate([out_blocks(c_re), out_blocks(-c_im)], axis=1).astype(BF16)
    bcast = lambda a: jnp.broadcast_to(
        a.reshape(S5_BLOCKS, 1, S5_BLOCK_STATES), (S5_BLOCKS, BATCH, S5_BLOCK_STATES))
    return bblk, bcast(ab_re), bcast(ab_im), cblk


def kernel(x, meta, ffn1_w_gate, ffn1_w_up, ffn1_w_down, ln1_g, ln1_b, w_in, mla_q_norm_g, mla_w_uq, mla_kv_norm_g, mla_w_ukv, mla_w_o, conv_w, conv_b, conv_w_out, s5_a_re, s5_a_im, s5_log_dt, s5_b_re, s5_b_im, s5_c_re, s5_c_im, s5_d, s5_w_glu, s5_b_glu, s5_w_out, w_o, ln2_g, ln2_b, ffn2_w_gate, ffn2_w_up, ffn2_w_down, ln3_g, ln3_b):
    rows = lambda v: v.reshape(DEPTH, 1, -1).astype(F32)
    bf = lambda w: w.astype(BF16)
    pad_rows = lambda a: jnp.pad(a, ((0, BLOCK_Q - N_META), (0, 0)))
    per_batch = lambda a: jnp.broadcast_to(a[None], (BATCH,) + a.shape)
    ffn1 = (bf(ffn1_w_gate), bf(ffn1_w_up), bf(ffn1_w_down), rows(ln1_g), rows(ln1_b))
    ffn2 = (bf(ffn2_w_gate), bf(ffn2_w_up), bf(ffn2_w_down), rows(ln3_g), rows(ln3_b))
    w_in_b = bf(w_in)
    wc, wqa, wqb, wuk, wuv, vones = jax.vmap(_qkv_weights)(w_in_b, bf(mla_w_uq), bf(mla_w_ukv))
    qkv_w = (wc, rows(mla_q_norm_g), rows(mla_kv_norm_g), wqa, wqb, wuk, wuv, vones)
    bblk, ar, ai, cblk = jax.vmap(_s5_weights)(s5_a_re, s5_a_im, s5_log_dt, s5_b_re, s5_b_im, s5_c_re, s5_c_im)
    mix_w = (w_in_b[:, :, _OFF_CONV:], conv_w.astype(F32), rows(conv_b), bf(conv_w_out),
             bblk, ar, ai, cblk, rows(s5_d), bf(s5_w_glu), rows(s5_b_glu), bf(s5_w_out),
             bf(mla_w_o), bf(w_o), rows(ln2_g), rows(ln2_b))

    h = x.astype(F32).reshape(BATCH * SEQ, D_MODEL)
    hm = meta.astype(F32)
    tab = _rope_tables(N_META + jnp.arange(SEQ))
    tab_m = _rope_tables(jnp.arange(N_META))
    for i in range(DEPTH):
        xm = _ffn_ln(hm, i, *ffn1)
        qm, km, vm = _qkv(xm[None], i, *qkv_w, tab_m)
        km, vm_t = pad_rows(km[0]), pad_rows(vm[0]).T
        att_m = _attention_meta(pad_rows(qm[0]), km, vm_t)[:N_META]
        hm8, state, halo = _mix(per_batch(xm), per_batch(att_m), jnp.zeros((BATCH, S5_COLS), F32),
                                jnp.zeros((_HALO, MIX_WIDTH), F32), i, *mix_w)

        xn3 = _ffn_ln(h, i, *ffn1).reshape(BATCH, SEQ, D_MODEL)
        q, k, v = _qkv(xn3, i, *qkv_w, tab)
        att = _attention(q, k, v, km, vm_t)
        h3, _, _ = _mix(xn3, att, state, halo, i, *mix_w)
        h = _ffn_ln(h3.reshape(BATCH * SEQ, D_MODEL), i, *ffn2)
        if i + 1 < DEPTH:
            hm = _ffn_ln(hm8[0], i, *ffn2)
    return h.reshape(BATCH, SEQ, D_MODEL)
```

```python
import functools
import math

import jax
import jax.numpy as jnp
from jax import lax
from jax.experimental import pallas as pl
from jax.experimental.pallas import tpu as pltpu

D_MODEL = 1024
BATCH = 8
SEQ = 2048
DEPTH = 2
N_META = 16
BLOCK_Q = 128
MIX_WIDTH = D_MODEL // 2
HEADS = 8
V_HEAD_DIM = 64
QK_NOPE_DIM = 64
QK_ROPE_DIM = 32
QK_HEAD_DIM = QK_NOPE_DIM + QK_ROPE_DIM
Q_LORA_RANK = 384
KV_LORA_RANK = 256
ROPE_BASE = 10000.0
S5_GROUP = 16
S5_GROUPS = 32
S5_STATE = 64
D_FF = 2816
ALPHA = (2.0 * DEPTH) ** 0.25
LN_EPS = 1e-5
RMS_EPS = 1e-6

HEAD_LANES = 128
QKV_WIDTH = HEADS * HEAD_LANES
TOK_TILE = 512
ATT_TILE = 2 * BLOCK_Q
N_SEQ_TILES = SEQ // ATT_TILE
V_ROWS = V_HEAD_DIM + 16
LOG2E = math.log2(math.e)
S5_BLOCKS = 4
S5_BLOCK_STATES = (S5_GROUPS // S5_BLOCKS) * S5_STATE
S5_COLS = 2 * S5_GROUPS * S5_STATE
MIX_STEPS = 64

_OFF_KR, _OFF_CONV = Q_LORA_RANK + KV_LORA_RANK, Q_LORA_RANK + KV_LORA_RANK + QK_ROPE_DIM

V7X_VMEM_BYTES = 64 * 1024 * 1024
VMEM_LIMIT = V7X_VMEM_BYTES * 7 // 8

F32 = jnp.float32
BF16 = jnp.bfloat16


def _dot(a, b):
    return jnp.dot(a, b, preferred_element_type=F32)


def _const_spec(shape):
    nd = len(shape)
    return pl.BlockSpec(shape, lambda *_: (0,) * nd, pipeline_mode=pl.Buffered(1))


def _layer_spec(stacked, layer):
    nd = stacked.ndim - 1
    return pl.BlockSpec((None,) + stacked.shape[1:], lambda *_: (layer,) + (0,) * nd,
                        pipeline_mode=pl.Buffered(1))


def _layer_norm(y, g, b):
    mu = jnp.mean(y, axis=-1, keepdims=True)
    yc = y - mu
    var = jnp.mean(yc * yc, axis=-1, keepdims=True)
    return yc * lax.rsqrt(var + LN_EPS) * g + b


def _rms_norm(y, g):
    return y * lax.rsqrt(jnp.mean(y * y, axis=-1, keepdims=True) + RMS_EPS) * g


def _sigmoid(y):
    return 1.0 / (1.0 + jnp.exp(-y))


def _gelu_tanh(y):
    return 0.5 * y * (1.0 + jnp.tanh(math.sqrt(2.0 / math.pi) * (y + 0.044715 * (y * y * y))))


def _ffn_rows(x, wg_ref, wu_ref, wd_ref, g_ref, b_ref):
    xb = x.astype(BF16)
    gate = _dot(xb, wg_ref[...])
    hmid = (gate * _sigmoid(gate) * _dot(xb, wu_ref[...])).astype(BF16)
    return _layer_norm(ALPHA * x + 0.5 * _dot(hmid, wd_ref[...]), g_ref[...], b_ref[...])


def _ffn_ln_kernel(x_ref, wg_ref, wu_ref, wd_ref, g_ref, b_ref, o_ref, *, sub_rows):
    for r in range(x_ref.shape[0] // sub_rows):
        rows = slice(r * sub_rows, (r + 1) * sub_rows)
        o_ref[rows, :] = _ffn_rows(x_ref[rows, :], wg_ref, wu_ref, wd_ref, g_ref, b_ref)


def _ffn_ln(x, layer, wg, wu, wd, g, b):
    rows = x.shape[0]
    tm = min(rows, 2 * TOK_TILE)
    flat = pl.BlockSpec((tm, D_MODEL), lambda i: (i, 0))
    return pl.pallas_call(
        functools.partial(_ffn_ln_kernel, sub_rows=min(tm, TOK_TILE // 2)),
        grid=(rows // tm,),
        in_specs=[flat] + [_layer_spec(w, layer) for w in (wg, wu, wd, g, b)],
        out_specs=flat,
        out_shape=jax.ShapeDtypeStruct((rows, D_MODEL), F32),
        compiler_params=pltpu.CompilerParams(
            dimension_semantics=("arbitrary",), vmem_limit_bytes=VMEM_LIMIT),
        name="ffn_ln",
    )(x, wg, wu, wd, g, b)


_WC_WIDTH = Q_LORA_RANK + KV_LORA_RANK + 2 * HEAD_LANES


def _qkv_kernel(x_ref, wc_ref, gq_ref, gkv_ref, wqa_ref, wqb_ref, wuk_ref, wuv_ref, vones_ref, tab_ref,
                q_ref, k_ref, v_ref, *, transpose_v):
    xb = x_ref[...].astype(BF16)
    c = _dot(xb, wc_ref[...])
    c_q = c[:, :Q_LORA_RANK]
    c_kv = c[:, Q_LORA_RANK:Q_LORA_RANK + KV_LORA_RANK]
    k_r = c[:, _WC_WIDTH - 2 * HEAD_LANES:_WC_WIDTH - HEAD_LANES]
    k_r_rot = c[:, _WC_WIDTH - HEAD_LANES:]
    qn = _rms_norm(c_q, gq_ref[...]).astype(BF16)
    kvn = _rms_norm(c_kv, gkv_ref[...]).astype(BF16)
    cos_q = tab_ref[:, 0 * HEAD_LANES:1 * HEAD_LANES]
    sin_q = tab_ref[:, 1 * HEAD_LANES:2 * HEAD_LANES]
    cos_k = tab_ref[:, 2 * HEAD_LANES:3 * HEAD_LANES]
    sin_k = tab_ref[:, 3 * HEAD_LANES:4 * HEAD_LANES]
    k_rope = k_r * cos_k + k_r_rot * sin_k
    q_a = _dot(qn, wqa_ref[...])
    q_b = _dot(qn, wqb_ref[...])
    k_nope = _dot(kvn, wuk_ref[...])
    for h in range(HEADS):
        sl = slice(h * HEAD_LANES, (h + 1) * HEAD_LANES)
        q_ref[:, sl] = (q_a[:, sl] * cos_q + q_b[:, sl] * sin_q).astype(BF16)
        k_ref[:, sl] = (k_nope[:, sl] + k_rope).astype(BF16)
    v = _dot(kvn, wuv_ref[...]) + vones_ref[...]
    if transpose_v:
        for t in range(x_ref.shape[0] // ATT_TILE):
            v_ref[t] = v[t * ATT_TILE:(t + 1) * ATT_TILE, :].T.astype(BF16)
    else:
        v_ref[...] = v.astype(BF16)


def _qkv(xn, layer, wc, gq, gkv, wqa, wqb, wuk, wuv, vones, tab):
    n_seq, rows, _ = xn.shape
    tile = min(rows, 2 * TOK_TILE)
    transpose_v = tile % ATT_TILE == 0
    seq = lambda w: pl.BlockSpec((None, tile, w), lambda s, j: (s, j, 0))
    seq_shape = jax.ShapeDtypeStruct((n_seq, rows, QKV_WIDTH), BF16)
    if transpose_v:
        v_spec = pl.BlockSpec((None, tile // ATT_TILE, HEADS * V_ROWS, ATT_TILE), lambda s, j: (s, j, 0, 0))
        v_shape = jax.ShapeDtypeStruct((n_seq, rows // ATT_TILE, HEADS * V_ROWS, ATT_TILE), BF16)
    else:
        v_spec = seq(HEADS * V_ROWS)
        v_shape = jax.ShapeDtypeStruct((n_seq, rows, HEADS * V_ROWS), BF16)
    return pl.pallas_call(
        functools.partial(_qkv_kernel, transpose_v=transpose_v),
        grid=(n_seq, rows // tile),
        in_specs=[seq(D_MODEL)]
        + [_layer_spec(w, layer) for w in (wc, gq, gkv, wqa, wqb, wuk, wuv, vones)]
        + [pl.BlockSpec((tile, 4 * HEAD_LANES), lambda s, j: (j, 0))],
        out_specs=[seq(QKV_WIDTH), seq(QKV_WIDTH), v_spec],
        out_shape=[seq_shape, seq_shape, v_shape],
        compiler_params=pltpu.CompilerParams(
            dimension_semantics=("arbitrary", "arbitrary"), vmem_limit_bytes=VMEM_LIMIT),
        name="qkv_proj",
    )(xn, wc, gq, gkv, wqa, wqb, wuk, wuv, vones, tab)


_HEAD_GROUPS = tuple((h,) for h in range(HEADS))


def _scores(q_ref, k_ref, s_scr, smax_scr, slot, c, mask, heads=range(HEADS)):
    k0 = c * ATT_TILE if isinstance(c, int) else pl.multiple_of(c * ATT_TILE, ATT_TILE)
    for h in heads:
        sl = slice(h * HEAD_LANES, (h + 1) * HEAD_LANES)
        s_t = lax.dot_general(
            k_ref[pl.ds(k0, ATT_TILE), sl], q_ref[:, sl], (((1,), (1,)), ((), ())),
            preferred_element_type=F32)
        if mask is not None:
            s_t = jnp.where(mask, s_t, -1e30)
        s_scr[slot, h] = s_t
        smax_scr[slot, h] = jnp.max(s_t, axis=0, keepdims=True)


def _attend(v_ref, s_scr, smax_scr, m_scr, acc_scr, slot, c, heads=range(HEADS)):
    for h in heads:
        m_old = m_scr[h]
        m_new = jnp.maximum(m_old, smax_scr[slot, h])
        p_t = jnp.exp2(s_scr[slot, h] - m_new).astype(BF16)
        pv = _dot(v_ref[c, h * V_ROWS:(h + 1) * V_ROWS, :], p_t)
        acc_scr[h] = jnp.exp2(m_old - m_new) * acc_scr[h] + pv
        m_scr[h] = m_new


def _scores_meta(q_ref, km_ref, sm_scr, m_scr, mask):
    for h in range(HEADS):
        sl = slice(h * HEAD_LANES, (h + 1) * HEAD_LANES)
        s_t = lax.dot_general(
            km_ref[:, sl], q_ref[:, sl], (((1,), (1,)), ((), ())), preferred_element_type=F32)
        s_t = jnp.where(mask, s_t, -1e30)
        sm_scr[h] = s_t
        m_scr[h] = jnp.max(s_t, axis=0, keepdims=True)


def _attend_meta(vm_ref, sm_scr, m_scr, acc_scr):
    for h in range(HEADS):
        p_t = jnp.exp2(sm_scr[h] - m_scr[h]).astype(BF16)
        acc_scr[h] = _dot(vm_ref[h * V_ROWS:(h + 1) * V_ROWS, :], p_t)


def _attn_out(acc_scr):
    o_t = jnp.concatenate(
        [acc_scr[h, 0:V_HEAD_DIM, :] / acc_scr[h, V_HEAD_DIM:V_HEAD_DIM + 1, :] for h in range(HEADS)],
        axis=0)
    return o_t.T.astype(BF16)


def _attn_kernel(q_ref, k_ref, v_ref, km_ref, vm_ref, o_ref, s_scr, smax_scr, sm_scr, m_scr, acc_scr):
    i = pl.program_id(1)
    causal = (lax.broadcasted_iota(jnp.int32, (ATT_TILE, ATT_TILE), 0)
              <= lax.broadcasted_iota(jnp.int32, (ATT_TILE, ATT_TILE), 1))
    meta_mask = lax.broadcasted_iota(jnp.int32, (BLOCK_Q, ATT_TILE), 0) < N_META

    scores = functools.partial(_scores, q_ref, k_ref, s_scr, smax_scr)
    attend = functools.partial(_attend, v_ref, s_scr, smax_scr, m_scr, acc_scr)
    def overlapped(next_slot, next_tile, mask, slot, tile):
        for hs in _HEAD_GROUPS:
            scores(next_slot, next_tile, mask, hs)
            attend(slot, tile, hs)

    _scores_meta(q_ref, km_ref, sm_scr, m_scr, meta_mask)
    scores(0, 0, causal | (i > 0))
    _attend_meta(vm_ref, sm_scr, m_scr, acc_scr)

    def two_tiles(pp, carry):
        p = 1 + 2 * pp
        overlapped(1, p, None, 0, p - 1)
        overlapped(0, p + 1, None, 1, p)
        return carry

    lax.fori_loop(0, (i - 1) // 2, two_tiles, 0)

    @pl.when(i == 0)
    def _():
        attend(0, 0)

    @pl.when(i % 2 == 1)
    def _():
        overlapped(1, i, causal, 0, i - 1)
        attend(1, i)

    @pl.when((i % 2 == 0) & (i > 0))
    def _():
        overlapped(1, i - 1, None, 0, i - 2)
        overlapped(0, i, causal, 1, i - 1)
        attend(0, i)

    o_ref[...] = _attn_out(acc_scr)


def _attention(q, k, v_t, k_meta, v_meta_t):
    return pl.pallas_call(
        _attn_kernel,
        grid=(BATCH, N_SEQ_TILES),
        in_specs=[
            pl.BlockSpec((None, ATT_TILE, QKV_WIDTH), lambda s, i: (s, i, 0)),
            pl.BlockSpec((None, SEQ, QKV_WIDTH), lambda s, i: (s, 0, 0)),
            pl.BlockSpec((None, N_SEQ_TILES, HEADS * V_ROWS, ATT_TILE), lambda s, i: (s, 0, 0, 0)),
            _const_spec((BLOCK_Q, QKV_WIDTH)),
            _const_spec((HEADS * V_ROWS, BLOCK_Q)),
        ],
        out_specs=pl.BlockSpec((None, ATT_TILE, MIX_WIDTH), lambda s, i: (s, i, 0)),
        out_shape=jax.ShapeDtypeStruct((BATCH, SEQ, MIX_WIDTH), BF16),
        scratch_shapes=[pltpu.VMEM((2, HEADS, ATT_TILE, ATT_TILE), F32),
                        pltpu.VMEM((2, HEADS, 1, ATT_TILE), F32),
                        pltpu.VMEM((HEADS, BLOCK_Q, ATT_TILE), F32),
                        pltpu.VMEM((HEADS, 1, ATT_TILE), F32),
                        pltpu.VMEM((HEADS, V_ROWS, ATT_TILE), F32)],
        compiler_params=pltpu.CompilerParams(
            dimension_semantics=("arbitrary", "arbitrary"), vmem_limit_bytes=VMEM_LIMIT),
        name="mla_attention",
    )(q, k, v_t, k_meta, v_meta_t)


def _attn_meta_kernel(q_ref, km_ref, vm_ref, o_ref, sm_scr, m_scr, acc_scr):
    k_off = lax.broadcasted_iota(jnp.int32, (BLOCK_Q, BLOCK_Q), 0)
    q_off = lax.broadcasted_iota(jnp.int32, (BLOCK_Q, BLOCK_Q), 1)
    _scores_meta(q_ref, km_ref, sm_scr, m_scr, (k_off < N_META) & (k_off <= q_off))
    _attend_meta(vm_ref, sm_scr, m_scr, acc_scr)
    o_ref[...] = _attn_out(acc_scr)


def _attention_meta(q_meta, k_meta, v_meta_t):
    return pl.pallas_call(
        _attn_meta_kernel,
        out_shape=jax.ShapeDtypeStruct((BLOCK_Q, MIX_WIDTH), BF16),
        scratch_shapes=[pltpu.VMEM((HEADS, BLOCK_Q, BLOCK_Q), F32),
                        pltpu.VMEM((HEADS, 1, BLOCK_Q), F32),
                        pltpu.VMEM((HEADS, V_ROWS, BLOCK_Q), F32)],
        name="mla_attention_meta",
    )(q_meta, k_meta, v_meta_t)


_HALO = 2 * BATCH
_D_SLABS = D_MODEL // HEAD_LANES
_MERGE_COLS = 256
_LN_GROUPS = 2


def _to_time_major(src_ref, slab_ref, width, steps):
    n = width // HEAD_LANES
    for b in range(BATCH):
        for j in range(n):
            slab_ref[j, pl.ds(b, steps, stride=BATCH), :] = (
                src_ref[b, :, j * HEAD_LANES:(j + 1) * HEAD_LANES].astype(F32))
    return jnp.concatenate([slab_ref[j] for j in range(n)], axis=1)


def _mix_kernel(x_ref, att_ref, st0_ref, halo0_ref, win_ref, cw_ref, cb_ref, cwo_ref, bblk_ref, ar_ref,
                ai_ref, cblk_ref, d_ref, wglu_ref, bglu_ref, swo_ref, mwo_ref, wo_ref, g_ref, b_ref,
                out_ref, st_out_ref, halo_out_ref, bu_scr, st_scr, cbuf_scr, slab_ref, *, steps):
    tm = steps * BATCH
    W = MIX_WIDTH

    @pl.when(pl.program_id(0) == 0)
    def _():
        st_scr[...] = st0_ref[...]
        cbuf_scr[0:_HALO, :] = halo0_ref[...]

    x = _to_time_major(x_ref, slab_ref, D_MODEL, steps)
    xb = x.astype(BF16)

    us = _dot(xb, win_ref[:, 3 * W:4 * W])
    usb = us.astype(BF16)
    nb = S5_BLOCK_STATES
    for blk in range(S5_BLOCKS):
        bu_scr[:, 2 * nb * blk:2 * nb * (blk + 1)] = _dot(
            usb[:, blk * 128:(blk + 1) * 128], bblk_ref[blk])
    for blk in range(S5_BLOCKS):
        c_re = 2 * nb * blk
        c_im = c_re + nb
        a_re = ar_ref[blk]
        a_im = ai_ref[blk]

        def step(t, carry, c_re=c_re, c_im=c_im, a_re=a_re, a_im=a_im):
            s_re, s_im = carry
            r0 = pl.multiple_of(t * BATCH, BATCH)
            n_re = a_re * s_re - a_im * s_im + bu_scr[pl.ds(r0, BATCH), c_re:c_re + nb]
            n_im = a_re * s_im + a_im * s_re + bu_scr[pl.ds(r0, BATCH), c_im:c_im + nb]
            bu_scr[pl.ds(r0, BATCH), c_re:c_re + nb] = n_re
            bu_scr[pl.ds(r0, BATCH), c_im:c_im + nb] = n_im
            return n_re, n_im

        s_re, s_im = lax.fori_loop(
            0, steps, step, (st_scr[:, c_re:c_re + nb], st_scr[:, c_im:c_im + nb]), unroll=True)
        st_scr[:, c_re:c_re + nb] = s_re
        st_scr[:, c_im:c_im + nb] = s_im

    pc = _dot(xb, win_ref[:, 0:3 * W])
    u = pc[:, 2 * W:3 * W] * pc[:, 0:W]
    cbuf_scr[_HALO:_HALO + tm, :] = u
    y = (cb_ref[...] + cw_ref[0:1, :] * cbuf_scr[0:tm, :]
         + cw_ref[1:2, :] * cbuf_scr[BATCH:BATCH + tm, :] + cw_ref[2:3, :] * u)
    cbuf_scr[0:_HALO, :] = u[tm - _HALO:, :]
    z_b = (pc[:, W:2 * W] * y).astype(BF16)

    y = jnp.concatenate(
        [_dot(bu_scr[:, 2 * nb * blk:2 * nb * (blk + 1)].astype(BF16), cblk_ref[blk])
         for blk in range(S5_BLOCKS)], axis=1)
    y = _gelu_tanh(y + d_ref[...] * us)

    att = _to_time_major(att_ref, slab_ref, W, steps).astype(BF16)
    g0 = 4 * W
    mixed_cols, gate_c_cols = [], []
    for c0 in range(0, D_MODEL, _MERGE_COLS):
        cols = slice(c0, c0 + _MERGE_COLS)
        gate = lambda k: _sigmoid(_dot(xb, win_ref[:, g0 + k * D_MODEL + c0:g0 + k * D_MODEL + c0 + _MERGE_COLS]))
        mixed_cols.append(gate(0) * _dot(att, mwo_ref[:, cols]) + gate(1) * _dot(z_b, cwo_ref[:, cols]))
        gate_c_cols.append(gate(2))
        if c0 == D_MODEL // 2 - _MERGE_COLS:
            y = y * _sigmoid(_dot(y.astype(BF16), wglu_ref[...]) + bglu_ref[...])
    mixed = jnp.concatenate(mixed_cols, axis=1) + jnp.concatenate(gate_c_cols, axis=1) * _dot(
        y.astype(BF16), swo_ref[...])

    group = tm // _LN_GROUPS
    for r in range(_LN_GROUPS):
        rows = slice(r * group, (r + 1) * group)
        z = _layer_norm(ALPHA * x[rows] + _dot(mixed[rows].astype(BF16), wo_ref[...]),
                        g_ref[...], b_ref[...])
        for j in range(_D_SLABS):
            slab_ref[j, rows, :] = z[:, j * HEAD_LANES:(j + 1) * HEAD_LANES]
    for b in range(BATCH):
        for j in range(_D_SLABS):
            out_ref[b, :, j * HEAD_LANES:(j + 1) * HEAD_LANES] = slab_ref[
                j, pl.ds(b, steps, stride=BATCH), :]
    st_out_ref[...] = st_scr[...]
    halo_out_ref[...] = cbuf_scr[0:_HALO, :]


def _mix(xn, att, st0, halo0, layer, *weights):
    t_len = xn.shape[1]
    steps = min(t_len, MIX_STEPS)
    tm = steps * BATCH
    tile = lambda w: pl.BlockSpec((BATCH, steps, w), lambda i: (0, i, 0))
    return pl.pallas_call(
        functools.partial(_mix_kernel, steps=steps),
        grid=(t_len // steps,),
        in_specs=[tile(D_MODEL), tile(MIX_WIDTH), _const_spec(st0.shape), _const_spec(halo0.shape)]
        + [_layer_spec(w, layer) for w in weights],
        out_specs=[tile(D_MODEL), pl.BlockSpec((BATCH, S5_COLS), lambda i: (0, 0)),
                   pl.BlockSpec((_HALO, MIX_WIDTH), lambda i: (0, 0))],
        out_shape=[jax.ShapeDtypeStruct((BATCH, t_len, D_MODEL), F32),
                   jax.ShapeDtypeStruct((BATCH, S5_COLS), F32),
                   jax.ShapeDtypeStruct((_HALO, MIX_WIDTH), F32)],
        scratch_shapes=[
            pltpu.VMEM((tm, S5_COLS), F32),
            pltpu.VMEM((BATCH, S5_COLS), F32),
            pltpu.VMEM((tm + _HALO, MIX_WIDTH), F32),
            pltpu.VMEM((_D_SLABS, tm, HEAD_LANES), F32),
        ],
        compiler_params=pltpu.CompilerParams(
            dimension_semantics=("arbitrary",), vmem_limit_bytes=VMEM_LIMIT),
        name="mixers_merge",
    )(xn, att, st0, halo0, *weights)


def _rope_tables(pos):
    n = pos.shape[0]
    inv_freq = ROPE_BASE ** (-jnp.arange(0, QK_ROPE_DIM, 2, dtype=F32) / QK_ROPE_DIM)
    ang = pos.astype(F32)[:, None] * inv_freq[None, :]
    cos2 = jnp.tile(jnp.cos(ang), (1, 2))
    sin2 = jnp.tile(jnp.sin(ang), (1, 2))
    zn = jnp.zeros((n, QK_NOPE_DIM), F32)
    zt = jnp.zeros((n, HEAD_LANES - QK_HEAD_DIM), F32)
    scale = QK_HEAD_DIM ** -0.5 * LOG2E
    cos_q = scale * jnp.concatenate([jnp.ones_like(zn), cos2, zt], axis=1)
    sin_q = scale * jnp.concatenate([zn, sin2, zt], axis=1)
    cos_k = jnp.concatenate([zn, cos2, zt], axis=1)
    sin_k = jnp.concatenate([zn, sin2, zt], axis=1)
    return jnp.concatenate([cos_q, sin_q, cos_k, sin_k], axis=1)


def _rot_half_cols(w):
    half = QK_ROPE_DIM // 2
    return jnp.concatenate([-w[..., half:], w[..., :half]], axis=-1)


def _qkv_weights(w_in, w_uq, w_ukv):
    zeros = lambda *shape: jnp.zeros(shape, BF16)
    kr = w_in[:, _OFF_KR:_OFF_CONV]
    tail = HEAD_LANES - QK_HEAD_DIM
    wc = jnp.concatenate([
        w_in[:, :_OFF_KR],
        zeros(D_MODEL, QK_NOPE_DIM), kr, zeros(D_MODEL, tail),
        zeros(D_MODEL, QK_NOPE_DIM), _rot_half_cols(kr), zeros(D_MODEL, tail)], axis=1)
    uq = w_uq.reshape(Q_LORA_RANK, HEADS, QK_HEAD_DIM)
    wqa = jnp.concatenate([uq, zeros(Q_LORA_RANK, HEADS, tail)], axis=-1)
    wqb = jnp.concatenate([zeros(Q_LORA_RANK, HEADS, QK_NOPE_DIM), _rot_half_cols(uq[..., QK_NOPE_DIM:]),
                           zeros(Q_LORA_RANK, HEADS, tail)], axis=-1)
    ukv = w_ukv.reshape(KV_LORA_RANK, HEADS, QK_NOPE_DIM + V_HEAD_DIM)
    wuk = jnp.concatenate([ukv[..., :QK_NOPE_DIM], zeros(KV_LORA_RANK, HEADS, HEAD_LANES - QK_NOPE_DIM)],
                          axis=-1)
    wuv = jnp.concatenate([ukv[..., QK_NOPE_DIM:], zeros(KV_LORA_RANK, HEADS, V_ROWS - V_HEAD_DIM)], axis=-1)
    vones = jnp.zeros((HEADS, V_ROWS), F32).at[:, V_HEAD_DIM].set(1.0).reshape(1, HEADS * V_ROWS)
    flat = lambda w: w.reshape(w.shape[0], -1)
    return wc, flat(wqa), flat(wqb), flat(wuk), flat(wuv), vones


def _s5_weights(a_re, a_im, log_dt, b_re, b_im, c_re, c_im):
    dt = jnp.exp(log_dt)[:, None]
    mag = jnp.exp(dt * a_re)
    ab_re, ab_im = mag * jnp.cos(dt * a_im), mag * jnp.sin(dt * a_im)
    den = a_re * a_re + a_im * a_im
    nr, ni = ab_re - 1.0, ab_im
    coef_re = (nr * a_re + ni * a_im) / den
    coef_im = (ni * a_re - nr * a_im) / den
    bb_re = coef_re[..., None] * b_re - coef_im[..., None] * b_im
    bb_im = coef_re[..., None] * b_im + coef_im[..., None] * b_re
    gpb = S5_GROUPS // S5_BLOCKS
    eye = jnp.eye(gpb, dtype=F32)

    def in_blocks(bb):
        t = bb.transpose(0, 2, 1).reshape(S5_BLOCKS, gpb, S5_GROUP, S5_STATE)
        return jnp.einsum('bghn,gk->bghkn', t, eye).reshape(S5_BLOCKS, gpb * S5_GROUP, gpb * S5_STATE)

    def out_blocks(cc):
        t = cc.reshape(S5_BLOCKS, gpb, S5_GROUP, S5_STATE)
        return jnp.einsum('bghn,gk->bgnkh', t, eye).reshape(S5_BLOCKS, gpb * S5_STATE, gpb * S5_GROUP)

    bblk = jnp.concatenate([in_blocks(bb_re), in_blocks(bb_im)], axis=2).astype(BF16)
    cblk = jnp.concatenate([out_blocks(c_re), out_blocks(-c_im)], axis=1).astype(BF16)
    bcast = lambda a: jnp.broadcast_to(
        a.reshape(S5_BLOCKS, 1, S5_BLOCK_STATES), (S5_BLOCKS, BATCH, S5_BLOCK_STATES))
    return bblk, bcast(ab_re), bcast(ab_im), cblk


def kernel(x, meta, ffn1_w_gate, ffn1_w_up, ffn1_w_down, ln1_g, ln1_b, w_in, mla_q_norm_g, mla_w_uq, mla_kv_norm_g, mla_w_ukv, mla_w_o, conv_w, conv_b, conv_w_out, s5_a_re, s5_a_im, s5_log_dt, s5_b_re, s5_b_im, s5_c_re, s5_c_im, s5_d, s5_w_glu, s5_b_glu, s5_w_out, w_o, ln2_g, ln2_b, ffn2_w_gate, ffn2_w_up, ffn2_w_down, ln3_g, ln3_b):
    rows = lambda v: v.reshape(DEPTH, 1, -1).astype(F32)
    bf = lambda w: w.astype(BF16)
    pad_rows = lambda a: jnp.pad(a, ((0, BLOCK_Q - N_META), (0, 0)))
    per_batch = lambda a: jnp.broadcast_to(a[None], (BATCH,) + a.shape)
    ffn1 = (bf(ffn1_w_gate), bf(ffn1_w_up), bf(ffn1_w_down), rows(ln1_g), rows(ln1_b))
    ffn2 = (bf(ffn2_w_gate), bf(ffn2_w_up), bf(ffn2_w_down), rows(ln3_g), rows(ln3_b))
    w_in_b = bf(w_in)
    wc, wqa, wqb, wuk, wuv, vones = jax.vmap(_qkv_weights)(w_in_b, bf(mla_w_uq), bf(mla_w_ukv))
    qkv_w = (wc, rows(mla_q_norm_g), rows(mla_kv_norm_g), wqa, wqb, wuk, wuv, vones)
    bblk, ar, ai, cblk = jax.vmap(_s5_weights)(s5_a_re, s5_a_im, s5_log_dt, s5_b_re, s5_b_im, s5_c_re, s5_c_im)
    mix_w = (w_in_b[:, :, _OFF_CONV:], conv_w.astype(F32), rows(conv_b), bf(conv_w_out),
             bblk, ar, ai, cblk, rows(s5_d), bf(s5_w_glu), rows(s5_b_glu), bf(s5_w_out),
             bf(mla_w_o), bf(w_o), rows(ln2_g), rows(ln2_b))

    h = x.astype(F32).reshape(BATCH * SEQ, D_MODEL)
    hm = meta.astype(F32)
    tab = _rope_tables(N_META + jnp.arange(SEQ))
    tab_m = _rope_tables(jnp.arange(N_META))
    for i in range(DEPTH):
        xm = _ffn_ln(hm, i, *ffn1)
        qm, km, vm = _qkv(xm[None], i, *qkv_w, tab_m)
        km, vm_t = pad_rows(km[0]), pad_rows(vm[0]).T
        att_m = _attention_meta(pad_rows(qm[0]), km, vm_t)[:N_META]
        hm8, state, halo = _mix(per_batch(xm), per_batch(att_m), jnp.zeros((BATCH, S5_COLS), F32),
                                jnp.zeros((_HALO, MIX_WIDTH), F32), i, *mix_w)

        xn3 = _ffn_ln(h, i, *ffn1).reshape(BATCH, SEQ, D_MODEL)
        q, k, v = _qkv(xn3, i, *qkv_w, tab)
        att = _attention(q, k, v, km, vm_t)
        h3, _, _ = _mix(xn3, att, state, halo, i, *mix_w)
        h = _ffn_ln(h3.reshape(BATCH * SEQ, D_MODEL), i, *ffn2)
        if i + 1 < DEPTH:
            hm = _ffn_ln(hm8[0], i, *ffn2)
    return h.reshape(BATCH, SEQ, D_MODEL)
```

```python
import functools
import math

import jax
import jax.numpy as jnp
from jax import lax
from jax.experimental import pallas as pl
from jax.experimental.pallas import tpu as pltpu

D_MODEL = 1024
BATCH = 8
SEQ = 2048
DEPTH = 2
N_META = 16
BLOCK_Q = 128
MIX_WIDTH = D_MODEL // 2
HEADS = 8
V_HEAD_DIM = 64
QK_NOPE_DIM = 64
QK_ROPE_DIM = 32
QK_HEAD_DIM = QK_NOPE_DIM + QK_ROPE_DIM
Q_LORA_RANK = 384
KV_LORA_RANK = 256
ROPE_BASE = 10000.0
S5_GROUP = 16
S5_GROUPS = 32
S5_STATE = 64
D_FF = 2816
ALPHA = (2.0 * DEPTH) ** 0.25
LN_EPS = 1e-5
RMS_EPS = 1e-6

HEAD_LANES = 128
QKV_WIDTH = HEADS * HEAD_LANES
TOK_TILE = 512
ATT_TILE = 2 * BLOCK_Q
N_SEQ_TILES = SEQ // ATT_TILE
V_ROWS = V_HEAD_DIM + 16
LOG2E = math.log2(math.e)
S5_BLOCKS = 4
S5_BLOCK_STATES = (S5_GROUPS // S5_BLOCKS) * S5_STATE
S5_COLS = 2 * S5_GROUPS * S5_STATE
MIX_STEPS = 64

_OFF_KR, _OFF_CONV = Q_LORA_RANK + KV_LORA_RANK, Q_LORA_RANK + KV_LORA_RANK + QK_ROPE_DIM

V7X_VMEM_BYTES = 64 * 1024 * 1024
VMEM_LIMIT = V7X_VMEM_BYTES * 7 // 8

F32 = jnp.float32
BF16 = jnp.bfloat16


def _dot(a, b):
    return jnp.dot(a, b, preferred_element_type=F32)


def _const_spec(shape):
    nd = len(shape)
    return pl.BlockSpec(shape, lambda *_: (0,) * nd, pipeline_mode=pl.Buffered(1))


def _layer_spec(stacked, layer):
    nd = stacked.ndim - 1
    return pl.BlockSpec((None,) + stacked.shape[1:], lambda *_: (layer,) + (0,) * nd,
                        pipeline_mode=pl.Buffered(1))


def _layer_norm(y, g, b):
    mu = jnp.mean(y, axis=-1, keepdims=True)
    yc = y - mu
    var = jnp.mean(yc * yc, axis=-1, keepdims=True)
    return yc * lax.rsqrt(var + LN_EPS) * g + b


def _rms_norm(y, g):
    return y * lax.rsqrt(jnp.mean(y * y, axis=-1, keepdims=True) + RMS_EPS) * g


def _sigmoid(y):
    return 1.0 / (1.0 + jnp.exp(-y))


def _gelu_tanh(y):
    return 0.5 * y * (1.0 + jnp.tanh(math.sqrt(2.0 / math.pi) * (y + 0.044715 * (y * y * y))))


_FF_COLS = 256


def _ffn_rows(x, wg_ref, wu_ref, wd_ref, g_ref, b_ref):
    xb = x.astype(BF16)
    hmid = []
    for c0 in range(0, D_FF, _FF_COLS):
        gate = _dot(xb, wg_ref[:, c0:c0 + _FF_COLS])
        hmid.append((gate * _sigmoid(gate) * _dot(xb, wu_ref[:, c0:c0 + _FF_COLS])).astype(BF16))
    hmid = jnp.concatenate(hmid, axis=1)
    return _layer_norm(ALPHA * x + 0.5 * _dot(hmid, wd_ref[...]), g_ref[...], b_ref[...])


def _ffn_ln_kernel(x_ref, wg_ref, wu_ref, wd_ref, g_ref, b_ref, o_ref, *, sub_rows):
    for r in range(x_ref.shape[0] // sub_rows):
        rows = slice(r * sub_rows, (r + 1) * sub_rows)
        o_ref[rows, :] = _ffn_rows(x_ref[rows, :], wg_ref, wu_ref, wd_ref, g_ref, b_ref)


def _ffn_ln(x, layer, wg, wu, wd, g, b):
    rows = x.shape[0]
    tm = min(rows, 2 * TOK_TILE)
    flat = pl.BlockSpec((tm, D_MODEL), lambda i: (i, 0))
    return pl.pallas_call(
        functools.partial(_ffn_ln_kernel, sub_rows=min(tm, TOK_TILE // 2)),
        grid=(rows // tm,),
        in_specs=[flat] + [_layer_spec(w, layer) for w in (wg, wu, wd, g, b)],
        out_specs=flat,
        out_shape=jax.ShapeDtypeStruct((rows, D_MODEL), F32),
        compiler_params=pltpu.CompilerParams(
            dimension_semantics=("arbitrary",), vmem_limit_bytes=VMEM_LIMIT),
        name="ffn_ln",
    )(x, wg, wu, wd, g, b)


_WC_WIDTH = Q_LORA_RANK + KV_LORA_RANK + 2 * HEAD_LANES


def _qkv_kernel(x_ref, wc_ref, gq_ref, gkv_ref, wqa_ref, wqb_ref, wuk_ref, wuv_ref, vones_ref, tab_ref,
                q_ref, k_ref, v_ref, *, transpose_v):
    xb = x_ref[...].astype(BF16)
    c = _dot(xb, wc_ref[...])
    c_q = c[:, :Q_LORA_RANK]
    c_kv = c[:, Q_LORA_RANK:Q_LORA_RANK + KV_LORA_RANK]
    k_r = c[:, _WC_WIDTH - 2 * HEAD_LANES:_WC_WIDTH - HEAD_LANES]
    k_r_rot = c[:, _WC_WIDTH - HEAD_LANES:]
    qn = _rms_norm(c_q, gq_ref[...]).astype(BF16)
    kvn = _rms_norm(c_kv, gkv_ref[...]).astype(BF16)
    cos_q = tab_ref[:, 0 * HEAD_LANES:1 * HEAD_LANES]
    sin_q = tab_ref[:, 1 * HEAD_LANES:2 * HEAD_LANES]
    cos_k = tab_ref[:, 2 * HEAD_LANES:3 * HEAD_LANES]
    sin_k = tab_ref[:, 3 * HEAD_LANES:4 * HEAD_LANES]
    k_rope = k_r * cos_k + k_r_rot * sin_k
    q_a = _dot(qn, wqa_ref[...])
    q_b = _dot(qn, wqb_ref[...])
    k_nope = _dot(kvn, wuk_ref[...])
    for h in range(HEADS):
        sl = slice(h * HEAD_LANES, (h + 1) * HEAD_LANES)
        q_ref[:, sl] = (q_a[:, sl] * cos_q + q_b[:, sl] * sin_q).astype(BF16)
        k_ref[:, sl] = (k_nope[:, sl] + k_rope).astype(BF16)
    v = _dot(kvn, wuv_ref[...]) + vones_ref[...]
    if transpose_v:
        for t in range(x_ref.shape[0] // ATT_TILE):
            v_ref[t] = v[t * ATT_TILE:(t + 1) * ATT_TILE, :].T.astype(BF16)
    else:
        v_ref[...] = v.astype(BF16)


def _qkv(xn, layer, wc, gq, gkv, wqa, wqb, wuk, wuv, vones, tab):
    n_seq, rows, _ = xn.shape
    tile = min(rows, 2 * TOK_TILE)
    transpose_v = tile % ATT_TILE == 0
    seq = lambda w: pl.BlockSpec((None, tile, w), lambda s, j: (s, j, 0))
    seq_shape = jax.ShapeDtypeStruct((n_seq, rows, QKV_WIDTH), BF16)
    if transpose_v:
        v_spec = pl.BlockSpec((None, tile // ATT_TILE, HEADS * V_ROWS, ATT_TILE), lambda s, j: (s, j, 0, 0))
        v_shape = jax.ShapeDtypeStruct((n_seq, rows // ATT_TILE, HEADS * V_ROWS, ATT_TILE), BF16)
    else:
        v_spec = seq(HEADS * V_ROWS)
        v_shape = jax.ShapeDtypeStruct((n_seq, rows, HEADS * V_ROWS), BF16)
    return pl.pallas_call(
        functools.partial(_qkv_kernel, transpose_v=transpose_v),
        grid=(n_seq, rows // tile),
        in_specs=[seq(D_MODEL)]
        + [_layer_spec(w, layer) for w in (wc, gq, gkv, wqa, wqb, wuk, wuv, vones)]
        + [pl.BlockSpec((tile, 4 * HEAD_LANES), lambda s, j: (j, 0))],
        out_specs=[seq(QKV_WIDTH), seq(QKV_WIDTH), v_spec],
        out_shape=[seq_shape, seq_shape, v_shape],
        compiler_params=pltpu.CompilerParams(
            dimension_semantics=("arbitrary", "arbitrary"), vmem_limit_bytes=VMEM_LIMIT),
        name="qkv_proj",
    )(xn, wc, gq, gkv, wqa, wqb, wuk, wuv, vones, tab)


_HEAD_GROUPS = tuple((h,) for h in range(HEADS))


def _scores(q_ref, k_ref, s_scr, smax_scr, slot, c, mask, heads=range(HEADS)):
    k0 = c * ATT_TILE if isinstance(c, int) else pl.multiple_of(c * ATT_TILE, ATT_TILE)
    for h in heads:
        sl = slice(h * HEAD_LANES, (h + 1) * HEAD_LANES)
        s_t = lax.dot_general(
            k_ref[pl.ds(k0, ATT_TILE), sl], q_ref[:, sl], (((1,), (1,)), ((), ())),
            preferred_element_type=F32)
        if mask is not None:
            s_t = jnp.where(mask, s_t, -1e30)
        s_scr[slot, h] = s_t
        smax_scr[slot, h] = jnp.max(s_t, axis=0, keepdims=True)


def _attend(v_ref, s_scr, smax_scr, m_scr, acc_scr, slot, c, heads=range(HEADS)):
    for h in heads:
        m_old = m_scr[h]
        m_new = jnp.maximum(m_old, smax_scr[slot, h])
        p_t = jnp.exp2(s_scr[slot, h] - m_new).astype(BF16)
        pv = _dot(v_ref[c, h * V_ROWS:(h + 1) * V_ROWS, :], p_t)
        acc_scr[h] = jnp.exp2(m_old - m_new) * acc_scr[h] + pv
        m_scr[h] = m_new


def _scores_meta(q_ref, km_ref, sm_scr, m_scr, mask):
    for h in range(HEADS):
        sl = slice(h * HEAD_LANES, (h + 1) * HEAD_LANES)
        s_t = lax.dot_general(
            km_ref[:, sl], q_ref[:, sl], (((1,), (1,)), ((), ())), preferred_element_type=F32)
        s_t = jnp.where(mask, s_t, -1e30)
        sm_scr[h] = s_t
        m_scr[h] = jnp.max(s_t, axis=0, keepdims=True)


def _attend_meta(vm_ref, sm_scr, m_scr, acc_scr):
    for h in range(HEADS):
        p_t = jnp.exp2(sm_scr[h] - m_scr[h]).astype(BF16)
        acc_scr[h] = _dot(vm_ref[h * V_ROWS:(h + 1) * V_ROWS, :], p_t)


def _attn_out(acc_scr):
    o_t = jnp.concatenate(
        [acc_scr[h, 0:V_HEAD_DIM, :] / acc_scr[h, V_HEAD_DIM:V_HEAD_DIM + 1, :] for h in range(HEADS)],
        axis=0)
    return o_t.T.astype(BF16)


def _attn_kernel(q_ref, k_ref, v_ref, km_ref, vm_ref, o_ref, s_scr, smax_scr, sm_scr, m_scr, acc_scr):
    i = pl.program_id(1)
    causal = (lax.broadcasted_iota(jnp.int32, (ATT_TILE, ATT_TILE), 0)
              <= lax.broadcasted_iota(jnp.int32, (ATT_TILE, ATT_TILE), 1))
    meta_mask = lax.broadcasted_iota(jnp.int32, (BLOCK_Q, ATT_TILE), 0) < N_META

    scores = functools.partial(_scores, q_ref, k_ref, s_scr, smax_scr)
    attend = functools.partial(_attend, v_ref, s_scr, smax_scr, m_scr, acc_scr)
    def overlapped(next_slot, next_tile, mask, slot, tile):
        for hs in _HEAD_GROUPS:
            scores(next_slot, next_tile, mask, hs)
            attend(slot, tile, hs)

    _scores_meta(q_ref, km_ref, sm_scr, m_scr, meta_mask)
    scores(0, 0, causal | (i > 0))
    _attend_meta(vm_ref, sm_scr, m_scr, acc_scr)

    def two_tiles(pp, carry):
        p = 1 + 2 * pp
        overlapped(1, p, None, 0, p - 1)
        overlapped(0, p + 1, None, 1, p)
        return carry

    lax.fori_loop(0, (i - 1) // 2, two_tiles, 0)

    @pl.when(i == 0)
    def _():
        attend(0, 0)

    @pl.when(i % 2 == 1)
    def _():
        overlapped(1, i, causal, 0, i - 1)
        attend(1, i)

    @pl.when((i % 2 == 0) & (i > 0))
    def _():
        overlapped(1, i - 1, None, 0, i - 2)
        overlapped(0, i, causal, 1, i - 1)
        attend(0, i)

    o_ref[...] = _attn_out(acc_scr)


def _attention(q, k, v_t, k_meta, v_meta_t):
    return pl.pallas_call(
        _attn_kernel,
        grid=(BATCH, N_SEQ_TILES),
        in_specs=[
            pl.BlockSpec((None, ATT_TILE, QKV_WIDTH), lambda s, i: (s, i, 0)),
            pl.BlockSpec((None, SEQ, QKV_WIDTH), lambda s, i: (s, 0, 0)),
            pl.BlockSpec((None, N_SEQ_TILES, HEADS * V_ROWS, ATT_TILE), lambda s, i: (s, 0, 0, 0)),
            _const_spec((BLOCK_Q, QKV_WIDTH)),
            _const_spec((HEADS * V_ROWS, BLOCK_Q)),
        ],
        out_specs=pl.BlockSpec((None, ATT_TILE, MIX_WIDTH), lambda s, i: (s, i, 0)),
        out_shape=jax.ShapeDtypeStruct((BATCH, SEQ, MIX_WIDTH), BF16),
        scratch_shapes=[pltpu.VMEM((2, HEADS, ATT_TILE, ATT_TILE), F32),
                        pltpu.VMEM((2, HEADS, 1, ATT_TILE), F32),
                        pltpu.VMEM((HEADS, BLOCK_Q, ATT_TILE), F32),
                        pltpu.VMEM((HEADS, 1, ATT_TILE), F32),
                        pltpu.VMEM((HEADS, V_ROWS, ATT_TILE), F32)],
        compiler_params=pltpu.CompilerParams(
            dimension_semantics=("arbitrary", "arbitrary"), vmem_limit_bytes=VMEM_LIMIT),
        name="mla_attention",
    )(q, k, v_t, k_meta, v_meta_t)


def _attn_meta_kernel(q_ref, km_ref, vm_ref, o_ref, sm_scr, m_scr, acc_scr):
    k_off = lax.broadcasted_iota(jnp.int32, (BLOCK_Q, BLOCK_Q), 0)
    q_off = lax.broadcasted_iota(jnp.int32, (BLOCK_Q, BLOCK_Q), 1)
    _scores_meta(q_ref, km_ref, sm_scr, m_scr, (k_off < N_META) & (k_off <= q_off))
    _attend_meta(vm_ref, sm_scr, m_scr, acc_scr)
    o_ref[...] = _attn_out(acc_scr)


def _attention_meta(q_meta, k_meta, v_meta_t):
    return pl.pallas_call(
        _attn_meta_kernel,
        out_shape=jax.ShapeDtypeStruct((BLOCK_Q, MIX_WIDTH), BF16),
        scratch_shapes=[pltpu.VMEM((HEADS, BLOCK_Q, BLOCK_Q), F32),
                        pltpu.VMEM((HEADS, 1, BLOCK_Q), F32),
                        pltpu.VMEM((HEADS, V_ROWS, BLOCK_Q), F32)],
        name="mla_attention_meta",
    )(q_meta, k_meta, v_meta_t)


_HALO = 2 * BATCH
_D_SLABS = D_MODEL // HEAD_LANES
_MERGE_COLS = 256
_LN_GROUPS = 2


def _to_time_major(src_ref, slab_ref, width, steps):
    n = width // HEAD_LANES
    for b in range(BATCH):
        for j in range(n):
            slab_ref[j, pl.ds(b, steps, stride=BATCH), :] = (
                src_ref[b, :, j * HEAD_LANES:(j + 1) * HEAD_LANES].astype(F32))
    return jnp.concatenate([slab_ref[j] for j in range(n)], axis=1)


def _mix_kernel(x_ref, att_ref, st0_ref, halo0_ref, win_ref, cw_ref, cb_ref, cwo_ref, bblk_ref, ar_ref,
                ai_ref, cblk_ref, d_ref, wglu_ref, bglu_ref, swo_ref, mwo_ref, wo_ref, g_ref, b_ref,
                out_ref, st_out_ref, halo_out_ref, bu_scr, st_scr, cbuf_scr, slab_ref, *, steps):
    tm = steps * BATCH
    W = MIX_WIDTH

    @pl.when(pl.program_id(0) == 0)
    def _():
        st_scr[...] = st0_ref[...]
        cbuf_scr[0:_HALO, :] = halo0_ref[...]

    x = _to_time_major(x_ref, slab_ref, D_MODEL, steps)
    xb = x.astype(BF16)

    us = _dot(xb, win_ref[:, 3 * W:4 * W])
    usb = us.astype(BF16)
    nb = S5_BLOCK_STATES
    for blk in range(S5_BLOCKS):
        bu_scr[:, 2 * nb * blk:2 * nb * (blk + 1)] = _dot(
            usb[:, blk * 128:(blk + 1) * 128], bblk_ref[blk])
    for blk in range(S5_BLOCKS):
        c_re = 2 * nb * blk
        c_im = c_re + nb
        a_re = ar_ref[blk]
        a_im = ai_ref[blk]

        def step(t, carry, c_re=c_re, c_im=c_im, a_re=a_re, a_im=a_im):
            s_re, s_im = carry
            r0 = pl.multiple_of(t * BATCH, BATCH)
            n_re = a_re * s_re - a_im * s_im + bu_scr[pl.ds(r0, BATCH), c_re:c_re + nb]
            n_im = a_re * s_im + a_im * s_re + bu_scr[pl.ds(r0, BATCH), c_im:c_im + nb]
            bu_scr[pl.ds(r0, BATCH), c_re:c_re + nb] = n_re
            bu_scr[pl.ds(r0, BATCH), c_im:c_im + nb] = n_im
            return n_re, n_im

        s_re, s_im = lax.fori_loop(
            0, steps, step, (st_scr[:, c_re:c_re + nb], st_scr[:, c_im:c_im + nb]), unroll=True)
        st_scr[:, c_re:c_re + nb] = s_re
        st_scr[:, c_im:c_im + nb] = s_im

    pc = _dot(xb, win_ref[:, 0:3 * W])
    u = pc[:, 2 * W:3 * W] * pc[:, 0:W]
    cbuf_scr[_HALO:_HALO + tm, :] = u
    y = (cb_ref[...] + cw_ref[0:1, :] * cbuf_scr[0:tm, :]
         + cw_ref[1:2, :] * cbuf_scr[BATCH:BATCH + tm, :] + cw_ref[2:3, :] * u)
    cbuf_scr[0:_HALO, :] = u[tm - _HALO:, :]
    z_b = (pc[:, W:2 * W] * y).astype(BF16)

    y = jnp.concatenate(
        [_dot(bu_scr[:, 2 * nb * blk:2 * nb * (blk + 1)].astype(BF16), cblk_ref[blk])
         for blk in range(S5_BLOCKS)], axis=1)
    y = _gelu_tanh(y + d_ref[...] * us)

    att = _to_time_major(att_ref, slab_ref, W, steps).astype(BF16)
    g0 = 4 * W
    mixed_cols, gate_c_cols = [], []
    for c0 in range(0, D_MODEL, _MERGE_COLS):
        cols = slice(c0, c0 + _MERGE_COLS)
        gate = lambda k: _sigmoid(_dot(xb, win_ref[:, g0 + k * D_MODEL + c0:g0 + k * D_MODEL + c0 + _MERGE_COLS]))
        mixed_cols.append(gate(0) * _dot(att, mwo_ref[:, cols]) + gate(1) * _dot(z_b, cwo_ref[:, cols]))
        gate_c_cols.append(gate(2))
        if c0 == D_MODEL // 2 - _MERGE_COLS:
            y = y * _sigmoid(_dot(y.astype(BF16), wglu_ref[...]) + bglu_ref[...])
    mixed = jnp.concatenate(mixed_cols, axis=1) + jnp.concatenate(gate_c_cols, axis=1) * _dot(
        y.astype(BF16), swo_ref[...])

    group = tm // _LN_GROUPS
    for r in range(_LN_GROUPS):
        rows = slice(r * group, (r + 1) * group)
        z = _layer_norm(ALPHA * x[rows] + _dot(mixed[rows].astype(BF16), wo_ref[...]),
                        g_ref[...], b_ref[...])
        for j in range(_D_SLABS):
            slab_ref[j, rows, :] = z[:, j * HEAD_LANES:(j + 1) * HEAD_LANES]
    for b in range(BATCH):
        for j in range(_D_SLABS):
            out_ref[b, :, j * HEAD_LANES:(j + 1) * HEAD_LANES] = slab_ref[
                j, pl.ds(b, steps, stride=BATCH), :]
    st_out_ref[...] = st_scr[...]
    halo_out_ref[...] = cbuf_scr[0:_HALO, :]


def _mix(xn, att, st0, halo0, layer, *weights):
    t_len = xn.shape[1]
    steps = min(t_len, MIX_STEPS)
    tm = steps * BATCH
    tile = lambda w: pl.BlockSpec((BATCH, steps, w), lambda i: (0, i, 0))
    return pl.pallas_call(
        functools.partial(_mix_kernel, steps=steps),
        grid=(t_len // steps,),
        in_specs=[tile(D_MODEL), tile(MIX_WIDTH), _const_spec(st0.shape), _const_spec(halo0.shape)]
        + [_layer_spec(w, layer) for w in weights],
        out_specs=[tile(D_MODEL), pl.BlockSpec((BATCH, S5_COLS), lambda i: (0, 0)),
                   pl.BlockSpec((_HALO, MIX_WIDTH), lambda i: (0, 0))],
        out_shape=[jax.ShapeDtypeStruct((BATCH, t_len, D_MODEL), F32),
                   jax.ShapeDtypeStruct((BATCH, S5_COLS), F32),
                   jax.ShapeDtypeStruct((_HALO, MIX_WIDTH), F32)],
        scratch_shapes=[
            pltpu.VMEM((tm, S5_COLS), F32),
            pltpu.VMEM((BATCH, S5_COLS), F32),
            pltpu.VMEM((tm + _HALO, MIX_WIDTH), F32),
            pltpu.VMEM((_D_SLABS, tm, HEAD_LANES), F32),
        ],
        compiler_params=pltpu.CompilerParams(
            dimension_semantics=("arbitrary",), vmem_limit_bytes=VMEM_LIMIT),
        name="mixers_merge",
    )(xn, att, st0, halo0, *weights)


def _rope_tables(pos):
    n = pos.shape[0]
    inv_freq = ROPE_BASE ** (-jnp.arange(0, QK_ROPE_DIM, 2, dtype=F32) / QK_ROPE_DIM)
    ang = pos.astype(F32)[:, None] * inv_freq[None, :]
    cos2 = jnp.tile(jnp.cos(ang), (1, 2))
    sin2 = jnp.tile(jnp.sin(ang), (1, 2))
    zn = jnp.zeros((n, QK_NOPE_DIM), F32)
    zt = jnp.zeros((n, HEAD_LANES - QK_HEAD_DIM), F32)
    scale = QK_HEAD_DIM ** -0.5 * LOG2E
    cos_q = scale * jnp.concatenate([jnp.ones_like(zn), cos2, zt], axis=1)
    sin_q = scale * jnp.concatenate([zn, sin2, zt], axis=1)
    cos_k = jnp.concatenate([zn, cos2, zt], axis=1)
    sin_k = jnp.concatenate([zn, sin2, zt], axis=1)
    return jnp.concatenate([cos_q, sin_q, cos_k, sin_k], axis=1)


def _rot_half_cols(w):
    half = QK_ROPE_DIM // 2
    return jnp.concatenate([-w[..., half:], w[..., :half]], axis=-1)


def _qkv_weights(w_in, w_uq, w_ukv):
    zeros = lambda *shape: jnp.zeros(shape, BF16)
    kr = w_in[:, _OFF_KR:_OFF_CONV]
    tail = HEAD_LANES - QK_HEAD_DIM
    wc = jnp.concatenate([
        w_in[:, :_OFF_KR],
        zeros(D_MODEL, QK_NOPE_DIM), kr, zeros(D_MODEL, tail),
        zeros(D_MODEL, QK_NOPE_DIM), _rot_half_cols(kr), zeros(D_MODEL, tail)], axis=1)
    uq = w_uq.reshape(Q_LORA_RANK, HEADS, QK_HEAD_DIM)
    wqa = jnp.concatenate([uq, zeros(Q_LORA_RANK, HEADS, tail)], axis=-1)
    wqb = jnp.concatenate([zeros(Q_LORA_RANK, HEADS, QK_NOPE_DIM), _rot_half_cols(uq[..., QK_NOPE_DIM:]),
                           zeros(Q_LORA_RANK, HEADS, tail)], axis=-1)
    ukv = w_ukv.reshape(KV_LORA_RANK, HEADS, QK_NOPE_DIM + V_HEAD_DIM)
    wuk = jnp.concatenate([ukv[..., :QK_NOPE_DIM], zeros(KV_LORA_RANK, HEADS, HEAD_LANES - QK_NOPE_DIM)],
                          axis=-1)
    wuv = jnp.concatenate([ukv[..., QK_NOPE_DIM:], zeros(KV_LORA_RANK, HEADS, V_ROWS - V_HEAD_DIM)], axis=-1)
    vones = jnp.zeros((HEADS, V_ROWS), F32).at[:, V_HEAD_DIM].set(1.0).reshape(1, HEADS * V_ROWS)
    flat = lambda w: w.reshape(w.shape[0], -1)
    return wc, flat(wqa), flat(wqb), flat(wuk), flat(wuv), vones


def _s5_weights(a_re, a_im, log_dt, b_re, b_im, c_re, c_im):
    dt = jnp.exp(log_dt)[:, None]
    mag = jnp.exp(dt * a_re)
    ab_re, ab_im = mag * jnp.cos(dt * a_im), mag * jnp.sin(dt * a_im)
    den = a_re * a_re + a_im * a_im
    nr, ni = ab_re - 1.0, ab_im
    coef_re = (nr * a_re + ni * a_im) / den
    coef_im = (ni * a_re - nr * a_im) / den
    bb_re = coef_re[..., None] * b_re - coef_im[..., None] * b_im
    bb_im = coef_re[..., None] * b_im + coef_im[..., None] * b_re
    gpb = S5_GROUPS // S5_BLOCKS
    eye = jnp.eye(gpb, dtype=F32)

    def in_blocks(bb):
        t = bb.transpose(0, 2, 1).reshape(S5_BLOCKS, gpb, S5_GROUP, S5_STATE)
        return jnp.einsum('bghn,gk->bghkn', t, eye).reshape(S5_BLOCKS, gpb * S5_GROUP, gpb * S5_STATE)

    def out_blocks(cc):
        t = cc.reshape(S5_BLOCKS, gpb, S5_GROUP, S5_STATE)
        return jnp.einsum('bghn,gk->bgnkh', t, eye).reshape(S5_BLOCKS, gpb * S5_STATE, gpb * S5_GROUP)

    bblk = jnp.concatenate([in_blocks(bb_re), in_blocks(bb_im)], axis=2).astype(BF16)
    cblk = jnp.concatenate([out_blocks(c_re), out_blocks(-c_im)], axis=1).astype(BF16)
    bcast = lambda a: jnp.broadcast_to(
        a.reshape(S5_BLOCKS, 1, S5_BLOCK_STATES), (S5_BLOCKS, BATCH, S5_BLOCK_STATES))
    return bblk, bcast(ab_re), bcast(ab_im), cblk


def kernel(x, meta, ffn1_w_gate, ffn1_w_up, ffn1_w_down, ln1_g, ln1_b, w_in, mla_q_norm_g, mla_w_uq, mla_kv_norm_g, mla_w_ukv, mla_w_o, conv_w, conv_b, conv_w_out, s5_a_re, s5_a_im, s5_log_dt, s5_b_re, s5_b_im, s5_c_re, s5_c_im, s5_d, s5_w_glu, s5_b_glu, s5_w_out, w_o, ln2_g, ln2_b, ffn2_w_gate, ffn2_w_up, ffn2_w_down, ln3_g, ln3_b):
    rows = lambda v: v.reshape(DEPTH, 1, -1).astype(F32)
    bf = lambda w: w.astype(BF16)
    pad_rows = lambda a: jnp.pad(a, ((0, BLOCK_Q - N_META), (0, 0)))
    per_batch = lambda a: jnp.broadcast_to(a[None], (BATCH,) + a.shape)
    ffn1 = (bf(ffn1_w_gate), bf(ffn1_w_up), bf(ffn1_w_down), rows(ln1_g), rows(ln1_b))
    ffn2 = (bf(ffn2_w_gate), bf(ffn2_w_up), bf(ffn2_w_down), rows(ln3_g), rows(ln3_b))
    w_in_b = bf(w_in)
    wc, wqa, wqb, wuk, wuv, vones = jax.vmap(_qkv_weights)(w_in_b, bf(mla_w_uq), bf(mla_w_ukv))
    qkv_w = (wc, rows(mla_q_norm_g), rows(mla_kv_norm_g), wqa, wqb, wuk, wuv, vones)
    bblk, ar, ai, cblk = jax.vmap(_s5_weights)(s5_a_re, s5_a_im, s5_log_dt, s5_b_re, s5_b_im, s5_c_re, s5_c_im)
    mix_w = (w_in_b[:, :, _OFF_CONV:], conv_w.astype(F32), rows(conv_b), bf(conv_w_out),
             bblk, ar, ai, cblk, rows(s5_d), bf(s5_w_glu), rows(s5_b_glu), bf(s5_w_out),
             bf(mla_w_o), bf(w_o), rows(ln2_g), rows(ln2_b))

    h = x.astype(F32).reshape(BATCH * SEQ, D_MODEL)
    hm = meta.astype(F32)
    tab = _rope_tables(N_META + jnp.arange(SEQ))
    tab_m = _rope_tables(jnp.arange(N_META))
    for i in range(DEPTH):
        xm = _ffn_ln(hm, i, *ffn1)
        qm, km, vm = _qkv(xm[None], i, *qkv_w, tab_m)
        km, vm_t = pad_rows(km[0]), pad_rows(vm[0]).T
        att_m = _attention_meta(pad_rows(qm[0]), km, vm_t)[:N_META]
        hm8, state, halo = _mix(per_batch(xm), per_batch(att_m), jnp.zeros((BATCH, S5_COLS), F32),
                                jnp.zeros((_HALO, MIX_WIDTH), F32), i, *mix_w)

        xn3 = _ffn_ln(h, i, *ffn1).reshape(BATCH, SEQ, D_MODEL)
        q, k, v = _qkv(xn3, i, *qkv_w, tab)
        att = _attention(q, k, v, km, vm_t)
        h3, _, _ = _mix(xn3, att, state, halo, i, *mix_w)
        h = _ffn_ln(h3.reshape(BATCH * SEQ, D_MODEL), i, *ffn2)
        if i + 1 < DEPTH:
            hm = _ffn_ln(hm8[0], i, *ffn2)
    return h.reshape(BATCH, SEQ, D_MODEL)
```
